```python
import math
import jax, jax.numpy as jnp
from jax import lax
import numpy as np

D_MODEL = 2048
BATCH = 8
SEQ = 2048
DEPTH = 2

DA_HEADS = 8
DA_HEAD_DIM = 64
DA_WIDTH = DA_HEADS * 2 * DA_HEAD_DIM
DA_Q_BLOCK = 128
SSM_HEADS = 16
SSM_HEAD_DIM = 64
SSM_INNER = SSM_HEADS * SSM_HEAD_DIM
SSM_GROUPS = 2
SSM_STATE = 128
SSM_CONV = 4
SSM_CHUNK = 128
SSM_CONV_CH = SSM_INNER + 2 * SSM_GROUPS * SSM_STATE
GDN_HEADS = 8
GDN_HEAD_DIM = 128
GDN_WIDTH = GDN_HEADS * GDN_HEAD_DIM
GDN_CONV = 4
GDN_CHUNK = 64
D_FF = 5632
FFN_CONV = 3
N_BRANCH = 3
RMS_EPS = 1e-6

IN_SIZES = (DA_WIDTH, DA_WIDTH, DA_WIDTH,
            SSM_INNER, SSM_CONV_CH, SSM_HEADS,
            3 * GDN_WIDTH, GDN_WIDTH, GDN_HEADS, GDN_HEADS,
            N_BRANCH * D_MODEL)
IN_SPLITS = tuple(sum(IN_SIZES[:i + 1]) for i in range(len(IN_SIZES) - 1))
IN_COLS = sum(IN_SIZES)
BRANCH_ROWS = DA_WIDTH + SSM_INNER + GDN_WIDTH

kernel_name = "hybrid_diffattn_mamba2_gdn_convffn"


def rms_norm(x, w, eps=RMS_EPS):
    xf = x.astype(jnp.float32)
    y = xf * lax.rsqrt(jnp.mean(xf * xf, axis=-1, keepdims=True) + eps)
    return (y * w.astype(jnp.float32)).astype(x.dtype)


def l2_normalize(x, eps=1e-6):
    xf = x.astype(jnp.float32)
    return xf * lax.rsqrt(jnp.sum(xf * xf, axis=-1, keepdims=True) + eps)


def causal_dwconv(x, w, b=None):
    K, C = w.shape
    y = lax.conv_general_dilated(x, w[:, None, :].astype(x.dtype), window_strides=(1,),
                                 padding=[(K - 1, 0)], dimension_numbers=('NWC', 'WIO', 'NWC'),
                                 feature_group_count=C)
    if b is not None:
        y = y + b.astype(x.dtype)
    return y


def alibi_slopes(n_heads):
    return jnp.asarray([2.0 ** (-8.0 * (h + 1) / n_heads) for h in range(n_heads)], dtype=jnp.float32)


def diff_attention(q_raw, k_raw, v_raw, lam_params, subln_w, lambda_init):
    Bsz, L, _ = q_raw.shape
    H, d = DA_HEADS, DA_HEAD_DIM
    q = q_raw.reshape(Bsz, L, H, 2, d) * (d ** -0.5)
    k = k_raw.reshape(Bsz, L, H, 2, d)
    v = v_raw.reshape(Bsz, L, H, 2 * d)
    lp = lam_params.astype(jnp.float32)
    lam = jnp.exp(jnp.sum(lp[0] * lp[1])) - jnp.exp(jnp.sum(lp[2] * lp[3])) + lambda_init
    slopes = alibi_slopes(H)
    kpos = jnp.arange(L, dtype=jnp.int32)
    nb = L // DA_Q_BLOCK
    q_blocks = jnp.moveaxis(q.reshape(Bsz, nb, DA_Q_BLOCK, H, 2, d), 1, 0)
    starts = jnp.arange(nb, dtype=jnp.int32) * DA_Q_BLOCK

    def block(args):
        qb, start = args
        qpos = start + jnp.arange(DA_Q_BLOCK, dtype=jnp.int32)
        dist = (qpos[:, None] - kpos[None, :]).astype(jnp.float32)
        s = jnp.einsum('bqhid,bkhid->bhiqk', qb, k).astype(jnp.float32)
        s = s - slopes[None, :, None, None, None] * dist
        s = jnp.where(dist >= 0, s, -jnp.inf)
        p = jax.nn.softmax(s, axis=-1)
        a = (p[:, :, 0] - lam * p[:, :, 1]).astype(v.dtype)
        return jnp.einsum('bhqk,bkhe->bqhe', a, v)

    o = lax.map(block, (q_blocks, starts))
    o = jnp.moveaxis(o, 0, 1).reshape(Bsz, L, H, 2 * d)
    o = rms_norm(o, subln_w) * (1.0 - lambda_init)
    return o.reshape(Bsz, L, DA_WIDTH)


def mamba2_ssd(xbc_raw, z, dt_raw, conv_w, conv_b, dt_bias, a_log, d_skip, norm_w):
    Bsz, L, _ = xbc_raw.shape
    f32 = jnp.float32
    Q, H, P, G, N = SSM_CHUNK, SSM_HEADS, SSM_HEAD_DIM, SSM_GROUPS, SSM_STATE
    nc = L // Q
    xbc = jax.nn.silu(causal_dwconv(xbc_raw, conv_w, conv_b)).astype(f32)
    xs, Bm, Cm = jnp.split(xbc, [SSM_INNER, SSM_INNER + G * N], axis=-1)
    x = xs.reshape(Bsz, nc, Q, H, P)
    Bm = jnp.repeat(Bm.reshape(Bsz, nc, Q, G, N), H // G, axis=3)
    Cm = jnp.repeat(Cm.reshape(Bsz, nc, Q, G, N), H // G, axis=3)
    dt = jax.nn.softplus(dt_raw.astype(f32) + dt_bias.astype(f32)).reshape(Bsz, nc, Q, H)
    A = -jnp.exp(a_log.astype(f32))
    a_cs = jnp.cumsum(dt * A, axis=2)
    xdt = x * dt[..., None]
    causal = jnp.tril(jnp.ones((Q, Q), dtype=bool))
    seg = a_cs[:, :, :, None, :] - a_cs[:, :, None, :, :]
    decay = jnp.exp(jnp.where(causal[None, None, :, :, None], seg, -jnp.inf))
    scores = jnp.einsum('bclhn,bcshn->bclsh', Cm, Bm) * decay
    y_diag = jnp.einsum('bclsh,bcshp->bclhp', scores, xdt)
    decay_to_end = jnp.exp(a_cs[:, :, -1:, :] - a_cs)
    states = jnp.einsum('bclhn,bclh,bclhp->bchpn', Bm, decay_to_end, xdt)
    chunk_decay = jnp.exp(a_cs[:, :, -1, :])

    def step(S, inp):
        st, dec = inp
        return S * dec[:, :, None, None] + st, S

    _, prev = lax.scan(step, jnp.zeros((Bsz, H, P, N), f32),
                       (jnp.moveaxis(states, 1, 0), jnp.moveaxis(chunk_decay, 1, 0)))
    prev = jnp.moveaxis(prev, 0, 1)
    y_off = jnp.einsum('bclhn,bchpn,bclh->bclhp', Cm, prev, jnp.exp(a_cs))
    y = y_diag + y_off + x * d_skip.astype(f32)[:, None]
    y = y.reshape(Bsz, L, SSM_INNER) * jax.nn.silu(z.astype(f32))
    y = rms_norm(y.reshape(Bsz, L, G, SSM_INNER // G),
                 norm_w.reshape(G, SSM_INNER // G)).reshape(Bsz, L, SSM_INNER)
    return y.astype(xbc_raw.dtype)


def gated_deltanet(qkv_raw, z, b_raw, a_raw, conv_w, dt_bias, a_log, norm_w):
    Bsz, L, _ = qkv_raw.shape
    f32 = jnp.float32
    C, H, D = GDN_CHUNK, GDN_HEADS, GDN_HEAD_DIM
    nc = L // C
    qkv = jax.nn.silu(causal_dwconv(qkv_raw, conv_w))
    q, k, v = jnp.split(qkv, 3, axis=-1)
    q = l2_normalize(q.reshape(Bsz, L, H, D)) * (D ** -0.5)
    k = l2_normalize(k.reshape(Bsz, L, H, D))
    v = v.reshape(Bsz, L, H, D).astype(f32)
    beta = jax.nn.sigmoid(b_raw.astype(f32))
    g = -jnp.exp(a_log.astype(f32)) * jax.nn.softplus(a_raw.astype(f32) + dt_bias.astype(f32))

    def chunked(t):
        t = t.reshape((Bsz, nc, C, H) + t.shape[3:])
        return jnp.moveaxis(t, 3, 1)

    q, k, v, beta, g = chunked(q), chunked(k), chunked(v), chunked(beta), chunked(g)
    g = jnp.cumsum(g, axis=-1)
    k_beta = k * beta[..., None]
    v_beta = v * beta[..., None]
    incl = jnp.tril(jnp.ones((C, C), dtype=bool))
    strict = jnp.tril(jnp.ones((C, C), dtype=bool), k=-1)
    gdiff = g[..., :, None] - g[..., None, :]
    decay = jnp.where(incl, jnp.exp(jnp.where(incl, gdiff, 0.0)), 0.0)
    A = jnp.where(strict, jnp.einsum('bhcld,bhcsd->bhcls', k_beta, k) * decay, 0.0)
    lhs = A + jnp.eye(C, dtype=f32)
    rhs = jnp.concatenate([v_beta, k_beta * jnp.exp(g)[..., None]], axis=-1)
    sol = lax.linalg.triangular_solve(lhs, rhs, left_side=True, lower=True, unit_diagonal=True)
    u, w = jnp.split(sol, 2, axis=-1)

    def step(S, inp):
        qc, kc, uc, wc, gc, dc = inp
        attn = jnp.einsum('bhld,bhsd->bhls', qc, kc) * dc
        v_new = uc - jnp.einsum('bhld,bhdv->bhlv', wc, S)
        o = (jnp.einsum('bhld,bhdv->bhlv', qc * jnp.exp(gc)[..., None], S)
             + jnp.einsum('bhls,bhsv->bhlv', attn, v_new))
        g_last = gc[..., -1]
        S = (S * jnp.exp(g_last)[..., None, None]
             + jnp.einsum('bhld,bhlv->bhdv', kc * jnp.exp(g_last[..., None] - gc)[..., None], v_new))
        return S, o

    xs = tuple(jnp.moveaxis(t, 2, 0) for t in (q, k, u, w, g, decay))
    _, o = lax.scan(step, jnp.zeros((Bsz, H, D, D), f32), xs)
    o = jnp.transpose(o, (1, 0, 3, 2, 4)).reshape(Bsz, L, H, D)
    o = rms_norm(o, norm_w) * jax.nn.silu(z.reshape(Bsz, L, H, D).astype(f32))
    return o.reshape(Bsz, L, GDN_WIDTH).astype(qkv_raw.dtype)


def setup_inputs(seed: int = 0) -> dict:
    key = jax.random.key(seed)
    ks = jax.random.split(key, 24)
    f32 = jnp.float32

    def nrm(k, shape, scale):
        return jax.random.normal(k, shape, f32) * scale

    def gain(k, shape):
        return 1.0 + 0.01 * jax.random.normal(k, shape, f32)

    def dt_bias_init(k, n):
        dt = jnp.exp(jax.random.uniform(k, (DEPTH, n), f32, math.log(1e-3), math.log(1e-1)))
        return dt + jnp.log(-jnp.expm1(-dt))

    def a_log_init(k, n):
        return jnp.log(jax.random.uniform(k, (DEPTH, n), f32, 1.0, 16.0))

    return {
        "x": nrm(ks[0], (BATCH, SEQ, D_MODEL), 1.0),
        "norm_mix": gain(ks[1], (DEPTH, D_MODEL)),
        "w_in": nrm(ks[2], (DEPTH, D_MODEL, IN_COLS), D_MODEL ** -0.5),
        "da_lambda": nrm(ks[3], (DEPTH, 4, DA_HEAD_DIM), 0.1),
        "da_subln": gain(ks[4], (DEPTH, 2 * DA_HEAD_DIM)),
        "ssm_conv_w": nrm(ks[5], (DEPTH, SSM_CONV, SSM_CONV_CH), SSM_CONV ** -0.5),
        "ssm_conv_b": nrm(ks[6], (DEPTH, SSM_CONV_CH), 0.02),
        "ssm_dt_bias": dt_bias_init(ks[7], SSM_HEADS),
        "ssm_a_log": a_log_init(ks[8], SSM_HEADS),
        "ssm_d": gain(ks[9], (DEPTH, SSM_HEADS)),
        "ssm_norm": gain(ks[10], (DEPTH, SSM_INNER)),
        "gdn_conv_w": nrm(ks[11], (DEPTH, GDN_CONV, 3 * GDN_WIDTH), GDN_CONV ** -0.5),
        "gdn_dt_bias": dt_bias_init(ks[12], GDN_HEADS),
        "gdn_a_log": a_log_init(ks[13], GDN_HEADS),
        "gdn_norm": gain(ks[14], (DEPTH, GDN_HEAD_DIM)),
        "w_branch": nrm(ks[15], (DEPTH, BRANCH_ROWS, D_MODEL), DA_WIDTH ** -0.5),
        "w_out": nrm(ks[16], (DEPTH, D_MODEL, D_MODEL), D_MODEL ** -0.5),
        "norm_ffn": gain(ks[17], (DEPTH, D_MODEL)),
        "ffn_up": nrm(ks[18], (DEPTH, D_MODEL, 2 * D_FF), D_MODEL ** -0.5),
        "ffn_conv_w": nrm(ks[19], (DEPTH, FFN_CONV, 2 * D_FF), FFN_CONV ** -0.5),
        "ffn_conv_b": nrm(ks[20], (DEPTH, 2 * D_FF), 0.02),
        "ffn_down": nrm(ks[21], (DEPTH, D_FF, D_MODEL), D_FF ** -0.5),
        "norm_final": gain(ks[22], (D_MODEL,)),
    }


def reference(x, norm_mix, w_in, da_lambda, da_subln, ssm_conv_w, ssm_conv_b, ssm_dt_bias,
              ssm_a_log, ssm_d, ssm_norm, gdn_conv_w, gdn_dt_bias, gdn_a_log, gdn_norm,
              w_branch, w_out, norm_ffn, ffn_up, ffn_conv_w, ffn_conv_b, ffn_down, norm_final):
    Bsz, L, _ = x.shape
    h = x
    for l in range(DEPTH):
        xn = rms_norm(h, norm_mix[l])
        (da_q, da_k, da_v, ssm_z, ssm_xbc, ssm_dt, gdn_qkv, gdn_z, gdn_b, gdn_a,
         gates) = jnp.split(xn @ w_in[l], IN_SPLITS, axis=-1)
        lambda_init = 0.8 - 0.6 * math.exp(-0.3 * l)
        o_da = diff_attention(da_q, da_k, da_v, da_lambda[l], da_subln[l], lambda_init)
        o_ssm = mamba2_ssd(ssm_xbc, ssm_z, ssm_dt, ssm_conv_w[l], ssm_conv_b[l], ssm_dt_bias[l],
                           ssm_a_log[l], ssm_d[l], ssm_norm[l])
        o_gdn = gated_deltanet(gdn_qkv, gdn_z, gdn_b, gdn_a, gdn_conv_w[l], gdn_dt_bias[l],
                               gdn_a_log[l], gdn_norm[l])
        wb = w_branch[l]
        gate = jax.nn.sigmoid(gates.reshape(Bsz, L, N_BRANCH, D_MODEL).astype(jnp.float32)).astype(h.dtype)
        merged = (gate[:, :, 0] * (o_da @ wb[:DA_WIDTH])
                  + gate[:, :, 1] * (o_ssm @ wb[DA_WIDTH:DA_WIDTH + SSM_INNER])
                  + gate[:, :, 2] * (o_gdn @ wb[DA_WIDTH + SSM_INNER:]))
        h = h + merged @ w_out[l]
        hn = rms_norm(h, norm_ffn[l])
        u = causal_dwconv(hn @ ffn_up[l], ffn_conv_w[l], ffn_conv_b[l])
        u_gate, u_val = jnp.split(u, 2, axis=-1)
        h = h + (jax.nn.silu(u_gate) * u_val) @ ffn_down[l]
    return rms_norm(h, norm_final)
```

```python
import functools
import math

import jax
import jax.numpy as jnp
from jax import lax
from jax.experimental import pallas as pl
from jax.experimental.pallas import tpu as pltpu

F32 = jnp.float32
BF16 = jnp.bfloat16

D_MODEL = 2048
DEPTH = 2
DA_HEADS = 8
DA_HEAD_DIM = 64
DA_WIDTH = DA_HEADS * 2 * DA_HEAD_DIM
SSM_HEADS = 16
SSM_HEAD_DIM = 64
SSM_INNER = SSM_HEADS * SSM_HEAD_DIM
SSM_GROUPS = 2
SSM_STATE = 128
SSM_CONV = 4
SSM_CHUNK = 128
SSM_BC = 2 * SSM_GROUPS * SSM_STATE
GDN_HEADS = 8
GDN_HEAD_DIM = 128
GDN_WIDTH = GDN_HEADS * GDN_HEAD_DIM
GDN_CONV = 4
GDN_CHUNK = 64
D_FF = 5632
FFN_CONV = 3
N_BRANCH = 3
RMS_EPS = 1e-6
L2_EPS = 1e-6

LANES = 128
HALO = 8
VMEM_LIMIT = 56 * 1024 * 1024

P_GDN_QKV = 0
P_SSM_Z = P_GDN_QKV + 3 * GDN_WIDTH
P_GDN_Z = P_SSM_Z + SSM_INNER
P_DA_Q = P_GDN_Z + GDN_WIDTH
P_DA_K = P_DA_Q + DA_WIDTH
P_DA_V = P_DA_K + DA_WIDTH
P_GATES = P_DA_V + DA_WIDTH
P_SSM_X = P_GATES + N_BRANCH * D_MODEL
P_SSM_BC = P_SSM_X + SSM_INNER
P_COLS = P_SSM_BC + SSM_BC
S_DT = 0
S_BETA = SSM_HEADS
S_DECAY = SSM_HEADS + GDN_HEADS


def _silu(x):
    return x / (1.0 + jnp.exp(-x))


def _softplus(x):
    return jnp.maximum(x, 0.0) + jnp.log1p(jnp.exp(-jnp.abs(x)))


def _dot(a, b):
    return jnp.dot(a, b, preferred_element_type=F32)


def _dot_nt(a, b):
    return lax.dot_general(a, b, (((1,), (1,)), ((), ())), preferred_element_type=F32)


def _dot_f32(a, b):
    return jnp.dot(a, b, preferred_element_type=F32, precision=lax.Precision.HIGHEST)


def _rmsnorm_kernel(x_ref, w_ref, o_ref):
    x = x_ref[...]
    ms = jnp.mean(x * x, axis=-1, keepdims=True)
    o_ref[...] = (x * lax.rsqrt(ms + RMS_EPS) * w_ref[...]).astype(o_ref.dtype)


def rmsnorm(x, w, *, tm=512, out_dtype=BF16):
    m, d = x.shape
    return pl.pallas_call(
        _rmsnorm_kernel,
        grid=(m // tm,),
        in_specs=[pl.BlockSpec((tm, d), lambda i: (i, 0)),
                  pl.BlockSpec((1, d), lambda i: (0, 0))],
        out_specs=pl.BlockSpec((tm, d), lambda i: (i, 0)),
        out_shape=jax.ShapeDtypeStruct((m, d), out_dtype),
        name="rmsnorm",
    )(x, w.reshape(1, d))


def _matmul_kernel(a_ref, w_ref, o_ref):
    o_ref[...] = _dot(a_ref[...], w_ref[...]).astype(o_ref.dtype)


def matmul(a, w, *, tm, tn, out_dtype, name):
    m, k = a.shape
    _, n = w.shape
    return pl.pallas_call(
        _matmul_kernel,
        grid=(m // tm, n // tn),
        in_specs=[pl.BlockSpec((tm, k), lambda i, j: (i, 0)),
                  pl.BlockSpec((k, tn), lambda i, j: (0, j))],
        out_specs=pl.BlockSpec((tm, tn), lambda i, j: (i, j)),
        out_shape=jax.ShapeDtypeStruct((m, n), out_dtype),
        compiler_params=pltpu.CompilerParams(
            dimension_semantics=("parallel", "arbitrary"), vmem_limit_bytes=VMEM_LIMIT),
        name=name,
    )(a, w)


def _matmul_res_norm_kernel(a_ref, w_ref, h_ref, nw_ref, *refs, emit_h):
    if emit_h:
        ho_ref, no_ref, acc_ref = refs
    else:
        no_ref, acc_ref = refs
    kk = pl.program_id(1)

    @pl.when(kk == 0)
    def _():
        acc_ref[...] = h_ref[...]

    acc_ref[...] += _dot(a_ref[...], w_ref[...])

    @pl.when(kk == pl.num_programs(1) - 1)
    def _():
        h = acc_ref[...]
        if emit_h:
            ho_ref[...] = h
        ms = jnp.mean(h * h, axis=-1, keepdims=True)
        no_ref[...] = (h * lax.rsqrt(ms + RMS_EPS) * nw_ref[...]).astype(no_ref.dtype)


def matmul_res_norm(a, w, h, nw, *, tm, tk, emit_h, norm_dtype, name):
    m, k = a.shape
    _, n = w.shape
    out_shape = [jax.ShapeDtypeStruct((m, n), norm_dtype)]
    out_specs = [pl.BlockSpec((tm, n), lambda i, j: (i, 0))]
    if emit_h:
        out_shape = [jax.ShapeDtypeStruct((m, n), F32)] + out_shape
        out_specs = [pl.BlockSpec((tm, n), lambda i, j: (i, 0))] + out_specs
    return pl.pallas_call(
        functools.partial(_matmul_res_norm_kernel, emit_h=emit_h),
        grid=(m // tm, k // tk),
        in_specs=[pl.BlockSpec((tm, tk), lambda i, j: (i, j)),
                  pl.BlockSpec((tk, n), lambda i, j: (j, 0)),
                  pl.BlockSpec((tm, n), lambda i, j: (i, 0)),
                  pl.BlockSpec((1, n), lambda i, j: (0, 0))],
        out_specs=out_specs,
        out_shape=out_shape,
        scratch_shapes=[pltpu.VMEM((tm, n), F32)],
        compiler_params=pltpu.CompilerParams(
            dimension_semantics=("parallel", "arbitrary"), vmem_limit_bytes=VMEM_LIMIT),
        name=name,
    )(a, w, h, nw.reshape(1, n))


def _diff_attn_kernel(slopes_ref, lamp_ref, subln_ref, q_ref, k_ref, v_ref, o_ref,
                      s_ref, st_ref, acc_ref, *, tq, lambda_init):
    head = pl.program_id(1)
    qi = pl.program_id(2)
    slope = slopes_ref[head]
    lp = lamp_ref[...]
    lam = (jnp.exp(jnp.sum(lp[0:1] * lp[1:2], axis=-1, keepdims=True))
           - jnp.exp(jnp.sum(lp[2:3] * lp[3:4], axis=-1, keepdims=True)) + lambda_init)

    qs = (q_ref[...].astype(F32) * (DA_HEAD_DIM ** -0.5)).astype(BF16)
    lane = lax.broadcasted_iota(jnp.int32, qs.shape, 1)
    zero = jnp.zeros_like(qs)
    q1 = jnp.where(lane < DA_HEAD_DIM, qs, zero)
    q2 = jnp.where(lane >= DA_HEAD_DIM, qs, zero)
    col = lax.broadcasted_iota(jnp.int32, (1, tq), 1).astype(F32)
    half = tq // 2

    def scores(j):
        kj = k_ref[pl.ds(pl.multiple_of(j * tq, tq), tq), :]
        kb = slope * (col + (j * tq).astype(F32))
        return _dot_nt(q1, kj) + kb, _dot_nt(q2, kj) + kb

    def fold(x):
        return x[:, :half], x[:, half:]

    def put_scores(j, s1, s2):
        s_ref[0, j] = s1
        s_ref[1, j] = s2
        a, b = fold(s1)
        st_ref[0] = jnp.maximum(st_ref[0], jnp.maximum(a, b))
        a, b = fold(s2)
        st_ref[1] = jnp.maximum(st_ref[1], jnp.maximum(a, b))

    st_ref[0] = jnp.full((tq, half), -jnp.inf, F32)
    st_ref[1] = jnp.full((tq, half), -jnp.inf, F32)

    def body_a(j, c):
        s1, s2 = scores(j)
        put_scores(j, s1, s2)
        return c

    lax.fori_loop(0, qi, body_a, 0)
    s1, s2 = scores(qi)
    causal = (lax.broadcasted_iota(jnp.int32, (tq, tq), 0)
              >= lax.broadcasted_iota(jnp.int32, (tq, tq), 1))
    put_scores(qi, jnp.where(causal, s1, -jnp.inf), jnp.where(causal, s2, -jnp.inf))

    for mp in range(2):
        m = jnp.max(st_ref[mp], axis=-1, keepdims=True)
        st_ref[mp] = jnp.broadcast_to(m, (tq, half))
        st_ref[2 + mp] = jnp.zeros((tq, half), F32)

    def body_b(j, c):
        for mp in range(2):
            m = st_ref[mp]
            a, b = fold(s_ref[mp, j])
            pa = jnp.exp(a - m)
            pb = jnp.exp(b - m)
            s_ref[mp, j] = jnp.concatenate([pa, pb], axis=1)
            st_ref[2 + mp] += pa + pb
        return c

    lax.fori_loop(0, qi + 1, body_b, 0)

    l1 = jnp.sum(st_ref[2], axis=-1, keepdims=True)
    l2 = jnp.sum(st_ref[3], axis=-1, keepdims=True)
    st_ref[2] = jnp.broadcast_to(1.0 / l1, (tq, half))
    st_ref[3] = jnp.broadcast_to(lam / l2, (tq, half))
    acc_ref[...] = jnp.zeros_like(acc_ref)

    def body_c(j, c):
        c1 = st_ref[2]
        c2 = st_ref[3]
        a1, b1 = fold(s_ref[0, j])
        a2, b2 = fold(s_ref[1, j])
        w = jnp.concatenate([a1 * c1 - a2 * c2, b1 * c1 - b2 * c2], axis=1).astype(BF16)
        vj = v_ref[pl.ds(pl.multiple_of(j * tq, tq), tq), :]
        acc_ref[...] += _dot(w, vj)
        return c

    lax.fori_loop(0, qi + 1, body_c, 0)

    o = acc_ref[...]
    ms = jnp.mean(o * o, axis=-1, keepdims=True)
    o = o * lax.rsqrt(ms + RMS_EPS) * subln_ref[...] * (1.0 - lambda_init)
    o_ref[...] = o.astype(o_ref.dtype)


def diff_attention(p3, lam_params, subln_w, lambda_init, *, tq=256):
    bsz, seq, _ = p3.shape
    hw = 2 * DA_HEAD_DIM
    nq = seq // tq
    slopes = jnp.asarray([2.0 ** (-8.0 * (h + 1) / DA_HEADS) for h in range(DA_HEADS)], F32)
    qb, kb, vb = P_DA_Q // hw, P_DA_K // hw, P_DA_V // hw
    return pl.pallas_call(
        functools.partial(_diff_attn_kernel, tq=tq, lambda_init=lambda_init),
        grid=(bsz, DA_HEADS, nq),
        in_specs=[pl.BlockSpec(memory_space=pltpu.SMEM),
                  pl.BlockSpec((4, DA_HEAD_DIM), lambda b, h, i: (0, 0)),
                  pl.BlockSpec((1, hw), lambda b, h, i: (0, 0)),
                  pl.BlockSpec((None, tq, hw), lambda b, h, i: (b, i, qb + h)),
                  pl.BlockSpec((None, seq, hw), lambda b, h, i: (b, 0, kb + h)),
                  pl.BlockSpec((None, seq, hw), lambda b, h, i: (b, 0, vb + h))],
        out_specs=pl.BlockSpec((None, tq, hw), lambda b, h, i: (b, i, h)),
        out_shape=jax.ShapeDtypeStruct((bsz, seq, DA_WIDTH), BF16),
        scratch_shapes=[pltpu.VMEM((2, nq, tq, tq), F32),
                        pltpu.VMEM((4, tq, tq // 2), F32),
                        pltpu.VMEM((tq, hw), F32)],
        compiler_params=pltpu.CompilerParams(
            dimension_semantics=("parallel", "parallel", "arbitrary"), vmem_limit_bytes=VMEM_LIMIT),
        name="diff_attention",
    )(slopes, lam_params, subln_w.reshape(1, hw), p3, p3, p3)


def _ssd_kernel(x_ref, bc_ref, z_ref, sm_ref, cw_ref, cb_ref, dtb_ref, alog_ref, dskip_ref, nw_ref,
                o_ref, tail_ref, state_ref, xe_ref, y_ref):
    q = SSM_CHUNK
    c = pl.program_id(1)

    @pl.when(c == 0)
    def _():
        tail_ref[...] = jnp.zeros_like(tail_ref)
        state_ref[...] = jnp.zeros_like(state_ref)

    xe_ref[0:HALO, :] = tail_ref[...]
    xe_ref[HALO:HALO + q, 0:SSM_INNER] = x_ref[...].astype(F32)
    xe_ref[HALO:HALO + q, SSM_INNER:] = bc_ref[...].astype(F32)
    tail_ref[...] = xe_ref[q:q + HALO, :]
    conv = cb_ref[...] + cw_ref[0:1, :] * xe_ref[HALO - 3:HALO - 3 + q, :]
    for t in range(1, SSM_CONV):
        conv = conv + cw_ref[t:t + 1, :] * xe_ref[HALO - 3 + t:HALO - 3 + t + q, :]
    xbc = _silu(conv)
    xs = xbc[:, :SSM_INNER]
    xs_b = xs.astype(BF16)

    dt = _softplus(sm_ref[...] + dtb_ref[...])
    da = dt * (-jnp.exp(alog_ref[...]))
    row = lax.broadcasted_iota(jnp.int32, (q, q), 0)
    colm = lax.broadcasted_iota(jnp.int32, (q, q), 1)
    causal = row >= colm
    tri = causal.astype(F32)
    a_cs = _dot_f32(tri, da)
    a_cs_t = a_cs.T
    dt_t = dt.T
    lane = lax.broadcasted_iota(jnp.int32, (1, LANES), 1)

    for g in range(SSM_GROUPS):
        bm = xbc[:, SSM_INNER + g * SSM_STATE:SSM_INNER + (g + 1) * SSM_STATE]
        cm = xbc[:, SSM_INNER + (SSM_GROUPS + g) * SSM_STATE:
                 SSM_INNER + (SSM_GROUPS + g + 1) * SSM_STATE]
        cb = _dot_nt(cm.astype(BF16), bm.astype(BF16))
        bm_t = bm.T
        hpg = SSM_HEADS // SSM_GROUPS
        for pr in range(hpg // 2):
            pair = g * (hpg // 2) + pr
            x_pair = xs_b[:, pair * LANES:(pair + 1) * LANES]
            st = state_ref[pair]
            rhs = jnp.concatenate([x_pair, st.astype(BF16)], axis=0)
            ys, sts, cds = [], [], []
            for sub in range(2):
                h = 2 * pair + sub
                acol = a_cs[:, h:h + 1]
                arow = a_cs_t[h:h + 1, :]
                dtrow = dt_t[h:h + 1, :]
                decay = jnp.exp(jnp.where(causal, acol - arow, -jnp.inf))
                sc = (cb * decay * dtrow).astype(BF16)
                c_in = (cm * jnp.exp(acol)).astype(BF16)
                ys.append(_dot(jnp.concatenate([sc, c_in], axis=1), rhs))
                a_last = arow[:, q - 1:q]
                wrow = jnp.exp(a_last - arow) * dtrow
                sts.append(_dot((bm_t * wrow).astype(BF16), x_pair))
                cds.append(jnp.exp(a_last))
            first = lane < SSM_HEAD_DIM
            y_ref[:, pair * LANES:(pair + 1) * LANES] = jnp.where(first, ys[0], ys[1])
            state_ref[pair] = (st * jnp.where(first, cds[0], cds[1])
                               + jnp.where(first, sts[0], sts[1]))

    y = y_ref[...] + xs * dskip_ref[...]
    y = y * _silu(z_ref[...].astype(F32))
    gw = SSM_INNER // SSM_GROUPS
    for g in range(SSM_GROUPS):
        yg = y[:, g * gw:(g + 1) * gw]
        ms = jnp.mean(yg * yg, axis=-1, keepdims=True)
        o_ref[:, g * gw:(g + 1) * gw] = (
            yg * lax.rsqrt(ms + RMS_EPS) * nw_ref[:, g * gw:(g + 1) * gw]).astype(o_ref.dtype)


def _pad_row(v, offset):
    return jnp.zeros((1, LANES), F32).at[0, offset:offset + v.shape[0]].set(v.astype(F32))


def mamba2_ssd(p3, small3, conv_w, conv_b, dt_bias, a_log, d_skip, norm_w):
    bsz, seq, _ = p3.shape
    q = SSM_CHUNK
    nc = seq // q
    cch = SSM_INNER + SSM_BC
    const = lambda b, c: (0, 0)
    return pl.pallas_call(
        _ssd_kernel,
        grid=(bsz, nc),
        in_specs=[pl.BlockSpec((None, q, SSM_INNER), lambda b, c: (b, c, P_SSM_X // SSM_INNER)),
                  pl.BlockSpec((None, q, SSM_BC), lambda b, c: (b, c, P_SSM_BC // SSM_BC)),
                  pl.BlockSpec((None, q, SSM_INNER), lambda b, c: (b, c, P_SSM_Z // SSM_INNER)),
                  pl.BlockSpec((None, q, LANES), lambda b, c: (b, c, 0)),
                  pl.BlockSpec((SSM_CONV, cch), const),
                  pl.BlockSpec((1, cch), const),
                  pl.BlockSpec((1, LANES), const),
                  pl.BlockSpec((1, LANES), const),
                  pl.BlockSpec((1, SSM_INNER), const),
                  pl.BlockSpec((1, SSM_INNER), const)],
        out_specs=pl.BlockSpec((None, q, SSM_INNER), lambda b, c: (b, c, 0)),
        out_shape=jax.ShapeDtypeStruct((bsz, seq, SSM_INNER), BF16),
        scratch_shapes=[pltpu.VMEM((HALO, cch), F32),
                        pltpu.VMEM((SSM_HEADS // 2, SSM_STATE, LANES), F32),
                        pltpu.VMEM((HALO + q, cch), F32),
                        pltpu.VMEM((q, SSM_INNER), F32)],
        compiler_params=pltpu.CompilerParams(
            dimension_semantics=("parallel", "arbitrary"), vmem_limit_bytes=VMEM_LIMIT),
        name="mamba2_ssd",
    )(p3, p3, p3, small3, conv_w, conv_b.reshape(1, cch), _pad_row(dt_bias, S_DT),
      _pad_row(a_log, S_DT), jnp.repeat(d_skip.astype(F32), SSM_HEAD_DIM).reshape(1, SSM_INNER),
      norm_w.reshape(1, SSM_INNER))


def _gdn_kernel(qkv_ref, z_ref, sm_ref, cw_ref, dtb_ref, alog_ref, nw_ref,
                o_ref, tail_ref, state_ref, xe_ref):
    cs = GDN_CHUNK
    d = GDN_HEAD_DIM
    c = pl.program_id(1)

    @pl.when(c == 0)
    def _():
        tail_ref[...] = jnp.zeros_like(tail_ref)
        state_ref[...] = jnp.zeros_like(state_ref)

    xe_ref[0:HALO, :] = tail_ref[...]
    xe_ref[HALO:HALO + cs, :] = qkv_ref[...].astype(F32)
    tail_ref[...] = xe_ref[cs:cs + HALO, :]
    conv = cw_ref[0:1, :] * xe_ref[HALO - 3:HALO - 3 + cs, :]
    for t in range(1, GDN_CONV):
        conv = conv + cw_ref[t:t + 1, :] * xe_ref[HALO - 3 + t:HALO - 3 + t + cs, :]
    qkv = _silu(conv)

    sm = sm_ref[...]
    beta = 1.0 / (1.0 + jnp.exp(-sm))
    gl = -jnp.exp(alog_ref[...]) * _softplus(sm + dtb_ref[...])
    row = lax.broadcasted_iota(jnp.int32, (cs, cs), 0)
    colm = lax.broadcasted_iota(jnp.int32, (cs, cs), 1)
    incl = row >= colm
    strict = row > colm
    eye = (row == colm).astype(F32)
    g_cs = _dot_f32(incl.astype(F32), gl)
    g_cs_t = g_cs.T

    for h in range(GDN_HEADS):
        qh = qkv[:, h * d:(h + 1) * d]
        kh = qkv[:, GDN_WIDTH + h * d:GDN_WIDTH + (h + 1) * d]
        vh = qkv[:, 2 * GDN_WIDTH + h * d:2 * GDN_WIDTH + (h + 1) * d]
        qn = qh * lax.rsqrt(jnp.sum(qh * qh, axis=-1, keepdims=True) + L2_EPS) * (d ** -0.5)
        kn = kh * lax.rsqrt(jnp.sum(kh * kh, axis=-1, keepdims=True) + L2_EPS)
        bcol = beta[:, S_BETA + h:S_BETA + h + 1]
        gcol = g_cs[:, S_DECAY + h:S_DECAY + h + 1]
        grow = g_cs_t[S_DECAY + h:S_DECAY + h + 1, :]
        eg = jnp.exp(gcol)
        decay = jnp.where(incl, jnp.exp(jnp.where(incl, gcol - grow, 0.0)), 0.0)
        kb = kn * bcol
        vb = vh * bcol
        kn_b = kn.astype(BF16)
        a_mat = jnp.where(strict, _dot_nt(kb.astype(BF16), kn_b) * decay, 0.0)
        pw = -a_mat
        tinv = eye + pw
        for _ in range(int(math.log2(cs)) - 1):
            pw = _dot_f32(pw, pw)
            tinv = tinv + _dot_f32(tinv, pw)
        sol = _dot_f32(tinv, jnp.concatenate([vb, kb * eg], axis=1))
        u = sol[:, :d]
        w = sol[:, d:]
        st = state_ref[h]
        st_b = st.astype(BF16)
        attn = _dot_nt(qn.astype(BF16), kn_b) * decay
        v_new = u - _dot(w.astype(BF16), st_b)
        o = _dot((qn * eg).astype(BF16), st_b) + _dot(attn.astype(BF16), v_new.astype(BF16))
        g_last = grow[:, cs - 1:cs]
        k_dec = (kn * jnp.exp(g_last - gcol)).T.astype(BF16)
        state_ref[h] = st * jnp.exp(g_last) + _dot(k_dec, v_new.astype(BF16))
        ms = jnp.mean(o * o, axis=-1, keepdims=True)
        zh = z_ref[:, h * d:(h + 1) * d].astype(F32)
        o_ref[:, h * d:(h + 1) * d] = (
            o * lax.rsqrt(ms + RMS_EPS) * nw_ref[...] * _silu(zh)).astype(o_ref.dtype)


def gated_deltanet(p3, small3, conv_w, dt_bias, a_log, norm_w):
    bsz, seq, _ = p3.shape
    cs = GDN_CHUNK
    nc = seq // cs
    w3 = 3 * GDN_WIDTH
    const = lambda b, c: (0, 0)
    return pl.pallas_call(
        _gdn_kernel,
        grid=(bsz, nc),
        in_specs=[pl.BlockSpec((None, cs, w3), lambda b, c: (b, c, P_GDN_QKV // w3)),
                  pl.BlockSpec((None, cs, GDN_WIDTH), lambda b, c: (b, c, P_GDN_Z // GDN_WIDTH)),
                  pl.BlockSpec((None, cs, LANES), lambda b, c: (b, c, 0)),
                  pl.BlockSpec((GDN_CONV, w3), const),
                  pl.BlockSpec((1, LANES), const),
                  pl.BlockSpec((1, LANES), const),
                  pl.BlockSpec((1, GDN_HEAD_DIM), const)],
        out_specs=pl.BlockSpec((None, cs, GDN_WIDTH), lambda b, c: (b, c, 0)),
        out_shape=jax.ShapeDtypeStruct((bsz, seq, GDN_WIDTH), BF16),
        scratch_shapes=[pltpu.VMEM((HALO, w3), F32),
                        pltpu.VMEM((GDN_HEADS, GDN_HEAD_DIM, GDN_HEAD_DIM), F32),
                        pltpu.VMEM((HALO + cs, w3), F32)],
        compiler_params=pltpu.CompilerParams(
            dimension_semantics=("parallel", "arbitrary"), vmem_limit_bytes=VMEM_LIMIT),
        name="gated_deltanet",
    )(p3, p3, small3, conv_w, _pad_row(dt_bias, S_DECAY), _pad_row(a_log, S_DECAY),
      norm_w.reshape(1, GDN_HEAD_DIM))


def _merge_kernel(oa_ref, os_ref, og_ref, wb_ref, g0_ref, g1_ref, g2_ref, o_ref):
    acc = None
    for br, (x_ref, g_ref) in enumerate(((oa_ref, g0_ref), (os_ref, g1_ref), (og_ref, g2_ref))):
        gate = 1.0 / (1.0 + jnp.exp(-g_ref[...].astype(F32)))
        term = gate * _dot(x_ref[...], wb_ref[br])
        acc = term if acc is None else acc + term
    o_ref[...] = acc.astype(o_ref.dtype)


def merge_branches(o_da, o_ssm, o_gdn, wb3, p, *, tm=1024, tn=512):
    m = o_da.shape[0]
    gb = P_GATES // tn
    nb = D_MODEL // tn
    act = pl.BlockSpec((tm, DA_WIDTH), lambda i, j: (i, 0))
    return pl.pallas_call(
        _merge_kernel,
        grid=(m // tm, nb),
        in_specs=[act, act, act,
                  pl.BlockSpec((N_BRANCH, DA_WIDTH, tn), lambda i, j: (0, 0, j)),
                  pl.BlockSpec((tm, tn), lambda i, j: (i, gb + j)),
                  pl.BlockSpec((tm, tn), lambda i, j: (i, gb + nb + j)),
                  pl.BlockSpec((tm, tn), lambda i, j: (i, gb + 2 * nb + j))],
        out_specs=pl.BlockSpec((tm, tn), lambda i, j: (i, j)),
        out_shape=jax.ShapeDtypeStruct((m, D_MODEL), BF16),
        compiler_params=pltpu.CompilerParams(
            dimension_semantics=("parallel", "arbitrary"), vmem_limit_bytes=VMEM_LIMIT),
        name="merge_branches",
    )(o_da, o_ssm, o_gdn, wb3, p, p, p)


def _ffn_up_kernel(x_ref, wg_ref, wv_ref, cwg_ref, cwv_ref, cbg_ref, cbv_ref, o_ref, ug_ref, uv_ref,
                   *, rows):
    seq = x_ref.shape[0]
    pad = HALO
    ug_ref[0:pad, :] = jnp.zeros((pad, ug_ref.shape[1]), F32)
    uv_ref[0:pad, :] = jnp.zeros((pad, uv_ref.shape[1]), F32)
    for r in range(0, seq, rows):
        xr = x_ref[r:r + rows, :]
        ug_ref[pad + r:pad + r + rows, :] = _dot(xr, wg_ref[...])
        uv_ref[pad + r:pad + r + rows, :] = _dot(xr, wv_ref[...])
    for r in range(0, seq, rows):
        def conv(u_ref, cw_ref, cb_ref):
            acc = cb_ref[...] + cw_ref[0:1, :] * u_ref[pad + r - 2:pad + r - 2 + rows, :]
            for t in range(1, FFN_CONV):
                acc = acc + cw_ref[t:t + 1, :] * u_ref[pad + r - 2 + t:pad + r - 2 + t + rows, :]
            return acc
        gate = conv(ug_ref, cwg_ref, cbg_ref)
        val = conv(uv_ref, cwv_ref, cbv_ref)
        o_ref[r:r + rows, :] = (_silu(gate) * val).astype(o_ref.dtype)


def ffn_up_proj(hn3, w_up, conv_w, conv_b, *, tf=512, rows=256):
    bsz, seq, d = hn3.shape
    nf = D_FF // tf
    cb = conv_b.reshape(1, 2 * D_FF)
    return pl.pallas_call(
        functools.partial(_ffn_up_kernel, rows=rows),
        grid=(bsz, nf),
        in_specs=[pl.BlockSpec((None, seq, d), lambda b, f: (b, 0, 0)),
                  pl.BlockSpec((d, tf), lambda b, f: (0, f)),
                  pl.BlockSpec((d, tf), lambda b, f: (0, nf + f)),
                  pl.BlockSpec((FFN_CONV, tf), lambda b, f: (0, f)),
                  pl.BlockSpec((FFN_CONV, tf), lambda b, f: (0, nf + f)),
                  pl.BlockSpec((1, tf), lambda b, f: (0, f)),
                  pl.BlockSpec((1, tf), lambda b, f: (0, nf + f))],
        out_specs=pl.BlockSpec((None, seq, tf), lambda b, f: (b, 0, f)),
        out_shape=jax.ShapeDtypeStruct((bsz, seq, D_FF), BF16),
        scratch_shapes=[pltpu.VMEM((HALO + seq, tf), F32), pltpu.VMEM((HALO + seq, tf), F32)],
        compiler_params=pltpu.CompilerParams(
            dimension_semantics=("parallel", "arbitrary"), vmem_limit_bytes=VMEM_LIMIT),
        name="ffn_up",
    )(hn3, w_up, w_up, conv_w, conv_w, cb, cb)


def _reorder_w_in(w):
    sizes = (DA_WIDTH, DA_WIDTH, DA_WIDTH, SSM_INNER, SSM_INNER + SSM_BC, SSM_HEADS,
             3 * GDN_WIDTH, GDN_WIDTH, GDN_HEADS, GDN_HEADS, N_BRANCH * D_MODEL)
    offs = [0]
    for s in sizes:
        offs.append(offs[-1] + s)
    seg = lambda i: w[:, offs[i]:offs[i + 1]]
    da_q, da_k, da_v, ssm_z, ssm_xbc, ssm_dt, gdn_qkv, gdn_z, gdn_b, gdn_a, gates = (
        seg(i) for i in range(len(sizes)))
    big = jnp.concatenate([gdn_qkv, ssm_z, gdn_z, da_q, da_k, da_v, gates, ssm_xbc], axis=1)
    small = jnp.concatenate(
        [ssm_dt, gdn_b, gdn_a,
         jnp.zeros((w.shape[0], LANES - SSM_HEADS - 2 * GDN_HEADS), w.dtype)], axis=1)
    return big.astype(BF16), small.astype(BF16)


def kernel(x, norm_mix, w_in, da_lambda, da_subln, ssm_conv_w, ssm_conv_b, ssm_dt_bias, ssm_a_log,
           ssm_d, ssm_norm, gdn_conv_w, gdn_dt_bias, gdn_a_log, gdn_norm, w_branch, w_out, norm_ffn,
           ffn_up, ffn_conv_w, ffn_conv_b, ffn_down, norm_final):
    bsz, seq, d = x.shape
    m = bsz * seq
    h = x.reshape(m, d)
    hn = rmsnorm(h, norm_mix[0])
    out = None
    for l in range(DEPTH):
        w_big, w_small = _reorder_w_in(w_in[l])
        p = matmul(hn, w_big, tm=1024, tn=512, out_dtype=BF16, name="in_proj")
        small = matmul(hn, w_small, tm=1024, tn=LANES, out_dtype=F32, name="in_proj_small")
        p3 = p.reshape(bsz, seq, P_COLS)
        small3 = small.reshape(bsz, seq, LANES)
        lambda_init = 0.8 - 0.6 * math.exp(-0.3 * l)
        o_da = diff_attention(p3, da_lambda[l], da_subln[l], lambda_init)
        o_ssm = mamba2_ssd(p3, small3, ssm_conv_w[l], ssm_conv_b[l], ssm_dt_bias[l], ssm_a_log[l],
                           ssm_d[l], ssm_norm[l])
        o_gdn = gated_deltanet(p3, small3, gdn_conv_w[l], gdn_dt_bias[l], gdn_a_log[l], gdn_norm[l])
        wb3 = w_branch[l].astype(BF16).reshape(N_BRANCH, DA_WIDTH, D_MODEL)
        merged = merge_branches(o_da.reshape(m, DA_WIDTH), o_ssm.reshape(m, SSM_INNER),
                                o_gdn.reshape(m, GDN_WIDTH), wb3, p)
        h, hn = matmul_res_norm(merged, w_out[l].astype(BF16), h, norm_ffn[l], tm=512, tk=512,
                                emit_h=True, norm_dtype=BF16, name="out_proj")
        act = ffn_up_proj(hn.reshape(bsz, seq, d), ffn_up[l].astype(BF16), ffn_conv_w[l], ffn_conv_b[l])
        act = act.reshape(m, D_FF)
        if l + 1 < DEPTH:
            h, hn = matmul_res_norm(act, ffn_down[l].astype(BF16), h, norm_mix[l + 1], tm=512, tk=512,
                                    emit_h=True, norm_dtype=BF16, name="ffn_down")
        else:
            (out,) = matmul_res_norm(act, ffn_down[l].astype(BF16), h, norm_final, tm=512, tk=512,
                                     emit_h=False, norm_dtype=F32, name="ffn_down_final")
    return out.reshape(bsz, seq, d)
```

```python
import functools
import math

import jax
import jax.numpy as jnp
from jax import lax
from jax.experimental import pallas as pl
from jax.experimental.pallas import tpu as pltpu

F32 = jnp.float32
BF16 = jnp.bfloat16

D_MODEL = 2048
DEPTH = 2
DA_HEADS = 8
DA_HEAD_DIM = 64
DA_WIDTH = DA_HEADS * 2 * DA_HEAD_DIM
SSM_HEADS = 16
SSM_HEAD_DIM = 64
SSM_INNER = SSM_HEADS * SSM_HEAD_DIM
SSM_GROUPS = 2
SSM_STATE = 128
SSM_CONV = 4
SSM_CHUNK = 128
SSM_BC = 2 * SSM_GROUPS * SSM_STATE
GDN_HEADS = 8
GDN_HEAD_DIM = 128
GDN_WIDTH = GDN_HEADS * GDN_HEAD_DIM
GDN_CONV = 4
GDN_CHUNK = 64
D_FF = 5632
FFN_CONV = 3
N_BRANCH = 3
RMS_EPS = 1e-6
L2_EPS = 1e-6

LANES = 128
HALO = 8
VMEM_LIMIT = 56 * 1024 * 1024

P_GDN_QKV = 0
P_SSM_Z = P_GDN_QKV + 3 * GDN_WIDTH
P_GDN_Z = P_SSM_Z + SSM_INNER
P_DA_Q = P_GDN_Z + GDN_WIDTH
P_DA_K = P_DA_Q + DA_WIDTH
P_DA_V = P_DA_K + DA_WIDTH
P_GATES = P_DA_V + DA_WIDTH
P_SSM_X = P_GATES + N_BRANCH * D_MODEL
P_SSM_BC = P_SSM_X + SSM_INNER
P_COLS = P_SSM_BC + SSM_BC
S_DT = 0
S_BETA = SSM_HEADS
S_DECAY = SSM_HEADS + GDN_HEADS


def _silu(x):
    return x / (1.0 + jnp.exp(-x))


def _softplus(x):
    return jnp.maximum(x, 0.0) + jnp.log1p(jnp.exp(-jnp.abs(x)))


def _dot(a, b):
    return jnp.dot(a, b, preferred_element_type=F32)


def _dot_nt(a, b):
    return lax.dot_general(a, b, (((1,), (1,)), ((), ())), preferred_element_type=F32)


def _dot_f32(a, b):
    return jnp.dot(a, b, preferred_element_type=F32, precision=lax.Precision.HIGHEST)


def _rmsnorm_kernel(x_ref, w_ref, o_ref):
    x = x_ref[...]
    ms = jnp.mean(x * x, axis=-1, keepdims=True)
    o_ref[...] = (x * lax.rsqrt(ms + RMS_EPS) * w_ref[...]).astype(o_ref.dtype)


def rmsnorm(x, w, *, tm=512, out_dtype=BF16):
    m, d = x.shape
    return pl.pallas_call(
        _rmsnorm_kernel,
        grid=(m // tm,),
        in_specs=[pl.BlockSpec((tm, d), lambda i: (i, 0)),
                  pl.BlockSpec((1, d), lambda i: (0, 0))],
        out_specs=pl.BlockSpec((tm, d), lambda i: (i, 0)),
        out_shape=jax.ShapeDtypeStruct((m, d), out_dtype),
        name="rmsnorm",
    )(x, w.reshape(1, d))


def _matmul_kernel(a_ref, w_ref, o_ref):
    o_ref[...] = _dot(a_ref[...], w_ref[...]).astype(o_ref.dtype)


def matmul(a, w, *, tm, tn, out_dtype, name):
    m, k = a.shape
    _, n = w.shape
    return pl.pallas_call(
        _matmul_kernel,
        grid=(m // tm, n // tn),
        in_specs=[pl.BlockSpec((tm, k), lambda i, j: (i, 0)),
                  pl.BlockSpec((k, tn), lambda i, j: (0, j))],
        out_specs=pl.BlockSpec((tm, tn), lambda i, j: (i, j)),
        out_shape=jax.ShapeDtypeStruct((m, n), out_dtype),
        compiler_params=pltpu.CompilerParams(
            dimension_semantics=("parallel", "arbitrary"), vmem_limit_bytes=VMEM_LIMIT),
        name=name,
    )(a, w)


def _matmul_res_norm_kernel(a_ref, w_ref, h_ref, nw_ref, *refs, emit_h):
    if emit_h:
        ho_ref, no_ref, acc_ref = refs
    else:
        no_ref, acc_ref = refs
    kk = pl.program_id(1)

    @pl.when(kk == 0)
    def _():
        acc_ref[...] = h_ref[...]

    acc_ref[...] += _dot(a_ref[...], w_ref[...])

    @pl.when(kk == pl.num_programs(1) - 1)
    def _():
        h = acc_ref[...]
        if emit_h:
            ho_ref[...] = h
        ms = jnp.mean(h * h, axis=-1, keepdims=True)
        no_ref[...] = (h * lax.rsqrt(ms + RMS_EPS) * nw_ref[...]).astype(no_ref.dtype)


def matmul_res_norm(a, w, h, nw, *, tm, tk, emit_h, norm_dtype, name):
    m, k = a.shape
    _, n = w.shape
    out_shape = [jax.ShapeDtypeStruct((m, n), norm_dtype)]
    out_specs = [pl.BlockSpec((tm, n), lambda i, j: (i, 0))]
    if emit_h:
        out_shape = [jax.ShapeDtypeStruct((m, n), F32)] + out_shape
        out_specs = [pl.BlockSpec((tm, n), lambda i, j: (i, 0))] + out_specs
    return pl.pallas_call(
        functools.partial(_matmul_res_norm_kernel, emit_h=emit_h),
        grid=(m // tm, k // tk),
        in_specs=[pl.BlockSpec((tm, tk), lambda i, j: (i, j)),
                  pl.BlockSpec((tk, n), lambda i, j: (j, 0)),
                  pl.BlockSpec((tm, n), lambda i, j: (i, 0)),
                  pl.BlockSpec((1, n), lambda i, j: (0, 0))],
        out_specs=out_specs,
        out_shape=out_shape,
        scratch_shapes=[pltpu.VMEM((tm, n), F32)],
        compiler_params=pltpu.CompilerParams(
            dimension_semantics=("parallel", "arbitrary"), vmem_limit_bytes=VMEM_LIMIT),
        name=name,
    )(a, w, h, nw.reshape(1, n))


def _diff_attn_kernel(slopes_ref, lamp_ref, subln_ref, q_ref, k_ref, v_ref, o_ref,
                      s_ref, st_ref, acc_ref, *, tq, lambda_init):
    head = pl.program_id(1)
    qi = pl.program_id(2)
    slope = slopes_ref[head]
    lp = lamp_ref[...]
    lam = (jnp.exp(jnp.sum(lp[0:1] * lp[1:2], axis=-1, keepdims=True))
           - jnp.exp(jnp.sum(lp[2:3] * lp[3:4], axis=-1, keepdims=True)) + lambda_init)

    qs = (q_ref[...].astype(F32) * (DA_HEAD_DIM ** -0.5)).astype(BF16)
    lane = lax.broadcasted_iota(jnp.int32, qs.shape, 1)
    zero = jnp.zeros_like(qs)
    q1 = jnp.where(lane < DA_HEAD_DIM, qs, zero)
    q2 = jnp.where(lane >= DA_HEAD_DIM, qs, zero)
    col = lax.broadcasted_iota(jnp.int32, (1, tq), 1).astype(F32)
    half = tq // 2

    def scores(j):
        kj = k_ref[pl.ds(pl.multiple_of(j * tq, tq), tq), :]
        kb = slope * (col + (j * tq).astype(F32))
        return _dot_nt(q1, kj) + kb, _dot_nt(q2, kj) + kb

    def fold(x):
        return x[:, :half], x[:, half:]

    def put_scores(j, s1, s2):
        s_ref[0, j] = s1
        s_ref[1, j] = s2
        a, b = fold(s1)
        st_ref[0] = jnp.maximum(st_ref[0], jnp.maximum(a, b))
        a, b = fold(s2)
        st_ref[1] = jnp.maximum(st_ref[1], jnp.maximum(a, b))

    st_ref[0] = jnp.full((tq, half), -jnp.inf, F32)
    st_ref[1] = jnp.full((tq, half), -jnp.inf, F32)

    def body_a(j, c):
        s1, s2 = scores(j)
        put_scores(j, s1, s2)
        return c

    lax.fori_loop(0, qi, body_a, 0)
    s1, s2 = scores(qi)
    causal = (lax.broadcasted_iota(jnp.int32, (tq, tq), 0)
              >= lax.broadcasted_iota(jnp.int32, (tq, tq), 1))
    put_scores(qi, jnp.where(causal, s1, -jnp.inf), jnp.where(causal, s2, -jnp.inf))

    for mp in range(2):
        m = jnp.max(st_ref[mp], axis=-1, keepdims=True)
        st_ref[mp] = jnp.broadcast_to(m, (tq, half))
        st_ref[2 + mp] = jnp.zeros((tq, half), F32)

    def body_b(j, c):
        for mp in range(2):
            m = st_ref[mp]
            a, b = fold(s_ref[mp, j])
            pa = jnp.exp(a - m)
            pb = jnp.exp(b - m)
            s_ref[mp, j] = jnp.concatenate([pa, pb], axis=1)
            st_ref[2 + mp] += pa + pb
        return c

    lax.fori_loop(0, qi + 1, body_b, 0)

    l1 = jnp.sum(st_ref[2], axis=-1, keepdims=True)
    l2 = jnp.sum(st_ref[3], axis=-1, keepdims=True)
    st_ref[2] = jnp.broadcast_to(1.0 / l1, (tq, half))
    st_ref[3] = jnp.broadcast_to(lam / l2, (tq, half))
    acc_ref[...] = jnp.zeros_like(acc_ref)

    def body_c(j, c):
        c1 = st_ref[2]
        c2 = st_ref[3]
        a1, b1 = fold(s_ref[0, j])
        a2, b2 = fold(s_ref[1, j])
        w = jnp.concatenate([a1 * c1 - a2 * c2, b1 * c1 - b2 * c2], axis=1).astype(BF16)
        vj = v_ref[pl.ds(pl.multiple_of(j * tq, tq), tq), :]
        acc_ref[...] += _dot(w, vj)
        return c

    lax.fori_loop(0, qi + 1, body_c, 0)

    o = acc_ref[...]
    ms = jnp.mean(o * o, axis=-1, keepdims=True)
    o = o * lax.rsqrt(ms + RMS_EPS) * subln_ref[...] * (1.0 - lambda_init)
    o_ref[...] = o.astype(o_ref.dtype)


def diff_attention(p3, lam_params, subln_w, lambda_init, *, tq=256):
    bsz, seq, _ = p3.shape
    hw = 2 * DA_HEAD_DIM
    nq = seq // tq
    slopes = jnp.asarray([2.0 ** (-8.0 * (h + 1) / DA_HEADS) for h in range(DA_HEADS)], F32)
    qb, kb, vb = P_DA_Q // hw, P_DA_K // hw, P_DA_V // hw
    return pl.pallas_call(
        functools.partial(_diff_attn_kernel, tq=tq, lambda_init=lambda_init),
        grid=(bsz, DA_HEADS, nq),
        in_specs=[pl.BlockSpec(memory_space=pltpu.SMEM),
                  pl.BlockSpec((4, DA_HEAD_DIM), lambda b, h, i: (0, 0)),
                  pl.BlockSpec((1, hw), lambda b, h, i: (0, 0)),
                  pl.BlockSpec((None, tq, hw), lambda b, h, i: (b, i, qb + h)),
                  pl.BlockSpec((None, seq, hw), lambda b, h, i: (b, 0, kb + h)),
                  pl.BlockSpec((None, seq, hw), lambda b, h, i: (b, 0, vb + h))],
        out_specs=pl.BlockSpec((None, tq, hw), lambda b, h, i: (b, i, h)),
        out_shape=jax.ShapeDtypeStruct((bsz, seq, DA_WIDTH), BF16),
        scratch_shapes=[pltpu.VMEM((2, nq, tq, tq), F32),
                        pltpu.VMEM((4, tq, tq // 2), F32),
                        pltpu.VMEM((tq, hw), F32)],
        compiler_params=pltpu.CompilerParams(
            dimension_semantics=("parallel", "parallel", "arbitrary"), vmem_limit_bytes=VMEM_LIMIT),
        name="diff_attention",
    )(slopes, lam_params, subln_w.reshape(1, hw), p3, p3, p3)


def _ssd_kernel(x_ref, bc_ref, z_ref, sm_ref, cw_ref, cb_ref, dtb_ref, alog_ref, dskip_ref, nw_ref,
                o_ref, tail_ref, state_ref, xe_ref, y_ref):
    q = SSM_CHUNK
    c = pl.program_id(1)

    @pl.when(c == 0)
    def _():
        tail_ref[...] = jnp.zeros_like(tail_ref)
        state_ref[...] = jnp.zeros_like(state_ref)

    xe_ref[0:HALO, :] = tail_ref[...]
    xe_ref[HALO:HALO + q, 0:SSM_INNER] = x_ref[...].astype(F32)
    xe_ref[HALO:HALO + q, SSM_INNER:] = bc_ref[...].astype(F32)
    tail_ref[...] = xe_ref[q:q + HALO, :]
    conv = cb_ref[...] + cw_ref[0:1, :] * xe_ref[HALO - 3:HALO - 3 + q, :]
    for t in range(1, SSM_CONV):
        conv = conv + cw_ref[t:t + 1, :] * xe_ref[HALO - 3 + t:HALO - 3 + t + q, :]
    xbc = _silu(conv)
    xs = xbc[:, :SSM_INNER]
    xs_b = xs.astype(BF16)

    dt = _softplus(sm_ref[...] + dtb_ref[...])
    da = dt * (-jnp.exp(alog_ref[...]))
    row = lax.broadcasted_iota(jnp.int32, (q, q), 0)
    colm = lax.broadcasted_iota(jnp.int32, (q, q), 1)
    causal = row >= colm
    tri = causal.astype(F32)
    a_cs = _dot_f32(tri, da)
    a_cs_t = a_cs.T
    dt_t = dt.T
    lane = lax.broadcasted_iota(jnp.int32, (1, LANES), 1)

    for g in range(SSM_GROUPS):
        bm = xbc[:, SSM_INNER + g * SSM_STATE:SSM_INNER + (g + 1) * SSM_STATE]
        cm = xbc[:, SSM_INNER + (SSM_GROUPS + g) * SSM_STATE:
                 SSM_INNER + (SSM_GROUPS + g + 1) * SSM_STATE]
        cb = _dot_nt(cm.astype(BF16), bm.astype(BF16))
        bm_t = bm.T
        hpg = SSM_HEADS // SSM_GROUPS
        for pr in range(hpg // 2):
            pair = g * (hpg // 2) + pr
            x_pair = xs_b[:, pair * LANES:(pair + 1) * LANES]
            st = state_ref[pair]
            rhs = jnp.concatenate([x_pair, st.astype(BF16)], axis=0)
            ys, sts, cds = [], [], []
            for sub in range(2):
                h = 2 * pair + sub
                acol = a_cs[:, h:h + 1]
                arow = a_cs_t[h:h + 1, :]
                dtrow = dt_t[h:h + 1, :]
                decay = jnp.exp(jnp.where(causal, acol - arow, -jnp.inf))
                sc = (cb * decay * dtrow).astype(BF16)
                c_in = (cm * jnp.exp(acol)).astype(BF16)
                ys.append(_dot(jnp.concatenate([sc, c_in], axis=1), rhs))
                a_last = arow[:, q - 1:q]
                wrow = jnp.exp(a_last - arow) * dtrow
                sts.append(_dot((bm_t * wrow).astype(BF16), x_pair))
                cds.append(jnp.exp(a_last))
            first = lane < SSM_HEAD_DIM
            y_ref[:, pair * LANES:(pair + 1) * LANES] = jnp.where(first, ys[0], ys[1])
            state_ref[pair] = (st * jnp.where(first, cds[0], cds[1])
                               + jnp.where(first, sts[0], sts[1]))

    y = y_ref[...] + xs * dskip_ref[...]
    y = y * _silu(z_ref[...].astype(F32))
    gw = SSM_INNER // SSM_GROUPS
    for g in range(SSM_GROUPS):
        yg = y[:, g * gw:(g + 1) * gw]
        ms = jnp.mean(yg * yg, axis=-1, keepdims=True)
        o_ref[:, g * gw:(g + 1) * gw] = (
            yg * lax.rsqrt(ms + RMS_EPS) * nw_ref[:, g * gw:(g + 1) * gw]).astype(o_ref.dtype)


def _pad_row(v, offset):
    return jnp.zeros((1, LANES), F32).at[0, offset:offset + v.shape[0]].set(v.astype(F32))


def mamba2_ssd(p3, small3, conv_w, conv_b, dt_bias, a_log, d_skip, norm_w):
    bsz, seq, _ = p3.shape
    q = SSM_CHUNK
    nc = seq // q
    cch = SSM_INNER + SSM_BC
    const = lambda b, c: (0, 0)
    return pl.pallas_call(
        _ssd_kernel,
        grid=(bsz, nc),
        in_specs=[pl.BlockSpec((None, q, SSM_INNER), lambda b, c: (b, c, P_SSM_X // SSM_INNER)),
                  pl.BlockSpec((None, q, SSM_BC), lambda b, c: (b, c, P_SSM_BC // SSM_BC)),
                  pl.BlockSpec((None, q, SSM_INNER), lambda b, c: (b, c, P_SSM_Z // SSM_INNER)),
                  pl.BlockSpec((None, q, LANES), lambda b, c: (b, c, 0)),
                  pl.BlockSpec((SSM_CONV, cch), const),
                  pl.BlockSpec((1, cch), const),
                  pl.BlockSpec((1, LANES), const),
                  pl.BlockSpec((1, LANES), const),
                  pl.BlockSpec((1, SSM_INNER), const),
                  pl.BlockSpec((1, SSM_INNER), const)],
        out_specs=pl.BlockSpec((None, q, SSM_INNER), lambda b, c: (b, c, 0)),
        out_shape=jax.ShapeDtypeStruct((bsz, seq, SSM_INNER), BF16),
        scratch_shapes=[pltpu.VMEM((HALO, cch), F32),
                        pltpu.VMEM((SSM_HEADS // 2, SSM_STATE, LANES), F32),
                        pltpu.VMEM((HALO + q, cch), F32),
                        pltpu.VMEM((q, SSM_INNER), F32)],
        compiler_params=pltpu.CompilerParams(
            dimension_semantics=("parallel", "arbitrary"), vmem_limit_bytes=VMEM_LIMIT),
        name="mamba2_ssd",
    )(p3, p3, p3, small3, conv_w, conv_b.reshape(1, cch), _pad_row(dt_bias, S_DT),
      _pad_row(a_log, S_DT), jnp.repeat(d_skip.astype(F32), SSM_HEAD_DIM).reshape(1, SSM_INNER),
      norm_w.reshape(1, SSM_INNER))


def _gdn_kernel(qkv_ref, z_ref, sm_ref, cw_ref, dtb_ref, alog_ref, nw_ref,
                o_ref, tail_ref, state_ref, xe_ref):
    cs = GDN_CHUNK
    d = GDN_HEAD_DIM
    c = pl.program_id(1)

    @pl.when(c == 0)
    def _():
        tail_ref[...] = jnp.zeros_like(tail_ref)
        state_ref[...] = jnp.zeros_like(state_ref)

    xe_ref[0:HALO, :] = tail_ref[...]
    xe_ref[HALO:HALO + cs, :] = qkv_ref[...].astype(F32)
    tail_ref[...] = xe_ref[cs:cs + HALO, :]
    conv = cw_ref[0:1, :] * xe_ref[HALO - 3:HALO - 3 + cs, :]
    for t in range(1, GDN_CONV):
        conv = conv + cw_ref[t:t + 1, :] * xe_ref[HALO - 3 + t:HALO - 3 + t + cs, :]
    qkv = _silu(conv)

    sm = sm_ref[...]
    beta = 1.0 / (1.0 + jnp.exp(-sm))
    gl = -jnp.exp(alog_ref[...]) * _softplus(sm + dtb_ref[...])
    row = lax.broadcasted_iota(jnp.int32, (cs, cs), 0)
    colm = lax.broadcasted_iota(jnp.int32, (cs, cs), 1)
    incl = row >= colm
    strict = row > colm
    eye = (row == colm).astype(F32)
    g_cs = _dot_f32(incl.astype(F32), gl)
    g_cs_t = g_cs.T

    for h in range(GDN_HEADS):
        qh = qkv[:, h * d:(h + 1) * d]
        kh = qkv[:, GDN_WIDTH + h * d:GDN_WIDTH + (h + 1) * d]
        vh = qkv[:, 2 * GDN_WIDTH + h * d:2 * GDN_WIDTH + (h + 1) * d]
        qn = qh * lax.rsqrt(jnp.sum(qh * qh, axis=-1, keepdims=True) + L2_EPS) * (d ** -0.5)
        kn = kh * lax.rsqrt(jnp.sum(kh * kh, axis=-1, keepdims=True) + L2_EPS)
        bcol = beta[:, S_BETA + h:S_BETA + h + 1]
        gcol = g_cs[:, S_DECAY + h:S_DECAY + h + 1]
        grow = g_cs_t[S_DECAY + h:S_DECAY + h + 1, :]
        eg = jnp.exp(gcol)
        decay = jnp.where(incl, jnp.exp(jnp.where(incl, gcol - grow, 0.0)), 0.0)
        kb = kn * bcol
        vb = vh * bcol
        kn_b = kn.astype(BF16)
        a_mat = jnp.where(strict, _dot_nt(kb.astype(BF16), kn_b) * decay, 0.0)
        pw = -a_mat
        tinv = eye + pw
        for _ in range(int(math.log2(cs)) - 1):
            pw = _dot_f32(pw, pw)
            tinv = tinv + _dot_f32(tinv, pw)
        sol = _dot_f32(tinv, jnp.concatenate([vb, kb * eg], axis=1))
        u = sol[:, :d]
        w = sol[:, d:]
        st = state_ref[h]
        st_b = st.astype(BF16)
        attn = _dot_nt(qn.astype(BF16), kn_b) * decay
        v_new = u - _dot(w.astype(BF16), st_b)
        o = _dot((qn * eg).astype(BF16), st_b) + _dot(attn.astype(BF16), v_new.astype(BF16))
        g_last = grow[:, cs - 1:cs]
        k_dec = (kn * jnp.exp(g_last - gcol)).T.astype(BF16)
        state_ref[h] = st * jnp.exp(g_last) + _dot(k_dec, v_new.astype(BF16))
        ms = jnp.mean(o * o, axis=-1, keepdims=True)
        zh = z_ref[:, h * d:(h + 1) * d].astype(F32)
        o_ref[:, h * d:(h + 1) * d] = (
            o * lax.rsqrt(ms + RMS_EPS) * nw_ref[...] * _silu(zh)).astype(o_ref.dtype)


GDN_ROWS = 256


def _gdn_kernel2(qkv_ref, z_ref, sm_ref, cw_ref, dtb_ref, alog_ref, nw_ref,
                 o_ref, tail_ref, state_ref, xe_ref, act_ref):
    cs = GDN_CHUNK
    rr = GDN_ROWS
    nch = rr // cs
    d = GDN_HEAD_DIM
    sh = int(math.log2(cs))
    step = pl.program_id(1)

    @pl.when(step == 0)
    def _():
        tail_ref[...] = jnp.zeros_like(tail_ref)
        state_ref[...] = jnp.zeros_like(state_ref)

    xe_ref[0:HALO, :] = tail_ref[...]
    xe_ref[HALO:HALO + rr, :] = qkv_ref[...].astype(F32)
    tail_ref[...] = xe_ref[rr:rr + HALO, :]
    conv = cw_ref[0:1, :] * xe_ref[HALO - 3:HALO - 3 + rr, :]
    for t in range(1, GDN_CONV):
        conv = conv + cw_ref[t:t + 1, :] * xe_ref[HALO - 3 + t:HALO - 3 + t + rr, :]
    act_ref[...] = _silu(conv)

    sm = sm_ref[...]
    beta = 1.0 / (1.0 + jnp.exp(-sm))
    gl = -jnp.exp(alog_ref[...]) * _softplus(sm + dtb_ref[...])
    row = lax.broadcasted_iota(jnp.int32, (rr, rr), 0)
    colm = lax.broadcasted_iota(jnp.int32, (rr, rr), 1)
    same_blk = lax.shift_right_logical(row, sh) == lax.shift_right_logical(colm, sh)
    g_cs = _dot_f32((same_blk & (row >= colm)).astype(F32), gl)
    g_t = g_cs.T
    beta_t = beta.T

    l_idx = lax.broadcasted_iota(jnp.int32, (cs, rr), 0)
    j_idx = lax.broadcasted_iota(jnp.int32, (cs, rr), 1)
    s_idx = jnp.bitwise_and(j_idx, cs - 1)
    blk = lax.shift_right_logical(j_idx, sh)
    incl_cat = l_idx >= s_idx
    strict_cat = l_idx > s_idx
    eye_cat = (l_idx == s_idx).astype(F32)
    blk_row = lax.shift_right_logical(lax.broadcasted_iota(jnp.int32, (1, rr), 1), sh)

    def to_cat(x):
        if x.shape[1] != rr:
            x = jnp.concatenate([x] * (rr // x.shape[1]), axis=1)
        out = x[(nch - 1) * cs:]
        for c in reversed(range(nch - 1)):
            out = jnp.where(blk == c, x[c * cs:(c + 1) * cs], out)
        return out

    def to_bd(x_cat):
        return jnp.where(same_blk, jnp.concatenate([x_cat] * nch, axis=0), jnp.zeros((), x_cat.dtype))

    heads = range(GDN_HEADS)
    qn, kn_b, v_b, kn_t, attn_cat, p_cat, t_cat, grow, eg_full = ([None] * GDN_HEADS for _ in range(9))
    for h in heads:
        qh = act_ref[:, h * d:(h + 1) * d]
        kh = act_ref[:, GDN_WIDTH + h * d:GDN_WIDTH + (h + 1) * d]
        qn[h] = (qh * lax.rsqrt(jnp.sum(qh * qh, axis=-1, keepdims=True) + L2_EPS) * (d ** -0.5)).astype(BF16)
        kn = kh * lax.rsqrt(jnp.sum(kh * kh, axis=-1, keepdims=True) + L2_EPS)
        kn_b[h] = kn.astype(BF16)
        kn_t[h] = kn.T
        v_b[h] = act_ref[:, 2 * GDN_WIDTH + h * d:2 * GDN_WIDTH + (h + 1) * d].astype(BF16)
        gcol_full = jnp.broadcast_to(g_cs[:, S_DECAY + h:S_DECAY + h + 1], (rr, LANES))
        bcol_full = jnp.broadcast_to(beta[:, S_BETA + h:S_BETA + h + 1], (rr, LANES))
        eg_full[h] = jnp.exp(gcol_full)
        grow[h] = g_t[S_DECAY + h:S_DECAY + h + 1, :]
        dec = jnp.exp(jnp.where(incl_cat, to_cat(gcol_full) - grow[h], 0.0))
        kk = to_cat(_dot_nt(kn_b[h], kn_b[h]))
        qk = to_cat(_dot_nt(qn[h], kn_b[h]))
        attn_cat[h] = jnp.where(incl_cat, qk * dec, 0.0)
        p_cat[h] = jnp.where(strict_cat, -(kk * dec * to_cat(bcol_full)), 0.0)
        t_cat[h] = eye_cat + p_cat[h]

    p_bd = [to_bd(p_cat[h].astype(BF16)) for h in heads]
    for _ in range(sh - 1):
        for h in heads:
            p_cat[h] = _dot(p_cat[h].astype(BF16), p_bd[h])
        for h in heads:
            p_bd[h] = to_bd(p_cat[h].astype(BF16))
        for h in heads:
            t_cat[h] = t_cat[h] + _dot(t_cat[h].astype(BF16), p_bd[h])

    u, w = [None] * GDN_HEADS, [None] * GDN_HEADS
    for h in heads:
        brow = beta_t[S_BETA + h:S_BETA + h + 1, :]
        u[h] = _dot(to_bd((t_cat[h] * brow).astype(BF16)), v_b[h])
        w[h] = _dot(to_bd((t_cat[h] * (brow * jnp.exp(grow[h]))).astype(BF16)), kn_b[h])

    st = [state_ref[h] for h in heads]
    zero_blk = jnp.zeros((cs, d), BF16)
    for c in range(nch):
        rows = slice(c * cs, (c + 1) * cs)
        for h in heads:
            ws = _dot(jnp.concatenate([w[h][rows].astype(BF16), qn[h][rows]], axis=0), st[h].astype(BF16))
            v_new = (u[h][rows] - ws[:cs]).astype(BF16)
            rhs = jnp.concatenate([zero_blk] * c + [v_new] + [zero_blk] * (nch - 1 - c), axis=0)
            g_last = grow[h][:, (c + 1) * cs - 1:(c + 1) * cs]
            e_row = jnp.exp(jnp.where(blk_row == c, g_last - grow[h], 0.0))
            lhs = jnp.concatenate(
                [jnp.where(blk == c, attn_cat[h], 0.0),
                 jnp.where(blk_row == c, kn_t[h] * e_row, 0.0)], axis=0).astype(BF16)
            r = _dot(lhs, rhs)
            o = eg_full[h][rows] * ws[cs:] + r[:cs]
            st[h] = st[h] * jnp.exp(g_last) + r[cs:]
            ms = jnp.mean(o * o, axis=-1, keepdims=True)
            zh = z_ref[rows, h * d:(h + 1) * d].astype(F32)
            o_ref[rows, h * d:(h + 1) * d] = (
                o * lax.rsqrt(ms + RMS_EPS) * nw_ref[...] * _silu(zh)).astype(o_ref.dtype)
    for h in heads:
        state_ref[h] = st[h]


def gated_deltanet(p3, small3, conv_w, dt_bias, a_log, norm_w):
    bsz, seq, _ = p3.shape
    cs = GDN_ROWS
    nc = seq // cs
    w3 = 3 * GDN_WIDTH
    const = lambda b, c: (0, 0)
    return pl.pallas_call(
        _gdn_kernel2,
        grid=(bsz, nc),
        in_specs=[pl.BlockSpec((None, cs, w3), lambda b, c: (b, c, P_GDN_QKV // w3)),
                  pl.BlockSpec((None, cs, GDN_WIDTH), lambda b, c: (b, c, P_GDN_Z // GDN_WIDTH)),
                  pl.BlockSpec((None, cs, LANES), lambda b, c: (b, c, 0)),
                  pl.BlockSpec((GDN_CONV, w3), const),
                  pl.BlockSpec((1, LANES), const),
                  pl.BlockSpec((1, LANES), const),
                  pl.BlockSpec((1, GDN_HEAD_DIM), const)],
        out_specs=pl.BlockSpec((None, cs, GDN_WIDTH), lambda b, c: (b, c, 0)),
        out_shape=jax.ShapeDtypeStruct((bsz, seq, GDN_WIDTH), BF16),
        scratch_shapes=[pltpu.VMEM((HALO, w3), F32),
                        pltpu.VMEM((GDN_HEADS, GDN_HEAD_DIM, GDN_HEAD_DIM), F32),
                        pltpu.VMEM((HALO + cs, w3), F32),
                        pltpu.VMEM((cs, w3), F32)],
        compiler_params=pltpu.CompilerParams(
            dimension_semantics=("parallel", "arbitrary"), vmem_limit_bytes=VMEM_LIMIT),
        name="gated_deltanet",
    )(p3, p3, small3, conv_w, _pad_row(dt_bias, S_DECAY), _pad_row(a_log, S_DECAY),
      norm_w.reshape(1, GDN_HEAD_DIM))


def _merge_kernel(oa_ref, os_ref, og_ref, wb_ref, g0_ref, g1_ref, g2_ref, o_ref):
    acc = None
    for br, (x_ref, g_ref) in enumerate(((oa_ref, g0_ref), (os_ref, g1_ref), (og_ref, g2_ref))):
        gate = 1.0 / (1.0 + jnp.exp(-g_ref[...].astype(F32)))
        term = gate * _dot(x_ref[...], wb_ref[br])
        acc = term if acc is None else acc + term
    o_ref[...] = acc.astype(o_ref.dtype)


def merge_branches(o_da, o_ssm, o_gdn, wb3, p, *, tm=1024, tn=512):
    m = o_da.shape[0]
    gb = P_GATES // tn
    nb = D_MODEL // tn
    act = pl.BlockSpec((tm, DA_WIDTH), lambda i, j: (i, 0))
    return pl.pallas_call(
        _merge_kernel,
        grid=(m // tm, nb),
        in_specs=[act, act, act,
                  pl.BlockSpec((N_BRANCH, DA_WIDTH, tn), lambda i, j: (0, 0, j)),
                  pl.BlockSpec((tm, tn), lambda i, j: (i, gb + j)),
                  pl.BlockSpec((tm, tn), lambda i, j: (i, gb + nb + j)),
                  pl.BlockSpec((tm, tn), lambda i, j: (i, gb + 2 * nb + j))],
        out_specs=pl.BlockSpec((tm, tn), lambda i, j: (i, j)),
        out_shape=jax.ShapeDtypeStruct((m, D_MODEL), BF16),
        compiler_params=pltpu.CompilerParams(
            dimension_semantics=("parallel", "arbitrary"), vmem_limit_bytes=VMEM_LIMIT),
        name="merge_branches",
    )(o_da, o_ssm, o_gdn, wb3, p, p, p)


def _ffn_up_kernel(x_ref, wg_ref, wv_ref, cwg_ref, cwv_ref, cbg_ref, cbv_ref, o_ref, ug_ref, uv_ref,
                   *, rows):
    seq = x_ref.shape[0]
    pad = HALO
    ug_ref[0:pad, :] = jnp.zeros((pad, ug_ref.shape[1]), F32)
    uv_ref[0:pad, :] = jnp.zeros((pad, uv_ref.shape[1]), F32)
    for r in range(0, seq, rows):
        xr = x_ref[r:r + rows, :]
        ug_ref[pad + r:pad + r + rows, :] = _dot(xr, wg_ref[...])
        uv_ref[pad + r:pad + r + rows, :] = _dot(xr, wv_ref[...])
    for r in range(0, seq, rows):
        def conv(u_ref, cw_ref, cb_ref):
            acc = cb_ref[...] + cw_ref[0:1, :] * u_ref[pad + r - 2:pad + r - 2 + rows, :]
            for t in range(1, FFN_CONV):
                acc = acc + cw_ref[t:t + 1, :] * u_ref[pad + r - 2 + t:pad + r - 2 + t + rows, :]
            return acc
        gate = conv(ug_ref, cwg_ref, cbg_ref)
        val = conv(uv_ref, cwv_ref, cbv_ref)
        o_ref[r:r + rows, :] = (_silu(gate) * val).astype(o_ref.dtype)


def ffn_up_proj(hn3, w_up, conv_w, conv_b, *, tf=512, rows=256):
    bsz, seq, d = hn3.shape
    nf = D_FF // tf
    cb = conv_b.reshape(1, 2 * D_FF)
    return pl.pallas_call(
        functools.partial(_ffn_up_kernel, rows=rows),
        grid=(bsz, nf),
        in_specs=[pl.BlockSpec((None, seq, d), lambda b, f: (b, 0, 0)),
                  pl.BlockSpec((d, tf), lambda b, f: (0, f)),
                  pl.BlockSpec((d, tf), lambda b, f: (0, nf + f)),
                  pl.BlockSpec((FFN_CONV, tf), lambda b, f: (0, f)),
                  pl.BlockSpec((FFN_CONV, tf), lambda b, f: (0, nf + f)),
                  pl.BlockSpec((1, tf), lambda b, f: (0, f)),
                  pl.BlockSpec((1, tf), lambda b, f: (0, nf + f))],
        out_specs=pl.BlockSpec((None, seq, tf), lambda b, f: (b, 0, f)),
        out_shape=jax.ShapeDtypeStruct((bsz, seq, D_FF), BF16),
        scratch_shapes=[pltpu.VMEM((HALO + seq, tf), F32), pltpu.VMEM((HALO + seq, tf), F32)],
        compiler_params=pltpu.CompilerParams(
            dimension_semantics=("parallel", "arbitrary"), vmem_limit_bytes=VMEM_LIMIT),
        name="ffn_up",
    )(hn3, w_up, w_up, conv_w, conv_w, cb, cb)


def _reorder_w_in(w):
    sizes = (DA_WIDTH, DA_WIDTH, DA_WIDTH, SSM_INNER, SSM_INNER + SSM_BC, SSM_HEADS,
             3 * GDN_WIDTH, GDN_WIDTH, GDN_HEADS, GDN_HEADS, N_BRANCH * D_MODEL)
    offs = [0]
    for s in sizes:
        offs.append(offs[-1] + s)
    seg = lambda i: w[:, offs[i]:offs[i + 1]]
    da_q, da_k, da_v, ssm_z, ssm_xbc, ssm_dt, gdn_qkv, gdn_z, gdn_b, gdn_a, gates = (
        seg(i) for i in range(len(sizes)))
    big = jnp.concatenate([gdn_qkv, ssm_z, gdn_z, da_q, da_k, da_v, gates, ssm_xbc], axis=1)
    small = jnp.concatenate(
        [ssm_dt, gdn_b, gdn_a,
         jnp.zeros((w.shape[0], LANES - SSM_HEADS - 2 * GDN_HEADS), w.dtype)], axis=1)
    return big.astype(BF16), small.astype(BF16)


def kernel(x, norm_mix, w_in, da_lambda, da_subln, ssm_conv_w, ssm_conv_b, ssm_dt_bias, ssm_a_log,
           ssm_d, ssm_norm, gdn_conv_w, gdn_dt_bias, gdn_a_log, gdn_norm, w_branch, w_out, norm_ffn,
           ffn_up, ffn_conv_w, ffn_conv_b, ffn_down, norm_final):
    bsz, seq, d = x.shape
    m = bsz * seq
    h = x.reshape(m, d)
    hn = rmsnorm(h, norm_mix[0])
    out = None
    for l in range(DEPTH):
        w_big, w_small = _reorder_w_in(w_in[l])
        p = matmul(hn, w_big, tm=1024, tn=512, out_dtype=BF16, name="in_proj")
        small = matmul(hn, w_small, tm=1024, tn=LANES, out_dtype=F32, name="in_proj_small")
        p3 = p.reshape(bsz, seq, P_COLS)
        small3 = small.reshape(bsz, seq, LANES)
        lambda_init = 0.8 - 0.6 * math.exp(-0.3 * l)
        o_da = diff_attention(p3, da_lambda[l], da_subln[l], lambda_init)
        o_ssm = mamba2_ssd(p3, small3, ssm_conv_w[l], ssm_conv_b[l], ssm_dt_bias[l], ssm_a_log[l],
                           ssm_d[l], ssm_norm[l])
        o_gdn = gated_deltanet(p3, small3, gdn_conv_w[l], gdn_dt_bias[l], gdn_a_log[l], gdn_norm[l])
        wb3 = w_branch[l].astype(BF16).reshape(N_BRANCH, DA_WIDTH, D_MODEL)
        merged = merge_branches(o_da.reshape(m, DA_WIDTH), o_ssm.reshape(m, SSM_INNER),
                                o_gdn.reshape(m, GDN_WIDTH), wb3, p)
        h, hn = matmul_res_norm(merged, w_out[l].astype(BF16), h, norm_ffn[l], tm=512, tk=512,
                                emit_h=True, norm_dtype=BF16, name="out_proj")
        act = ffn_up_proj(hn.reshape(bsz, seq, d), ffn_up[l].astype(BF16), ffn_conv_w[l], ffn_conv_b[l])
        act = act.reshape(m, D_FF)
        if l + 1 < DEPTH:
            h, hn = matmul_res_norm(act, ffn_down[l].astype(BF16), h, norm_mix[l + 1], tm=512, tk=512,
                                    emit_h=True, norm_dtype=BF16, name="ffn_down")
        else:
            (out,) = matmul_res_norm(act, ffn_down[l].astype(BF16), h, norm_final, tm=512, tk=512,
                                     emit_h=False, norm_dtype=F32, name="ffn_down_final")
    return out.reshape(bsz, seq, d)
```

```python
import functools
import math

import jax
import jax.numpy as jnp
from jax import lax
from jax.experimental import pallas as pl
from jax.experimental.pallas import tpu as pltpu

F32 = jnp.float32
BF16 = jnp.bfloat16

D_MODEL = 2048
DEPTH = 2
DA_HEADS = 8
DA_HEAD_DIM = 64
DA_WIDTH = DA_HEADS * 2 * DA_HEAD_DIM
SSM_HEADS = 16
SSM_HEAD_DIM = 64
SSM_INNER = SSM_HEADS * SSM_HEAD_DIM
SSM_GROUPS = 2
SSM_STATE = 128
SSM_CONV = 4
SSM_CHUNK = 128
SSM_BC = 2 * SSM_GROUPS * SSM_STATE
GDN_HEADS = 8
GDN_HEAD_DIM = 128
GDN_WIDTH = GDN_HEADS * GDN_HEAD_DIM
GDN_CONV = 4
GDN_CHUNK = 64
GDN_ROWS = 256
D_FF = 5632
FFN_CONV = 3
N_BRANCH = 3
RMS_EPS = 1e-6
L2_EPS = 1e-6
LOG2E = math.log2(math.e)

LANES = 128
HALO = 8
VMEM_LIMIT = 56 * 1024 * 1024

P_GDN_QKV = 0
P_SSM_Z = P_GDN_QKV + 3 * GDN_WIDTH
P_GDN_Z = P_SSM_Z + SSM_INNER
P_DA_Q = P_GDN_Z + GDN_WIDTH
P_DA_K = P_DA_Q + DA_WIDTH
P_DA_V = P_DA_K + DA_WIDTH
P_GATES = P_DA_V + DA_WIDTH
P_SSM_X = P_GATES + N_BRANCH * D_MODEL
P_SSM_BC = P_SSM_X + SSM_INNER
P_COLS = P_SSM_BC + SSM_BC
S_DT = 0
S_BETA = SSM_HEADS
S_DECAY = SSM_HEADS + GDN_HEADS


def _silu(x):
    return x / (1.0 + jnp.exp(-x))


def _softplus(x):
    return jnp.maximum(x, 0.0) + jnp.log1p(jnp.exp(-jnp.abs(x)))


def _dot(a, b):
    return jnp.dot(a, b, preferred_element_type=F32)


def _dot_nt(a, b):
    return lax.dot_general(a, b, (((1,), (1,)), ((), ())), preferred_element_type=F32)


def _dot_f32(a, b):
    return jnp.dot(a, b, preferred_element_type=F32, precision=lax.Precision.HIGHEST)


def _rmsnorm_kernel(x_ref, w_ref, o_ref):
    x = x_ref[...]
    ms = jnp.mean(x * x, axis=-1, keepdims=True)
    o_ref[...] = (x * lax.rsqrt(ms + RMS_EPS) * w_ref[...]).astype(o_ref.dtype)


def rmsnorm(x, w, *, tm=512, out_dtype=BF16):
    m, d = x.shape
    return pl.pallas_call(
        _rmsnorm_kernel,
        grid=(m // tm,),
        in_specs=[pl.BlockSpec((tm, d), lambda i: (i, 0)),
                  pl.BlockSpec((1, d), lambda i: (0, 0))],
        out_specs=pl.BlockSpec((tm, d), lambda i: (i, 0)),
        out_shape=jax.ShapeDtypeStruct((m, d), out_dtype),
        name="rmsnorm",
    )(x, w.reshape(1, d))


def _norm_matmul_kernel(x_ref, nw_ref, w_ref, o_ref, xn_ref):
    @pl.when(pl.program_id(1) == 0)
    def _():
        x = x_ref[...]
        ms = jnp.mean(x * x, axis=-1, keepdims=True)
        xn_ref[...] = (x * lax.rsqrt(ms + RMS_EPS) * nw_ref[...]).astype(xn_ref.dtype)

    o_ref[...] = _dot(xn_ref[...], w_ref[...]).astype(o_ref.dtype)


def norm_matmul(x, nw, w, *, tm, tn, out_dtype, name):
    m, k = x.shape
    _, n = w.shape
    return pl.pallas_call(
        _norm_matmul_kernel,
        grid=(m // tm, n // tn),
        in_specs=[pl.BlockSpec((tm, k), lambda i, j: (i, 0)),
                  pl.BlockSpec((1, k), lambda i, j: (0, 0)),
                  pl.BlockSpec((k, tn), lambda i, j: (0, j))],
        out_specs=pl.BlockSpec((tm, tn), lambda i, j: (i, j)),
        out_shape=jax.ShapeDtypeStruct((m, n), out_dtype),
        scratch_shapes=[pltpu.VMEM((tm, k), BF16)],
        compiler_params=pltpu.CompilerParams(
            dimension_semantics=("parallel", "arbitrary"), vmem_limit_bytes=VMEM_LIMIT),
        name=name,
    )(x, nw.reshape(1, k), w)


def _matmul_res_norm_kernel(a_ref, w_ref, h_ref, nw_ref, ho_ref, no_ref):
    h = h_ref[...] + _dot(a_ref[...], w_ref[...])
    ho_ref[...] = h
    ms = jnp.mean(h * h, axis=-1, keepdims=True)
    no_ref[...] = (h * lax.rsqrt(ms + RMS_EPS) * nw_ref[...]).astype(no_ref.dtype)


def matmul_res_norm(a, w, h, nw, *, tm, name):
    m, k = a.shape
    _, n = w.shape
    row = lambda i: (i, 0)
    return pl.pallas_call(
        _matmul_res_norm_kernel,
        grid=(m // tm,),
        in_specs=[pl.BlockSpec((tm, k), row),
                  pl.BlockSpec((k, n), lambda i: (0, 0)),
                  pl.BlockSpec((tm, n), row),
                  pl.BlockSpec((1, n), lambda i: (0, 0))],
        out_specs=[pl.BlockSpec((tm, n), row), pl.BlockSpec((tm, n), row)],
        out_shape=[jax.ShapeDtypeStruct((m, n), F32), jax.ShapeDtypeStruct((m, n), BF16)],
        compiler_params=pltpu.CompilerParams(
            dimension_semantics=("parallel",), vmem_limit_bytes=VMEM_LIMIT),
        name=name,
    )(a, w, h, nw.reshape(1, n))


def _matmul_res_kernel(a_ref, w_ref, h_ref, o_ref):
    o_ref[...] = h_ref[...] + _dot(a_ref[...], w_ref[...])


def matmul_res(a, w, h, *, tm, tn, name):
    m, k = a.shape
    _, n = w.shape
    return pl.pallas_call(
        _matmul_res_kernel,
        grid=(n // tn, m // tm),
        in_specs=[pl.BlockSpec((tm, k), lambda j, i: (i, 0)),
                  pl.BlockSpec((k, tn), lambda j, i: (0, j)),
                  pl.BlockSpec((tm, tn), lambda j, i: (i, j))],
        out_specs=pl.BlockSpec((tm, tn), lambda j, i: (i, j)),
        out_shape=jax.ShapeDtypeStruct((m, n), F32),
        compiler_params=pltpu.CompilerParams(
            dimension_semantics=("parallel", "parallel"), vmem_limit_bytes=VMEM_LIMIT),
        name=name,
    )(a, w, h)


DA_BIAS_LANES = 3


def _diff_attn_kernel(slopes_ref, lamp_ref, subln_ref, q_ref, k_ref, v_ref, o_ref,
                      k1_ref, k2_ref, s_ref, st_ref, w_ref, *, tq, lambda_init):
    hd = DA_HEAD_DIM
    seq = q_ref.shape[0]
    nq = seq // tq
    half = tq // 2
    slope = slopes_ref[pl.program_id(1)]
    lp = lamp_ref[...]
    lam = (jnp.exp(jnp.sum(lp[0:1] * lp[1:2], axis=-1, keepdims=True))
           - jnp.exp(jnp.sum(lp[2:3] * lp[3:4], axis=-1, keepdims=True)) + lambda_init)

    lane_k = lax.broadcasted_iota(jnp.int32, (seq, 2 * hd), 1)
    bias = (slope * LOG2E) * lax.broadcasted_iota(jnp.int32, (seq, 2 * hd), 0).astype(F32)
    pieces = []
    rest = bias
    for _ in range(DA_BIAS_LANES):
        piece = rest.astype(BF16).astype(F32)
        pieces.append(piece)
        rest = rest - piece
    kf = k_ref[...].astype(F32)
    k1 = jnp.where(lane_k < hd, kf, 0.0)
    k2 = jnp.where(lane_k >= hd, kf, 0.0)
    for t, piece in enumerate(pieces):
        k1 = jnp.where(lane_k == hd + t, piece, k1)
        k2 = jnp.where(lane_k == t, piece, k2)
    k1_ref[...] = k1.astype(BF16)
    k2_ref[...] = k2.astype(BF16)

    lane_q = lax.broadcasted_iota(jnp.int32, (tq, 2 * hd), 1)
    causal = (lax.broadcasted_iota(jnp.int32, (tq, tq), 0)
              >= lax.broadcasted_iota(jnp.int32, (tq, tq), 1))

    def fold(x):
        return x[:, :half], x[:, half:]

    for qi in range(nq):
        rows = slice(qi * tq, (qi + 1) * tq)
        qf = q_ref[rows, :].astype(F32) * (hd ** -0.5 * LOG2E)
        q1 = jnp.where(lane_q < hd, qf, jnp.where(lane_q < hd + DA_BIAS_LANES, 1.0, 0.0)).astype(BF16)
        q2 = jnp.where(lane_q >= hd, qf, jnp.where(lane_q < DA_BIAS_LANES, 1.0, 0.0)).astype(BF16)

        mx = [None, None]
        for j in range(qi + 1):
            keys = slice(j * tq, (j + 1) * tq)
            for mp, (qm, km_ref) in enumerate(((q1, k1_ref), (q2, k2_ref))):
                s = _dot_nt(qm, km_ref[keys, :])
                if j == qi:
                    s = jnp.where(causal, s, -jnp.inf)
                s_ref[mp, j] = s
                a, b = fold(s)
                ab = jnp.maximum(a, b)
                mx[mp] = ab if mx[mp] is None else jnp.maximum(mx[mp], ab)
        for mp in range(2):
            st_ref[mp] = jnp.broadcast_to(jnp.max(mx[mp], axis=-1, keepdims=True), (tq, half))

        sm = [None, None]
        for j in range(qi + 1):
            for mp in range(2):
                m = st_ref[mp]
                a, b = fold(s_ref[mp, j])
                pa = jnp.exp2(a - m)
                pb = jnp.exp2(b - m)
                s_ref[mp, j] = jnp.concatenate([pa, pb], axis=1)
                sm[mp] = pa + pb if sm[mp] is None else sm[mp] + (pa + pb)
        l1 = jnp.sum(sm[0], axis=-1, keepdims=True)
        l2 = jnp.sum(sm[1], axis=-1, keepdims=True)
        st_ref[2] = jnp.broadcast_to(1.0 / l1, (tq, half))
        st_ref[3] = jnp.broadcast_to(lam / l2, (tq, half))

        for j in range(qi + 1):
            c1 = st_ref[2]
            c2 = st_ref[3]
            a1, b1 = fold(s_ref[0, j])
            a2, b2 = fold(s_ref[1, j])
            w_ref[:, j * tq:(j + 1) * tq] = jnp.concatenate(
                [a1 * c1 - a2 * c2, b1 * c1 - b2 * c2], axis=1).astype(BF16)
        kv = (qi + 1) * tq
        o = _dot(w_ref[:, :kv], v_ref[:kv, :])
        ms = jnp.mean(o * o, axis=-1, keepdims=True)
        o = o * lax.rsqrt(ms + RMS_EPS) * subln_ref[...] * (1.0 - lambda_init)
        o_ref[rows, :] = o.astype(o_ref.dtype)


def diff_attention(p3, lam_params, subln_w, lambda_init, *, tq=256):
    bsz, seq, _ = p3.shape
    hw = 2 * DA_HEAD_DIM
    nq = seq // tq
    slopes = jnp.asarray([2.0 ** (-8.0 * (h + 1) / DA_HEADS) for h in range(DA_HEADS)], F32)
    qb, kb, vb = P_DA_Q // hw, P_DA_K // hw, P_DA_V // hw
    return pl.pallas_call(
        functools.partial(_diff_attn_kernel, tq=tq, lambda_init=lambda_init),
        grid=(bsz, DA_HEADS),
        in_specs=[pl.BlockSpec(memory_space=pltpu.SMEM),
                  pl.BlockSpec((4, DA_HEAD_DIM), lambda b, h: (0, 0)),
                  pl.BlockSpec((1, hw), lambda b, h: (0, 0)),
                  pl.BlockSpec((None, seq, hw), lambda b, h: (b, 0, qb + h)),
                  pl.BlockSpec((None, seq, hw), lambda b, h: (b, 0, kb + h)),
                  pl.BlockSpec((None, seq, hw), lambda b, h: (b, 0, vb + h))],
        out_specs=pl.BlockSpec((None, seq, hw), lambda b, h: (b, 0, h)),
        out_shape=jax.ShapeDtypeStruct((bsz, seq, DA_WIDTH), BF16),
        scratch_shapes=[pltpu.VMEM((seq, hw), BF16),
                        pltpu.VMEM((seq, hw), BF16),
                        pltpu.VMEM((2, nq, tq, tq), F32),
                        pltpu.VMEM((4, tq, tq // 2), F32),
                        pltpu.VMEM((tq, seq), BF16)],
        compiler_params=pltpu.CompilerParams(
            dimension_semantics=("parallel", "parallel"), vmem_limit_bytes=VMEM_LIMIT),
        name="diff_attention",
    )(slopes, lam_params, subln_w.reshape(1, hw), p3, p3, p3)


def _ssd_kernel(x_ref, bc_ref, z_ref, sm_ref, cw_ref, cb_ref, dtb_ref, alog_ref, dskip_ref, nw_ref,
                o_ref, tail_ref, state_ref, xe_ref, y_ref):
    q = SSM_CHUNK
    c = pl.program_id(1)

    @pl.when(c == 0)
    def _():
        tail_ref[...] = jnp.zeros_like(tail_ref)
        state_ref[...] = jnp.zeros_like(state_ref)

    xe_ref[0:HALO, :] = tail_ref[...]
    xe_ref[HALO:HALO + q, 0:SSM_INNER] = x_ref[...].astype(F32)
    xe_ref[HALO:HALO + q, SSM_INNER:] = bc_ref[...].astype(F32)
    tail_ref[...] = xe_ref[q:q + HALO, :]
    conv = cb_ref[...] + cw_ref[0:1, :] * xe_ref[HALO - 3:HALO - 3 + q, :]
    for t in range(1, SSM_CONV):
        conv = conv + cw_ref[t:t + 1, :] * xe_ref[HALO - 3 + t:HALO - 3 + t + q, :]
    xbc = _silu(conv)
    xs = xbc[:, :SSM_INNER]
    xs_b = xs.astype(BF16)

    dt = _softplus(sm_ref[...] + dtb_ref[...])
    da = dt * (-jnp.exp(alog_ref[...]))
    row = lax.broadcasted_iota(jnp.int32, (q, q), 0)
    colm = lax.broadcasted_iota(jnp.int32, (q, q), 1)
    causal = row >= colm
    tri = causal.astype(F32)
    a_cs = _dot_f32(tri, da)
    a_cs_t = a_cs.T
    dt_t = dt.T
    lane = lax.broadcasted_iota(jnp.int32, (1, LANES), 1)

    for g in range(SSM_GROUPS):
        bm = xbc[:, SSM_INNER + g * SSM_STATE:SSM_INNER + (g + 1) * SSM_STATE]
        cm = xbc[:, SSM_INNER + (SSM_GROUPS + g) * SSM_STATE:
                 SSM_INNER + (SSM_GROUPS + g + 1) * SSM_STATE]
        cb = _dot_nt(cm.astype(BF16), bm.astype(BF16))
        bm_t = bm.T
        hpg = SSM_HEADS // SSM_GROUPS
        for pr in range(hpg // 2):
            pair = g * (hpg // 2) + pr
            x_pair = xs_b[:, pair * LANES:(pair + 1) * LANES]
            st = state_ref[pair]
            rhs = jnp.concatenate([x_pair, st.astype(BF16)], axis=0)
            ys, sts, cds = [], [], []
            for sub in range(2):
                h = 2 * pair + sub
                acol = a_cs[:, h:h + 1]
                arow = a_cs_t[h:h + 1, :]
                dtrow = dt_t[h:h + 1, :]
                decay = jnp.exp(jnp.where(causal, acol - arow, -jnp.inf))
                sc = (cb * decay * dtrow).astype(BF16)
                c_in = (cm * jnp.exp(acol)).astype(BF16)
                ys.append(_dot(jnp.concatenate([sc, c_in], axis=1), rhs))
                a_last = arow[:, q - 1:q]
                wrow = jnp.exp(a_last - arow) * dtrow
                sts.append(_dot((bm_t * wrow).astype(BF16), x_pair))
                cds.append(jnp.exp(a_last))
            first = lane < SSM_HEAD_DIM
            y_ref[:, pair * LANES:(pair + 1) * LANES] = jnp.where(first, ys[0], ys[1])
            state_ref[pair] = (st * jnp.where(first, cds[0], cds[1])
                               + jnp.where(first, sts[0], sts[1]))

    y = y_ref[...] + xs * dskip_ref[...]
    y = y * _silu(z_ref[...].astype(F32))
    gw = SSM_INNER // SSM_GROUPS
    for g in range(SSM_GROUPS):
        yg = y[:, g * gw:(g + 1) * gw]
        ms = jnp.mean(yg * yg, axis=-1, keepdims=True)
        o_ref[:, g * gw:(g + 1) * gw] = (
            yg * lax.rsqrt(ms + RMS_EPS) * nw_ref[:, g * gw:(g + 1) * gw]).astype(o_ref.dtype)


def _pad_row(v, offset):
    return jnp.zeros((1, LANES), F32).at[0, offset:offset + v.shape[0]].set(v.astype(F32))


def mamba2_ssd(p3, small3, conv_w, conv_b, dt_bias, a_log, d_skip, norm_w):
    bsz, seq, _ = p3.shape
    q = SSM_CHUNK
    nc = seq // q
    cch = SSM_INNER + SSM_BC
    const = lambda b, c: (0, 0)
    return pl.pallas_call(
        _ssd_kernel,
        grid=(bsz, nc),
        in_specs=[pl.BlockSpec((None, q, SSM_INNER), lambda b, c: (b, c, P_SSM_X // SSM_INNER)),
                  pl.BlockSpec((None, q, SSM_BC), lambda b, c: (b, c, P_SSM_BC // SSM_BC)),
                  pl.BlockSpec((None, q, SSM_INNER), lambda b, c: (b, c, P_SSM_Z // SSM_INNER)),
                  pl.BlockSpec((None, q, LANES), lambda b, c: (b, c, 0)),
                  pl.BlockSpec((SSM_CONV, cch), const),
                  pl.BlockSpec((1, cch), const),
                  pl.BlockSpec((1, LANES), const),
                  pl.BlockSpec((1, LANES), const),
                  pl.BlockSpec((1, SSM_INNER), const),
                  pl.BlockSpec((1, SSM_INNER), const)],
        out_specs=pl.BlockSpec((None, q, SSM_INNER), lambda b, c: (b, c, 0)),
        out_shape=jax.ShapeDtypeStruct((bsz, seq, SSM_INNER), BF16),
        scratch_shapes=[pltpu.VMEM((HALO, cch), F32),
                        pltpu.VMEM((SSM_HEADS // 2, SSM_STATE, LANES), F32),
                        pltpu.VMEM((HALO + q, cch), F32),
                        pltpu.VMEM((q, SSM_INNER), F32)],
        compiler_params=pltpu.CompilerParams(
            dimension_semantics=("parallel", "arbitrary"), vmem_limit_bytes=VMEM_LIMIT),
        name="mamba2_ssd",
    )(p3, p3, p3, small3, conv_w, conv_b.reshape(1, cch), _pad_row(dt_bias, S_DT),
      _pad_row(a_log, S_DT), jnp.repeat(d_skip.astype(F32), SSM_HEAD_DIM).reshape(1, SSM_INNER),
      norm_w.reshape(1, SSM_INNER))


def _gdn_kernel(qkv_ref, z_ref, sm_ref, cw_ref, dtb_ref, alog_ref, nw_ref,
                o_ref, tail_ref, state_ref, xe_ref, act_ref):
    cs = GDN_CHUNK
    rr = GDN_ROWS
    nch = rr // cs
    d = GDN_HEAD_DIM
    sh = int(math.log2(cs))
    step = pl.program_id(1)

    @pl.when(step == 0)
    def _():
        tail_ref[...] = jnp.zeros_like(tail_ref)
        state_ref[...] = jnp.zeros_like(state_ref)

    xe_ref[0:HALO, :] = tail_ref[...]
    xe_ref[HALO:HALO + rr, :] = qkv_ref[...].astype(F32)
    tail_ref[...] = xe_ref[rr:rr + HALO, :]
    conv = cw_ref[0:1, :] * xe_ref[HALO - 3:HALO - 3 + rr, :]
    for t in range(1, GDN_CONV):
        conv = conv + cw_ref[t:t + 1, :] * xe_ref[HALO - 3 + t:HALO - 3 + t + rr, :]
    act_ref[...] = _silu(conv)

    sm = sm_ref[...]
    beta = 1.0 / (1.0 + jnp.exp(-sm))
    gl = -jnp.exp(alog_ref[...]) * _softplus(sm + dtb_ref[...])
    row = lax.broadcasted_iota(jnp.int32, (rr, rr), 0)
    colm = lax.broadcasted_iota(jnp.int32, (rr, rr), 1)
    same_blk = lax.shift_right_logical(row, sh) == lax.shift_right_logical(colm, sh)
    g_cs = _dot_f32((same_blk & (row >= colm)).astype(F32), gl)
    g_t = g_cs.T
    beta_t = beta.T

    l_idx = lax.broadcasted_iota(jnp.int32, (cs, rr), 0)
    j_idx = lax.broadcasted_iota(jnp.int32, (cs, rr), 1)
    s_idx = jnp.bitwise_and(j_idx, cs - 1)
    blk = lax.shift_right_logical(j_idx, sh)
    incl_cat = l_idx >= s_idx
    strict_cat = l_idx > s_idx
    eye_cat = (l_idx == s_idx).astype(F32)
    blk_row = lax.shift_right_logical(lax.broadcasted_iota(jnp.int32, (1, rr), 1), sh)

    def to_cat(x):
        if x.shape[1] != rr:
            x = jnp.concatenate([x] * (rr // x.shape[1]), axis=1)
        out = x[(nch - 1) * cs:]
        for c in reversed(range(nch - 1)):
            out = jnp.where(blk == c, x[c * cs:(c + 1) * cs], out)
        return out

    def to_bd(x_cat):
        return jnp.where(same_blk, jnp.concatenate([x_cat] * nch, axis=0), jnp.zeros((), x_cat.dtype))

    heads = range(GDN_HEADS)
    qn, kn_b, v_b, kn_t, attn_cat, p_cat, t_cat, grow, eg_full = ([None] * GDN_HEADS for _ in range(9))
    for h in heads:
        qh = act_ref[:, h * d:(h + 1) * d]
        kh = act_ref[:, GDN_WIDTH + h * d:GDN_WIDTH + (h + 1) * d]
        qn[h] = (qh * lax.rsqrt(jnp.sum(qh * qh, axis=-1, keepdims=True) + L2_EPS) * (d ** -0.5)).astype(BF16)
        kn = kh * lax.rsqrt(jnp.sum(kh * kh, axis=-1, keepdims=True) + L2_EPS)
        kn_b[h] = kn.astype(BF16)
        kn_t[h] = kn.T
        v_b[h] = act_ref[:, 2 * GDN_WIDTH + h * d:2 * GDN_WIDTH + (h + 1) * d].astype(BF16)
        gcol_full = jnp.broadcast_to(g_cs[:, S_DECAY + h:S_DECAY + h + 1], (rr, LANES))
        bcol_full = jnp.broadcast_to(beta[:, S_BETA + h:S_BETA + h + 1], (rr, LANES))
        eg_full[h] = jnp.exp(gcol_full)
        grow[h] = g_t[S_DECAY + h:S_DECAY + h + 1, :]
        dec = jnp.exp(jnp.where(incl_cat, to_cat(gcol_full) - grow[h], 0.0))
        kk = to_cat(_dot_nt(kn_b[h], kn_b[h]))
        qk = to_cat(_dot_nt(qn[h], kn_b[h]))
        attn_cat[h] = jnp.where(incl_cat, qk * dec, 0.0)
        p_cat[h] = jnp.where(strict_cat, -(kk * dec * to_cat(bcol_full)), 0.0)
        t_cat[h] = eye_cat + p_cat[h]

    p_bd = [to_bd(p_cat[h].astype(BF16)) for h in heads]
    for _ in range(sh - 1):
        for h in heads:
            p_cat[h] = _dot(p_cat[h].astype(BF16), p_bd[h])
        for h in heads:
            p_bd[h] = to_bd(p_cat[h].astype(BF16))
        for h in heads:
            t_cat[h] = t_cat[h] + _dot(t_cat[h].astype(BF16), p_bd[h])

    u, w = [None] * GDN_HEADS, [None] * GDN_HEADS
    for h in heads:
        brow = beta_t[S_BETA + h:S_BETA + h + 1, :]
        u[h] = _dot(to_bd((t_cat[h] * brow).astype(BF16)), v_b[h])
        w[h] = _dot(to_bd((t_cat[h] * (brow * jnp.exp(grow[h]))).astype(BF16)), kn_b[h])

    st = [state_ref[h] for h in heads]
    zero_blk = jnp.zeros((cs, d), BF16)
    for c in range(nch):
        rows = slice(c * cs, (c + 1) * cs)
        for h in heads:
            ws = _dot(jnp.concatenate([w[h][rows].astype(BF16), qn[h][rows]], axis=0), st[h].astype(BF16))
            v_new = (u[h][rows] - ws[:cs]).astype(BF16)
            rhs = jnp.concatenate([zero_blk] * c + [v_new] + [zero_blk] * (nch - 1 - c), axis=0)
            g_last = grow[h][:, (c + 1) * cs - 1:(c + 1) * cs]
            e_row = jnp.exp(jnp.where(blk_row == c, g_last - grow[h], 0.0))
            lhs = jnp.concatenate(
                [jnp.where(blk == c, attn_cat[h], 0.0),
                 jnp.where(blk_row == c, kn_t[h] * e_row, 0.0)], axis=0).astype(BF16)
            r = _dot(lhs, rhs)
            o = eg_full[h][rows] * ws[cs:] + r[:cs]
            st[h] = st[h] * jnp.exp(g_last) + r[cs:]
            ms = jnp.mean(o * o, axis=-1, keepdims=True)
            zh = z_ref[rows, h * d:(h + 1) * d].astype(F32)
            o_ref[rows, h * d:(h + 1) * d] = (
                o * lax.rsqrt(ms + RMS_EPS) * nw_ref[...] * _silu(zh)).astype(o_ref.dtype)
    for h in heads:
        state_ref[h] = st[h]


def gated_deltanet(p3, small3, conv_w, dt_bias, a_log, norm_w):
    bsz, seq, _ = p3.shape
    rr = GDN_ROWS
    w3 = 3 * GDN_WIDTH
    const = lambda b, c: (0, 0)
    return pl.pallas_call(
        _gdn_kernel,
        grid=(bsz, seq // rr),
        in_specs=[pl.BlockSpec((None, rr, w3), lambda b, c: (b, c, P_GDN_QKV // w3)),
                  pl.BlockSpec((None, rr, GDN_WIDTH), lambda b, c: (b, c, P_GDN_Z // GDN_WIDTH)),
                  pl.BlockSpec((None, rr, LANES), lambda b, c: (b, c, 0)),
                  pl.BlockSpec((GDN_CONV, w3), const),
                  pl.BlockSpec((1, LANES), const),
                  pl.BlockSpec((1, LANES), const),
                  pl.BlockSpec((1, GDN_HEAD_DIM), const)],
        out_specs=pl.BlockSpec((None, rr, GDN_WIDTH), lambda b, c: (b, c, 0)),
        out_shape=jax.ShapeDtypeStruct((bsz, seq, GDN_WIDTH), BF16),
        scratch_shapes=[pltpu.VMEM((HALO, w3), F32),
                        pltpu.VMEM((GDN_HEADS, GDN_HEAD_DIM, GDN_HEAD_DIM), F32),
                        pltpu.VMEM((HALO + rr, w3), F32),
                        pltpu.VMEM((rr, w3), F32)],
        compiler_params=pltpu.CompilerParams(
            dimension_semantics=("parallel", "arbitrary"), vmem_limit_bytes=VMEM_LIMIT),
        name="gated_deltanet",
    )(p3, p3, small3, conv_w, _pad_row(dt_bias, S_DECAY), _pad_row(a_log, S_DECAY),
      norm_w.reshape(1, GDN_HEAD_DIM))


def _merge_kernel(oa_ref, os_ref, og_ref, wb_ref, g0_ref, g1_ref, g2_ref, o_ref):
    acc = None
    for br, (x_ref, g_ref) in enumerate(((oa_ref, g0_ref), (os_ref, g1_ref), (og_ref, g2_ref))):
        gate = 1.0 / (1.0 + jnp.exp(-g_ref[...].astype(F32)))
        term = gate * _dot(x_ref[...], wb_ref[br])
        acc = term if acc is None else acc + term
    o_ref[...] = acc.astype(o_ref.dtype)


def merge_branches(o_da, o_ssm, o_gdn, wb3, p, *, tm=1024, tn=512):
    m = o_da.shape[0]
    gb = P_GATES // tn
    nb = D_MODEL // tn
    act = pl.BlockSpec((tm, DA_WIDTH), lambda i, j: (i, 0))
    return pl.pallas_call(
        _merge_kernel,
        grid=(m // tm, nb),
        in_specs=[act, act, act,
                  pl.BlockSpec((N_BRANCH, DA_WIDTH, tn), lambda i, j: (0, 0, j)),
                  pl.BlockSpec((tm, tn), lambda i, j: (i, gb + j)),
                  pl.BlockSpec((tm, tn), lambda i, j: (i, gb + nb + j)),
                  pl.BlockSpec((tm, tn), lambda i, j: (i, gb + 2 * nb + j))],
        out_specs=pl.BlockSpec((tm, tn), lambda i, j: (i, j)),
        out_shape=jax.ShapeDtypeStruct((m, D_MODEL), BF16),
        compiler_params=pltpu.CompilerParams(
            dimension_semantics=("parallel", "arbitrary"), vmem_limit_bytes=VMEM_LIMIT),
        name="merge_branches",
    )(o_da, o_ssm, o_gdn, wb3, p, p, p)


def _ffn_up_kernel(x_ref, wg_ref, wv_ref, cwg_ref, cwv_ref, cbg_ref, cbv_ref, o_ref, ug_ref, uv_ref,
                   *, rows):
    seq = x_ref.shape[0]
    pad = HALO
    ug_ref[0:pad, :] = jnp.zeros((pad, ug_ref.shape[1]), F32)
    uv_ref[0:pad, :] = jnp.zeros((pad, uv_ref.shape[1]), F32)

    def project(r):
        xr = x_ref[r:r + rows, :]
        ug_ref[pad + r:pad + r + rows, :] = _dot(xr, wg_ref[...])
        uv_ref[pad + r:pad + r + rows, :] = _dot(xr, wv_ref[...])

    def conv(u_ref, cw_ref, cb_ref, r):
        acc = cb_ref[...] + cw_ref[0:1, :] * u_ref[pad + r - 2:pad + r - 2 + rows, :]
        for t in range(1, FFN_CONV):
            acc = acc + cw_ref[t:t + 1, :] * u_ref[pad + r - 2 + t:pad + r - 2 + t + rows, :]
        return acc

    def activate(r):
        gate = conv(ug_ref, cwg_ref, cbg_ref, r)
        val = conv(uv_ref, cwv_ref, cbv_ref, r)
        o_ref[r:r + rows, :] = (_silu(gate) * val).astype(o_ref.dtype)

    project(0)
    for r in range(rows, seq, rows):
        project(r)
        activate(r - rows)
    activate(seq - rows)


def ffn_up_proj(hn3, w_up, conv_w, conv_b, *, tf=512, rows=256):
    bsz, seq, d = hn3.shape
    nf = D_FF // tf
    cb = conv_b.reshape(1, 2 * D_FF)
    return pl.pallas_call(
        functools.partial(_ffn_up_kernel, rows=rows),
        grid=(bsz, nf),
        in_specs=[pl.BlockSpec((None, seq, d), lambda b, f: (b, 0, 0)),
                  pl.BlockSpec((d, tf), lambda b, f: (0, f)),
                  pl.BlockSpec((d, tf), lambda b, f: (0, nf + f)),
                  pl.BlockSpec((FFN_CONV, tf), lambda b, f: (0, f)),
                  pl.BlockSpec((FFN_CONV, tf), lambda b, f: (0, nf + f)),
                  pl.BlockSpec((1, tf), lambda b, f: (0, f)),
                  pl.BlockSpec((1, tf), lambda b, f: (0, nf + f))],
        out_specs=pl.BlockSpec((None, seq, tf), lambda b, f: (b, 0, f)),
        out_shape=jax.ShapeDtypeStruct((bsz, seq, D_FF), BF16),
        scratch_shapes=[pltpu.VMEM((HALO + seq, tf), F32), pltpu.VMEM((HALO + seq, tf), F32)],
        compiler_params=pltpu.CompilerParams(
            dimension_semantics=("parallel", "arbitrary"), vmem_limit_bytes=VMEM_LIMIT),
        name="ffn_up",
    )(hn3, w_up, w_up, conv_w, conv_w, cb, cb)


def _reorder_w_in(w):
    sizes = (DA_WIDTH, DA_WIDTH, DA_WIDTH, SSM_INNER, SSM_INNER + SSM_BC, SSM_HEADS,
             3 * GDN_WIDTH, GDN_WIDTH, GDN_HEADS, GDN_HEADS, N_BRANCH * D_MODEL)
    offs = [0]
    for s in sizes:
        offs.append(offs[-1] + s)
    seg = lambda i: w[:, offs[i]:offs[i + 1]]
    da_q, da_k, da_v, ssm_z, ssm_xbc, ssm_dt, gdn_qkv, gdn_z, gdn_b, gdn_a, gates = (
        seg(i) for i in range(len(sizes)))
    big = jnp.concatenate([gdn_qkv, ssm_z, gdn_z, da_q, da_k, da_v, gates, ssm_xbc], axis=1)
    small = jnp.concatenate(
        [ssm_dt, gdn_b, gdn_a,
         jnp.zeros((w.shape[0], LANES - SSM_HEADS - 2 * GDN_HEADS), w.dtype)], axis=1)
    return big.astype(BF16), small.astype(BF16)


def kernel(x, norm_mix, w_in, da_lambda, da_subln, ssm_conv_w, ssm_conv_b, ssm_dt_bias, ssm_a_log,
           ssm_d, ssm_norm, gdn_conv_w, gdn_dt_bias, gdn_a_log, gdn_norm, w_branch, w_out, norm_ffn,
           ffn_up, ffn_conv_w, ffn_conv_b, ffn_down, norm_final):
    bsz, seq, d = x.shape
    m = bsz * seq
    h = x.reshape(m, d)
    for l in range(DEPTH):
        w_big, w_small = _reorder_w_in(w_in[l])
        p = norm_matmul(h, norm_mix[l], w_big, tm=1024, tn=512, out_dtype=BF16, name="in_proj")
        small = norm_matmul(h, norm_mix[l], w_small, tm=1024, tn=LANES, out_dtype=F32, name="in_proj_small")
        p3 = p.reshape(bsz, seq, P_COLS)
        small3 = small.reshape(bsz, seq, LANES)
        lambda_init = 0.8 - 0.6 * math.exp(-0.3 * l)
        o_da = diff_attention(p3, da_lambda[l], da_subln[l], lambda_init)
        o_ssm = mamba2_ssd(p3, small3, ssm_conv_w[l], ssm_conv_b[l], ssm_dt_bias[l], ssm_a_log[l],
                           ssm_d[l], ssm_norm[l])
        o_gdn = gated_deltanet(p3, small3, gdn_conv_w[l], gdn_dt_bias[l], gdn_a_log[l], gdn_norm[l])
        wb3 = w_branch[l].astype(BF16).reshape(N_BRANCH, DA_WIDTH, D_MODEL)
        merged = merge_branches(o_da.reshape(m, DA_WIDTH), o_ssm.reshape(m, SSM_INNER),
                                o_gdn.reshape(m, GDN_WIDTH), wb3, p)
        h, hn = matmul_res_norm(merged, w_out[l].astype(BF16), h, norm_ffn[l], tm=512, name="out_proj")
        act = ffn_up_proj(hn.reshape(bsz, seq, d), ffn_up[l].astype(BF16), ffn_conv_w[l], ffn_conv_b[l])
        h = matmul_res(act.reshape(m, D_FF), ffn_down[l].astype(BF16), h, tm=512, tn=1024, name="ffn_down")
    return rmsnorm(h, norm_final, out_dtype=F32).reshape(bsz, seq, d)
```

```python
import functools
import math

import jax
import jax.numpy as jnp
from jax import lax
from jax.experimental import pallas as pl
from jax.experimental.pallas import tpu as pltpu

F32 = jnp.float32
BF16 = jnp.bfloat16

D_MODEL = 2048
DEPTH = 2
DA_HEADS = 8
DA_HEAD_DIM = 64
DA_WIDTH = DA_HEADS * 2 * DA_HEAD_DIM
SSM_HEADS = 16
SSM_HEAD_DIM = 64
SSM_INNER = SSM_HEADS * SSM_HEAD_DIM
SSM_GROUPS = 2
SSM_STATE = 128
SSM_CONV = 4
SSM_CHUNK = 128
SSM_BC = 2 * SSM_GROUPS * SSM_STATE
GDN_HEADS = 8
GDN_HEAD_DIM = 128
GDN_WIDTH = GDN_HEADS * GDN_HEAD_DIM
GDN_CONV = 4
GDN_CHUNK = 64
GDN_ROWS = 256
D_FF = 5632
FFN_CONV = 3
N_BRANCH = 3
RMS_EPS = 1e-6
L2_EPS = 1e-6
LOG2E = math.log2(math.e)

LANES = 128
HALO = 8
VMEM_LIMIT = 56 * 1024 * 1024

P_GDN_QKV = 0
P_SSM_Z = P_GDN_QKV + 3 * GDN_WIDTH
P_GDN_Z = P_SSM_Z + SSM_INNER
P_DA_Q = P_GDN_Z + GDN_WIDTH
P_DA_K = P_DA_Q + DA_WIDTH
P_DA_V = P_DA_K + DA_WIDTH
P_GATES = P_DA_V + DA_WIDTH
P_SSM_X = P_GATES + N_BRANCH * D_MODEL
P_SSM_BC = P_SSM_X + SSM_INNER
P_COLS = P_SSM_BC + SSM_BC
S_DT = 0
S_BETA = SSM_HEADS
S_DECAY = SSM_HEADS + GDN_HEADS


def _silu(x):
    return x / (1.0 + jnp.exp(-x))


def _softplus(x):
    return jnp.maximum(x, 0.0) + jnp.log1p(jnp.exp(-jnp.abs(x)))


def _dot(a, b):
    return jnp.dot(a, b, preferred_element_type=F32)


def _dot_nt(a, b):
    return lax.dot_general(a, b, (((1,), (1,)), ((), ())), preferred_element_type=F32)


def _dot_f32(a, b):
    return jnp.dot(a, b, preferred_element_type=F32, precision=lax.Precision.HIGHEST)


def _rmsnorm_kernel(x_ref, w_ref, o_ref):
    x = x_ref[...]
    ms = jnp.mean(x * x, axis=-1, keepdims=True)
    o_ref[...] = (x * lax.rsqrt(ms + RMS_EPS) * w_ref[...]).astype(o_ref.dtype)


def rmsnorm(x, w, *, tm=512, out_dtype=BF16):
    m, d = x.shape
    return pl.pallas_call(
        _rmsnorm_kernel,
        grid=(m // tm,),
        in_specs=[pl.BlockSpec((tm, d), lambda i: (i, 0)),
                  pl.BlockSpec((1, d), lambda i: (0, 0))],
        out_specs=pl.BlockSpec((tm, d), lambda i: (i, 0)),
        out_shape=jax.ShapeDtypeStruct((m, d), out_dtype),
        name="rmsnorm",
    )(x, w.reshape(1, d))


IN_TM = 1024
IN_TN = 512
IN_ROWS = 256


def _in_proj_kernel(x_ref, nw_ref, w_ref, ws_ref, cw_ref, cb_ref, o_ref, small_ref, xn_ref, u_ref, halo_ref,
                    *, conv_front, conv_back, tiles_per_seq):
    i = pl.program_id(0)
    j = pl.program_id(1)
    tm = x_ref.shape[0]

    @pl.when(j == 0)
    def _():
        x = x_ref[...]
        ms = jnp.mean(x * x, axis=-1, keepdims=True)
        xn_ref[...] = (x * lax.rsqrt(ms + RMS_EPS) * nw_ref[...]).astype(xn_ref.dtype)
        small_ref[...] = _dot(xn_ref[...], ws_ref[...])

    is_conv = (j < conv_front) | (j >= conv_back)

    @pl.when(jnp.logical_not(is_conv))
    def _():
        o_ref[...] = _dot(xn_ref[...], w_ref[...]).astype(o_ref.dtype)

    @pl.when(is_conv)
    def _():
        first = lax.rem(i, tiles_per_seq) == 0

        @pl.when(first)
        def _():
            u_ref[0:HALO, :] = jnp.zeros((HALO, u_ref.shape[1]), F32)

        @pl.when(jnp.logical_not(first))
        def _():
            u_ref[0:HALO, :] = halo_ref[j]

        def project(r):
            u_ref[HALO + r:HALO + r + IN_ROWS, :] = _dot(xn_ref[r:r + IN_ROWS, :], w_ref[...])

        def activate(r):
            acc = cb_ref[...] + cw_ref[0:1, :] * u_ref[HALO + r - 3:HALO + r - 3 + IN_ROWS, :]
            for t in range(1, SSM_CONV):
                acc = acc + cw_ref[t:t + 1, :] * u_ref[HALO + r - 3 + t:HALO + r - 3 + t + IN_ROWS, :]
            o_ref[r:r + IN_ROWS, :] = _silu(acc).astype(o_ref.dtype)

        project(0)
        for r in range(IN_ROWS, tm, IN_ROWS):
            activate(r - IN_ROWS)
            project(r)
        activate(tm - IN_ROWS)
        halo_ref[j] = u_ref[tm:tm + HALO, :]


def in_proj(x, nw, w, w_small, conv_w, conv_b, *, seq):
    m, k = x.shape
    assert GDN_CONV == SSM_CONV and P_GDN_QKV == 0 and P_SSM_X % IN_TN == 0 and (3 * GDN_WIDTH) % IN_TN == 0
    n_tiles = P_COLS // IN_TN
    return pl.pallas_call(
        functools.partial(_in_proj_kernel, conv_front=3 * GDN_WIDTH // IN_TN, conv_back=P_SSM_X // IN_TN,
                          tiles_per_seq=seq // IN_TM),
        grid=(m // IN_TM, n_tiles),
        in_specs=[pl.BlockSpec((IN_TM, k), lambda i, j: (i, 0)),
                  pl.BlockSpec((1, k), lambda i, j: (0, 0)),
                  pl.BlockSpec((k, IN_TN), lambda i, j: (0, j)),
                  pl.BlockSpec((k, LANES), lambda i, j: (0, 0)),
                  pl.BlockSpec((SSM_CONV, IN_TN), lambda i, j: (0, j)),
                  pl.BlockSpec((1, IN_TN), lambda i, j: (0, j))],
        out_specs=[pl.BlockSpec((IN_TM, IN_TN), lambda i, j: (i, j)),
                   pl.BlockSpec((IN_TM, LANES), lambda i, j: (i, 0))],
        out_shape=[jax.ShapeDtypeStruct((m, P_COLS), BF16), jax.ShapeDtypeStruct((m, LANES), F32)],
        scratch_shapes=[pltpu.VMEM((IN_TM, k), BF16),
                        pltpu.VMEM((HALO + IN_TM, IN_TN), F32),
                        pltpu.VMEM((n_tiles, HALO, IN_TN), F32)],
        compiler_params=pltpu.CompilerParams(
            dimension_semantics=("arbitrary", "arbitrary"), vmem_limit_bytes=VMEM_LIMIT),
        name="in_proj",
    )(x, nw.reshape(1, k), w, w_small, conv_w, conv_b)


def _matmul_res_norm_kernel(a_ref, w_ref, h_ref, nw_ref, ho_ref, no_ref):
    h = h_ref[...] + _dot(a_ref[...], w_ref[...])
    ho_ref[...] = h
    ms = jnp.mean(h * h, axis=-1, keepdims=True)
    no_ref[...] = (h * lax.rsqrt(ms + RMS_EPS) * nw_ref[...]).astype(no_ref.dtype)


def matmul_res_norm(a, w, h, nw, *, tm, name):
    m, k = a.shape
    _, n = w.shape
    row = lambda i: (i, 0)
    return pl.pallas_call(
        _matmul_res_norm_kernel,
        grid=(m // tm,),
        in_specs=[pl.BlockSpec((tm, k), row),
                  pl.BlockSpec((k, n), lambda i: (0, 0)),
                  pl.BlockSpec((tm, n), row),
                  pl.BlockSpec((1, n), lambda i: (0, 0))],
        out_specs=[pl.BlockSpec((tm, n), row), pl.BlockSpec((tm, n), row)],
        out_shape=[jax.ShapeDtypeStruct((m, n), F32), jax.ShapeDtypeStruct((m, n), BF16)],
        compiler_params=pltpu.CompilerParams(
            dimension_semantics=("parallel",), vmem_limit_bytes=VMEM_LIMIT),
        name=name,
    )(a, w, h, nw.reshape(1, n))


def _matmul_res_kernel(a_ref, w_ref, h_ref, o_ref):
    o_ref[...] = h_ref[...] + _dot(a_ref[...], w_ref[...])


def matmul_res(a, w, h, *, tm, tn, name):
    m, k = a.shape
    _, n = w.shape
    return pl.pallas_call(
        _matmul_res_kernel,
        grid=(n // tn, m // tm),
        in_specs=[pl.BlockSpec((tm, k), lambda j, i: (i, 0)),
                  pl.BlockSpec((k, tn), lambda j, i: (0, j)),
                  pl.BlockSpec((tm, tn), lambda j, i: (i, j))],
        out_specs=pl.BlockSpec((tm, tn), lambda j, i: (i, j)),
        out_shape=jax.ShapeDtypeStruct((m, n), F32),
        compiler_params=pltpu.CompilerParams(
            dimension_semantics=("parallel", "parallel"), vmem_limit_bytes=VMEM_LIMIT),
        name=name,
    )(a, w, h)


DA_BIAS_LANES = 3


def _diff_attn_kernel(slopes_ref, lamp_ref, subln_ref, q_ref, k_ref, v_ref, o_ref,
                      k1_ref, k2_ref, s_ref, st_ref, w_ref, *, tq, lambda_init):
    hd = DA_HEAD_DIM
    seq = q_ref.shape[0]
    nq = seq // tq
    half = tq // 2
    slope = slopes_ref[pl.program_id(1)]
    lp = lamp_ref[...]
    lam = (jnp.exp(jnp.sum(lp[0:1] * lp[1:2], axis=-1, keepdims=True))
           - jnp.exp(jnp.sum(lp[2:3] * lp[3:4], axis=-1, keepdims=True)) + lambda_init)

    lane_k = lax.broadcasted_iota(jnp.int32, (seq, 2 * hd), 1)
    bias = (slope * LOG2E) * lax.broadcasted_iota(jnp.int32, (seq, 2 * hd), 0).astype(F32)
    pieces = []
    rest = bias
    for _ in range(DA_BIAS_LANES):
        piece = rest.astype(BF16).astype(F32)
        pieces.append(piece)
        rest = rest - piece
    kf = k_ref[...].astype(F32)
    k1 = jnp.where(lane_k < hd, kf, 0.0)
    k2 = jnp.where(lane_k >= hd, kf, 0.0)
    for t, piece in enumerate(pieces):
        k1 = jnp.where(lane_k == hd + t, piece, k1)
        k2 = jnp.where(lane_k == t, piece, k2)
    k1_ref[...] = k1.astype(BF16)
    k2_ref[...] = k2.astype(BF16)

    lane_q = lax.broadcasted_iota(jnp.int32, (tq, 2 * hd), 1)
    causal = (lax.broadcasted_iota(jnp.int32, (tq, tq), 0)
              >= lax.broadcasted_iota(jnp.int32, (tq, tq), 1))

    def fold(x):
        return x[:, :half], x[:, half:]

    for qi in range(nq):
        rows = slice(qi * tq, (qi + 1) * tq)
        qf = q_ref[rows, :].astype(F32) * (hd ** -0.5 * LOG2E)
        q1 = jnp.where(lane_q < hd, qf, jnp.where(lane_q < hd + DA_BIAS_LANES, 1.0, 0.0)).astype(BF16)
        q2 = jnp.where(lane_q >= hd, qf, jnp.where(lane_q < DA_BIAS_LANES, 1.0, 0.0)).astype(BF16)

        mx = [None, None]
        for j in range(qi + 1):
            keys = slice(j * tq, (j + 1) * tq)
            for mp, (qm, km_ref) in enumerate(((q1, k1_ref), (q2, k2_ref))):
                s = _dot_nt(qm, km_ref[keys, :])
                if j == qi:
                    s = jnp.where(causal, s, -jnp.inf)
                s_ref[mp, j] = s
                a, b = fold(s)
                ab = jnp.maximum(a, b)
                mx[mp] = ab if mx[mp] is None else jnp.maximum(mx[mp], ab)
        for mp in range(2):
            st_ref[mp] = jnp.broadcast_to(jnp.max(mx[mp], axis=-1, keepdims=True), (tq, half))

        sm = [None, None]
        for j in range(qi + 1):
            for mp in range(2):
                m = st_ref[mp]
                a, b = fold(s_ref[mp, j])
                pa = jnp.exp2(a - m)
                pb = jnp.exp2(b - m)
                s_ref[mp, j] = jnp.concatenate([pa, pb], axis=1)
                sm[mp] = pa + pb if sm[mp] is None else sm[mp] + (pa + pb)
        l1 = jnp.sum(sm[0], axis=-1, keepdims=True)
        l2 = jnp.sum(sm[1], axis=-1, keepdims=True)
        st_ref[2] = jnp.broadcast_to(1.0 / l1, (tq, half))
        st_ref[3] = jnp.broadcast_to(lam / l2, (tq, half))

        for j in range(qi + 1):
            c1 = st_ref[2]
            c2 = st_ref[3]
            a1, b1 = fold(s_ref[0, j])
            a2, b2 = fold(s_ref[1, j])
            w_ref[:, j * tq:(j + 1) * tq] = jnp.concatenate(
                [a1 * c1 - a2 * c2, b1 * c1 - b2 * c2], axis=1).astype(BF16)
        kv = (qi + 1) * tq
        o = _dot(w_ref[:, :kv], v_ref[:kv, :])
        ms = jnp.mean(o * o, axis=-1, keepdims=True)
        o = o * lax.rsqrt(ms + RMS_EPS) * subln_ref[...] * (1.0 - lambda_init)
        o_ref[rows, :] = o.astype(o_ref.dtype)


def diff_attention(p3, lam_params, subln_w, lambda_init, *, tq=256):
    bsz, seq, _ = p3.shape
    hw = 2 * DA_HEAD_DIM
    nq = seq // tq
    slopes = jnp.asarray([2.0 ** (-8.0 * (h + 1) / DA_HEADS) for h in range(DA_HEADS)], F32)
    qb, kb, vb = P_DA_Q // hw, P_DA_K // hw, P_DA_V // hw
    return pl.pallas_call(
        functools.partial(_diff_attn_kernel, tq=tq, lambda_init=lambda_init),
        grid=(bsz, DA_HEADS),
        in_specs=[pl.BlockSpec(memory_space=pltpu.SMEM),
                  pl.BlockSpec((4, DA_HEAD_DIM), lambda b, h: (0, 0)),
                  pl.BlockSpec((1, hw), lambda b, h: (0, 0)),
                  pl.BlockSpec((None, seq, hw), lambda b, h: (b, 0, qb + h)),
                  pl.BlockSpec((None, seq, hw), lambda b, h: (b, 0, kb + h)),
                  pl.BlockSpec((None, seq, hw), lambda b, h: (b, 0, vb + h))],
        out_specs=pl.BlockSpec((None, seq, hw), lambda b, h: (b, 0, h)),
        out_shape=jax.ShapeDtypeStruct((bsz, seq, DA_WIDTH), BF16),
        scratch_shapes=[pltpu.VMEM((seq, hw), BF16),
                        pltpu.VMEM((seq, hw), BF16),
                        pltpu.VMEM((2, nq, tq, tq), F32),
                        pltpu.VMEM((4, tq, tq // 2), F32),
                        pltpu.VMEM((tq, seq), BF16)],
        compiler_params=pltpu.CompilerParams(
            dimension_semantics=("parallel", "parallel"), vmem_limit_bytes=VMEM_LIMIT),
        name="diff_attention",
    )(slopes, lam_params, subln_w.reshape(1, hw), p3, p3, p3)


def _ssd_kernel(x_ref, bc_ref, z_ref, sm_ref, dtb_ref, alog_ref, dskip_ref, nw_ref,
                o_ref, state_ref, y_ref):
    q = SSM_CHUNK
    c = pl.program_id(1)

    @pl.when(c == 0)
    def _():
        state_ref[...] = jnp.zeros_like(state_ref)

    xs_b = x_ref[...]
    xs = xs_b.astype(F32)
    bc = bc_ref[...]

    dt = _softplus(sm_ref[...] + dtb_ref[...])
    da = dt * (-jnp.exp(alog_ref[...]))
    row = lax.broadcasted_iota(jnp.int32, (q, q), 0)
    colm = lax.broadcasted_iota(jnp.int32, (q, q), 1)
    causal = row >= colm
    tri = causal.astype(F32)
    a_cs = _dot_f32(tri, da)
    a_cs_t = a_cs.T
    dt_t = dt.T
    lane = lax.broadcasted_iota(jnp.int32, (1, LANES), 1)

    for g in range(SSM_GROUPS):
        bm_b = bc[:, g * SSM_STATE:(g + 1) * SSM_STATE]
        cm_b = bc[:, (SSM_GROUPS + g) * SSM_STATE:(SSM_GROUPS + g + 1) * SSM_STATE]
        cm = cm_b.astype(F32)
        cb = _dot_nt(cm_b, bm_b)
        bm_t = bm_b.astype(F32).T
        hpg = SSM_HEADS // SSM_GROUPS
        for pr in range(hpg // 2):
            pair = g * (hpg // 2) + pr
            x_pair = xs_b[:, pair * LANES:(pair + 1) * LANES]
            st = state_ref[pair]
            rhs = jnp.concatenate([x_pair, st.astype(BF16)], axis=0)
            ys, sts, cds = [], [], []
            for sub in range(2):
                h = 2 * pair + sub
                acol = a_cs[:, h:h + 1]
                arow = a_cs_t[h:h + 1, :]
                dtrow = dt_t[h:h + 1, :]
                decay = jnp.exp(jnp.where(causal, acol - arow, -jnp.inf))
                sc = (cb * decay * dtrow).astype(BF16)
                c_in = (cm * jnp.exp(acol)).astype(BF16)
                ys.append(_dot(jnp.concatenate([sc, c_in], axis=1), rhs))
                a_last = arow[:, q - 1:q]
                wrow = jnp.exp(a_last - arow) * dtrow
                sts.append(_dot((bm_t * wrow).astype(BF16), x_pair))
                cds.append(jnp.exp(a_last))
            first = lane < SSM_HEAD_DIM
            y_ref[:, pair * LANES:(pair + 1) * LANES] = jnp.where(first, ys[0], ys[1])
            state_ref[pair] = (st * jnp.where(first, cds[0], cds[1])
                               + jnp.where(first, sts[0], sts[1]))

    y = y_ref[...] + xs * dskip_ref[...]
    y = y * _silu(z_ref[...].astype(F32))
    gw = SSM_INNER // SSM_GROUPS
    for g in range(SSM_GROUPS):
        yg = y[:, g * gw:(g + 1) * gw]
        ms = jnp.mean(yg * yg, axis=-1, keepdims=True)
        o_ref[:, g * gw:(g + 1) * gw] = (
            yg * lax.rsqrt(ms + RMS_EPS) * nw_ref[:, g * gw:(g + 1) * gw]).astype(o_ref.dtype)


def _pad_row(v, offset):
    return jnp.zeros((1, LANES), F32).at[0, offset:offset + v.shape[0]].set(v.astype(F32))


def mamba2_ssd(p3, small3, dt_bias, a_log, d_skip, norm_w):
    bsz, seq, _ = p3.shape
    q = SSM_CHUNK
    nc = seq // q
    const = lambda b, c: (0, 0)
    return pl.pallas_call(
        _ssd_kernel,
        grid=(bsz, nc),
        in_specs=[pl.BlockSpec((None, q, SSM_INNER), lambda b, c: (b, c, P_SSM_X // SSM_INNER)),
                  pl.BlockSpec((None, q, SSM_BC), lambda b, c: (b, c, P_SSM_BC // SSM_BC)),
                  pl.BlockSpec((None, q, SSM_INNER), lambda b, c: (b, c, P_SSM_Z // SSM_INNER)),
                  pl.BlockSpec((None, q, LANES), lambda b, c: (b, c, 0)),
                  pl.BlockSpec((1, LANES), const),
                  pl.BlockSpec((1, LANES), const),
                  pl.BlockSpec((1, SSM_INNER), const),
                  pl.BlockSpec((1, SSM_INNER), const)],
        out_specs=pl.BlockSpec((None, q, SSM_INNER), lambda b, c: (b, c, 0)),
        out_shape=jax.ShapeDtypeStruct((bsz, seq, SSM_INNER), BF16),
        scratch_shapes=[pltpu.VMEM((SSM_HEADS // 2, SSM_STATE, LANES), F32),
                        pltpu.VMEM((q, SSM_INNER), F32)],
        compiler_params=pltpu.CompilerParams(
            dimension_semantics=("parallel", "arbitrary"), vmem_limit_bytes=VMEM_LIMIT),
        name="mamba2_ssd",
    )(p3, p3, p3, small3, _pad_row(dt_bias, S_DT),
      _pad_row(a_log, S_DT), jnp.repeat(d_skip.astype(F32), SSM_HEAD_DIM).reshape(1, SSM_INNER),
      norm_w.reshape(1, SSM_INNER))


def _gdn_kernel(act_ref, z_ref, sm_ref, dtb_ref, alog_ref, nw_ref, o_ref, state_ref):
    cs = GDN_CHUNK
    rr = GDN_ROWS
    nch = rr // cs
    d = GDN_HEAD_DIM
    sh = int(math.log2(cs))
    step = pl.program_id(1)

    @pl.when(step == 0)
    def _():
        state_ref[...] = jnp.zeros_like(state_ref)

    sm = sm_ref[...]
    beta = 1.0 / (1.0 + jnp.exp(-sm))
    gl = -jnp.exp(alog_ref[...]) * _softplus(sm + dtb_ref[...])
    row = lax.broadcasted_iota(jnp.int32, (rr, rr), 0)
    colm = lax.broadcasted_iota(jnp.int32, (rr, rr), 1)
    same_blk = lax.shift_right_logical(row, sh) == lax.shift_right_logical(colm, sh)
    g_cs = _dot_f32((same_blk & (row >= colm)).astype(F32), gl)
    g_t = g_cs.T
    beta_t = beta.T

    l_idx = lax.broadcasted_iota(jnp.int32, (cs, rr), 0)
    j_idx = lax.broadcasted_iota(jnp.int32, (cs, rr), 1)
    s_idx = jnp.bitwise_and(j_idx, cs - 1)
    blk = lax.shift_right_logical(j_idx, sh)
    incl_cat = l_idx >= s_idx
    strict_cat = l_idx > s_idx
    eye_cat = (l_idx == s_idx).astype(F32)
    blk_row = lax.shift_right_logical(lax.broadcasted_iota(jnp.int32, (1, rr), 1), sh)

    def to_cat(x):
        if x.shape[1] != rr:
            x = jnp.concatenate([x] * (rr // x.shape[1]), axis=1)
        out = x[(nch - 1) * cs:]
        for c in reversed(range(nch - 1)):
            out = jnp.where(blk == c, x[c * cs:(c + 1) * cs], out)
        return out

    def to_bd(x_cat):
        return jnp.where(same_blk, jnp.concatenate([x_cat] * nch, axis=0), jnp.zeros((), x_cat.dtype))

    heads = range(GDN_HEADS)
    qn, kn_b, v_b, kn_t, attn_cat, p_cat, t_cat, grow, eg_full = ([None] * GDN_HEADS for _ in range(9))
    for h in heads:
        qh = act_ref[:, h * d:(h + 1) * d].astype(F32)
        kh = act_ref[:, GDN_WIDTH + h * d:GDN_WIDTH + (h + 1) * d].astype(F32)
        qn[h] = (qh * lax.rsqrt(jnp.sum(qh * qh, axis=-1, keepdims=True) + L2_EPS) * (d ** -0.5)).astype(BF16)
        kn = kh * lax.rsqrt(jnp.sum(kh * kh, axis=-1, keepdims=True) + L2_EPS)
        kn_b[h] = kn.astype(BF16)
        kn_t[h] = kn.T
        v_b[h] = act_ref[:, 2 * GDN_WIDTH + h * d:2 * GDN_WIDTH + (h + 1) * d]
        gcol_full = jnp.broadcast_to(g_cs[:, S_DECAY + h:S_DECAY + h + 1], (rr, LANES))
        bcol_full = jnp.broadcast_to(beta[:, S_BETA + h:S_BETA + h + 1], (rr, LANES))
        eg_full[h] = jnp.exp(gcol_full)
        grow[h] = g_t[S_DECAY + h:S_DECAY + h + 1, :]
        dec = jnp.exp(jnp.where(incl_cat, to_cat(gcol_full) - grow[h], 0.0))
        kk = to_cat(_dot_nt(kn_b[h], kn_b[h]))
        qk = to_cat(_dot_nt(qn[h], kn_b[h]))
        attn_cat[h] = jnp.where(incl_cat, qk * dec, 0.0)
        p_cat[h] = jnp.where(strict_cat, -(kk * dec * to_cat(bcol_full)), 0.0)
        t_cat[h] = eye_cat + p_cat[h]

    p_bd = [to_bd(p_cat[h].astype(BF16)) for h in heads]
    for _ in range(sh - 1):
        for h in heads:
            p_cat[h] = _dot(p_cat[h].astype(BF16), p_bd[h])
        for h in heads:
            p_bd[h] = to_bd(p_cat[h].astype(BF16))
        for h in heads:
            t_cat[h] = t_cat[h] + _dot(t_cat[h].astype(BF16), p_bd[h])

    u, w = [None] * GDN_HEADS, [None] * GDN_HEADS
    for h in heads:
        brow = beta_t[S_BETA + h:S_BETA + h + 1, :]
        u[h] = _dot(to_bd((t_cat[h] * brow).astype(BF16)), v_b[h])
        w[h] = _dot(to_bd((t_cat[h] * (brow * jnp.exp(grow[h]))).astype(BF16)), kn_b[h])

    st = [state_ref[h] for h in heads]
    zero_blk = jnp.zeros((cs, d), BF16)
    for c in range(nch):
        rows = slice(c * cs, (c + 1) * cs)
        for h in heads:
            ws = _dot(jnp.concatenate([w[h][rows].astype(BF16), qn[h][rows]], axis=0), st[h].astype(BF16))
            v_new = (u[h][rows] - ws[:cs]).astype(BF16)
            rhs = jnp.concatenate([zero_blk] * c + [v_new] + [zero_blk] * (nch - 1 - c), axis=0)
            g_last = grow[h][:, (c + 1) * cs - 1:(c + 1) * cs]
            e_row = jnp.exp(jnp.where(blk_row == c, g_last - grow[h], 0.0))
            lhs = jnp.concatenate(
                [jnp.where(blk == c, attn_cat[h], 0.0),
                 jnp.where(blk_row == c, kn_t[h] * e_row, 0.0)], axis=0).astype(BF16)
            r = _dot(lhs, rhs)
            o = eg_full[h][rows] * ws[cs:] + r[:cs]
            st[h] = st[h] * jnp.exp(g_last) + r[cs:]
            ms = jnp.mean(o * o, axis=-1, keepdims=True)
            zh = z_ref[rows, h * d:(h + 1) * d].astype(F32)
            o_ref[rows, h * d:(h + 1) * d] = (
                o * lax.rsqrt(ms + RMS_EPS) * nw_ref[...] * _silu(zh)).astype(o_ref.dtype)
    for h in heads:
        state_ref[h] = st[h]


def gated_deltanet(p3, small3, dt_bias, a_log, norm_w):
    bsz, seq, _ = p3.shape
    rr = GDN_ROWS
    w3 = 3 * GDN_WIDTH
    const = lambda b, c: (0, 0)
    return pl.pallas_call(
        _gdn_kernel,
        grid=(bsz, seq // rr),
        in_specs=[pl.BlockSpec((None, rr, w3), lambda b, c: (b, c, P_GDN_QKV // w3)),
                  pl.BlockSpec((None, rr, GDN_WIDTH), lambda b, c: (b, c, P_GDN_Z // GDN_WIDTH)),
                  pl.BlockSpec((None, rr, LANES), lambda b, c: (b, c, 0)),
                  pl.BlockSpec((1, LANES), const),
                  pl.BlockSpec((1, LANES), const),
                  pl.BlockSpec((1, GDN_HEAD_DIM), const)],
        out_specs=pl.BlockSpec((None, rr, GDN_WIDTH), lambda b, c: (b, c, 0)),
        out_shape=jax.ShapeDtypeStruct((bsz, seq, GDN_WIDTH), BF16),
        scratch_shapes=[pltpu.VMEM((GDN_HEADS, GDN_HEAD_DIM, GDN_HEAD_DIM), F32)],
        compiler_params=pltpu.CompilerParams(
            dimension_semantics=("parallel", "arbitrary"), vmem_limit_bytes=VMEM_LIMIT),
        name="gated_deltanet",
    )(p3, p3, small3, _pad_row(dt_bias, S_DECAY), _pad_row(a_log, S_DECAY),
      norm_w.reshape(1, GDN_HEAD_DIM))


def _merge_kernel(oa_ref, os_ref, og_ref, wb_ref, g0_ref, g1_ref, g2_ref, o_ref):
    acc = None
    for br, (x_ref, g_ref) in enumerate(((oa_ref, g0_ref), (os_ref, g1_ref), (og_ref, g2_ref))):
        gate = 1.0 / (1.0 + jnp.exp(-g_ref[...].astype(F32)))
        term = gate * _dot(x_ref[...], wb_ref[br].astype(BF16))
        acc = term if acc is None else acc + term
    o_ref[...] = acc.astype(o_ref.dtype)


def merge_branches(o_da, o_ssm, o_gdn, wb3, p, *, tm=1024, tn=512):
    m = o_da.shape[0]
    gb = P_GATES // tn
    nb = D_MODEL // tn
    act = pl.BlockSpec((tm, DA_WIDTH), lambda i, j: (i, 0))
    return pl.pallas_call(
        _merge_kernel,
        grid=(m // tm, nb),
        in_specs=[act, act, act,
                  pl.BlockSpec((N_BRANCH, DA_WIDTH, tn), lambda i, j: (0, 0, j)),
                  pl.BlockSpec((tm, tn), lambda i, j: (i, gb + j)),
                  pl.BlockSpec((tm, tn), lambda i, j: (i, gb + nb + j)),
                  pl.BlockSpec((tm, tn), lambda i, j: (i, gb + 2 * nb + j))],
        out_specs=pl.BlockSpec((tm, tn), lambda i, j: (i, j)),
        out_shape=jax.ShapeDtypeStruct((m, D_MODEL), BF16),
        compiler_params=pltpu.CompilerParams(
            dimension_semantics=("parallel", "arbitrary"), vmem_limit_bytes=VMEM_LIMIT),
        name="merge_branches",
    )(o_da, o_ssm, o_gdn, wb3, p, p, p)


def _ffn_up_kernel(x_ref, wg_ref, wv_ref, cwg_ref, cwv_ref, cbg_ref, cbv_ref, o_ref, ug_ref, uv_ref,
                   wgb_ref, wvb_ref, *, rows):
    seq = x_ref.shape[0]
    pad = HALO
    ug_ref[0:pad, :] = jnp.zeros((pad, ug_ref.shape[1]), F32)
    uv_ref[0:pad, :] = jnp.zeros((pad, uv_ref.shape[1]), F32)
    wgb_ref[...] = wg_ref[...].astype(BF16)
    wvb_ref[...] = wv_ref[...].astype(BF16)

    def project(r):
        xr = x_ref[r:r + rows, :]
        ug_ref[pad + r:pad + r + rows, :] = _dot(xr, wgb_ref[...])
        uv_ref[pad + r:pad + r + rows, :] = _dot(xr, wvb_ref[...])

    def conv(u_ref, cw_ref, cb_ref, r):
        acc = cb_ref[...] + cw_ref[0:1, :] * u_ref[pad + r - 2:pad + r - 2 + rows, :]
        for t in range(1, FFN_CONV):
            acc = acc + cw_ref[t:t + 1, :] * u_ref[pad + r - 2 + t:pad + r - 2 + t + rows, :]
        return acc

    def activate(r):
        gate = conv(ug_ref, cwg_ref, cbg_ref, r)
        val = conv(uv_ref, cwv_ref, cbv_ref, r)
        o_ref[r:r + rows, :] = (_silu(gate) * val).astype(o_ref.dtype)

    project(0)
    for r in range(rows, seq, rows):
        activate(r - rows)
        project(r)
    activate(seq - rows)


def ffn_up_proj(hn3, w_up, conv_w, conv_b, *, tf=512, rows=256):
    bsz, seq, d = hn3.shape
    nf = D_FF // tf
    cb = conv_b.reshape(1, 2 * D_FF)
    return pl.pallas_call(
        functools.partial(_ffn_up_kernel, rows=rows),
        grid=(bsz, nf),
        in_specs=[pl.BlockSpec((None, seq, d), lambda b, f: (b, 0, 0)),
                  pl.BlockSpec((d, tf), lambda b, f: (0, f)),
                  pl.BlockSpec((d, tf), lambda b, f: (0, nf + f)),
                  pl.BlockSpec((FFN_CONV, tf), lambda b, f: (0, f)),
                  pl.BlockSpec((FFN_CONV, tf), lambda b, f: (0, nf + f)),
                  pl.BlockSpec((1, tf), lambda b, f: (0, f)),
                  pl.BlockSpec((1, tf), lambda b, f: (0, nf + f))],
        out_specs=pl.BlockSpec((None, seq, tf), lambda b, f: (b, 0, f)),
        out_shape=jax.ShapeDtypeStruct((bsz, seq, D_FF), BF16),
        scratch_shapes=[pltpu.VMEM((HALO + seq, tf), F32), pltpu.VMEM((HALO + seq, tf), F32),
                        pltpu.VMEM((d, tf), BF16), pltpu.VMEM((d, tf), BF16)],
        compiler_params=pltpu.CompilerParams(
            dimension_semantics=("parallel", "arbitrary"), vmem_limit_bytes=VMEM_LIMIT),
        name="ffn_up",
    )(hn3, w_up, w_up, conv_w, conv_w, cb, cb)


def _reorder_w_in(w):
    sizes = (DA_WIDTH, DA_WIDTH, DA_WIDTH, SSM_INNER, SSM_INNER + SSM_BC, SSM_HEADS,
             3 * GDN_WIDTH, GDN_WIDTH, GDN_HEADS, GDN_HEADS, N_BRANCH * D_MODEL)
    offs = [0]
    for s in sizes:
        offs.append(offs[-1] + s)
    seg = lambda i: w[:, offs[i]:offs[i + 1]]
    da_q, da_k, da_v, ssm_z, ssm_xbc, ssm_dt, gdn_qkv, gdn_z, gdn_b, gdn_a, gates = (
        seg(i) for i in range(len(sizes)))
    big = jnp.concatenate([gdn_qkv, ssm_z, gdn_z, da_q, da_k, da_v, gates, ssm_xbc], axis=1)
    small = jnp.concatenate(
        [ssm_dt, gdn_b, gdn_a,
         jnp.zeros((w.shape[0], LANES - SSM_HEADS - 2 * GDN_HEADS), w.dtype)], axis=1)
    return big.astype(BF16), small.astype(BF16)


def kernel(x, norm_mix, w_in, da_lambda, da_subln, ssm_conv_w, ssm_conv_b, ssm_dt_bias, ssm_a_log,
           ssm_d, ssm_norm, gdn_conv_w, gdn_dt_bias, gdn_a_log, gdn_norm, w_branch, w_out, norm_ffn,
           ffn_up, ffn_conv_w, ffn_conv_b, ffn_down, norm_final):
    bsz, seq, d = x.shape
    m = bsz * seq
    h = x.reshape(m, d)
    mid = P_SSM_X - 3 * GDN_WIDTH
    for l in range(DEPTH):
        w_big, w_small = _reorder_w_in(w_in[l])
        conv_w = jnp.concatenate([gdn_conv_w[l], jnp.zeros((SSM_CONV, mid), F32), ssm_conv_w[l]], axis=1)
        conv_b = jnp.concatenate([jnp.zeros((P_SSM_X,), F32), ssm_conv_b[l]]).reshape(1, P_COLS)
        p, small = in_proj(h, norm_mix[l], w_big, w_small, conv_w, conv_b, seq=seq)
        p3 = p.reshape(bsz, seq, P_COLS)
        small3 = small.reshape(bsz, seq, LANES)
        lambda_init = 0.8 - 0.6 * math.exp(-0.3 * l)
        o_da = diff_attention(p3, da_lambda[l], da_subln[l], lambda_init)
        o_ssm = mamba2_ssd(p3, small3, ssm_dt_bias[l], ssm_a_log[l], ssm_d[l], ssm_norm[l])
        o_gdn = gated_deltanet(p3, small3, gdn_dt_bias[l], gdn_a_log[l], gdn_norm[l])
        wb3 = w_branch[l].reshape(N_BRANCH, DA_WIDTH, D_MODEL)
        merged = merge_branches(o_da.reshape(m, DA_WIDTH), o_ssm.reshape(m, SSM_INNER),
                                o_gdn.reshape(m, GDN_WIDTH), wb3, p)
        h, hn = matmul_res_norm(merged, w_out[l].astype(BF16), h, norm_ffn[l], tm=512, name="out_proj")
        act = ffn_up_proj(hn.reshape(bsz, seq, d), ffn_up[l], ffn_conv_w[l], ffn_conv_b[l])
        h = matmul_res(act.reshape(m, D_FF), ffn_down[l].astype(BF16), h, tm=512, tn=1024, name="ffn_down")
    return rmsnorm(h, norm_final, out_dtype=F32).reshape(bsz, seq, d)
```

```python
import functools
import math

import jax
import jax.numpy as jnp
from jax import lax
from jax.experimental import pallas as pl
from jax.experimental.pallas import tpu as pltpu

F32 = jnp.float32
BF16 = jnp.bfloat16

D_MODEL = 2048
DEPTH = 2
DA_HEADS = 8
DA_HEAD_DIM = 64
DA_WIDTH = DA_HEADS * 2 * DA_HEAD_DIM
SSM_HEADS = 16
SSM_HEAD_DIM = 64
SSM_INNER = SSM_HEADS * SSM_HEAD_DIM
SSM_GROUPS = 2
SSM_STATE = 128
SSM_CONV = 4
SSM_CHUNK = 128
SSM_BC = 2 * SSM_GROUPS * SSM_STATE
GDN_HEADS = 8
GDN_HEAD_DIM = 128
GDN_WIDTH = GDN_HEADS * GDN_HEAD_DIM
GDN_CONV = 4
GDN_CHUNK = 64
GDN_ROWS = 256
D_FF = 5632
FFN_CONV = 3
N_BRANCH = 3
RMS_EPS = 1e-6
L2_EPS = 1e-6
LOG2E = math.log2(math.e)

LANES = 128
HALO = 8
VMEM_LIMIT = 56 * 1024 * 1024

P_GDN_QKV = 0
P_SSM_Z = P_GDN_QKV + 3 * GDN_WIDTH
P_GDN_Z = P_SSM_Z + SSM_INNER
P_DA_Q = P_GDN_Z + GDN_WIDTH
P_DA_K = P_DA_Q + DA_WIDTH
P_DA_V = P_DA_K + DA_WIDTH
P_GATES = P_DA_V + DA_WIDTH
P_SSM_X = P_GATES + N_BRANCH * D_MODEL
P_SSM_BC = P_SSM_X + SSM_INNER
P_COLS = P_SSM_BC + SSM_BC
S_DT = 0
S_BETA = SSM_HEADS
S_DECAY = SSM_HEADS + GDN_HEADS


def _silu(x):
    return x / (1.0 + jnp.exp(-x))


def _softplus(x):
    return jnp.maximum(x, 0.0) + jnp.log1p(jnp.exp(-jnp.abs(x)))


def _dot(a, b):
    return jnp.dot(a, b, preferred_element_type=F32)


def _dot_nt(a, b):
    return lax.dot_general(a, b, (((1,), (1,)), ((), ())), preferred_element_type=F32)


def _dot_f32(a, b):
    return jnp.dot(a, b, preferred_element_type=F32, precision=lax.Precision.HIGHEST)


def _rmsnorm_kernel(x_ref, w_ref, o_ref):
    x = x_ref[...]
    ms = jnp.mean(x * x, axis=-1, keepdims=True)
    o_ref[...] = (x * lax.rsqrt(ms + RMS_EPS) * w_ref[...]).astype(o_ref.dtype)


def rmsnorm(x, w, *, tm=512, out_dtype=BF16):
    m, d = x.shape
    return pl.pallas_call(
        _rmsnorm_kernel,
        grid=(m // tm,),
        in_specs=[pl.BlockSpec((tm, d), lambda i: (i, 0)),
                  pl.BlockSpec((1, d), lambda i: (0, 0))],
        out_specs=pl.BlockSpec((tm, d), lambda i: (i, 0)),
        out_shape=jax.ShapeDtypeStruct((m, d), out_dtype),
        name="rmsnorm",
    )(x, w.reshape(1, d))


IN_TM = 1024
IN_TN = 512
IN_ROWS = 128


def _in_proj_kernel(x_ref, nw_ref, w_ref, ws_ref, cw_ref, cb_ref, o_ref, small_ref, xn_ref, u_ref, halo_ref,
                    *, conv_front, conv_back, tiles_per_seq):
    i = pl.program_id(0)
    j = pl.program_id(1)
    tm = x_ref.shape[0]

    @pl.when(j == 0)
    def _():
        x = x_ref[...]
        ms = jnp.mean(x * x, axis=-1, keepdims=True)
        xn_ref[...] = (x * lax.rsqrt(ms + RMS_EPS) * nw_ref[...]).astype(xn_ref.dtype)
        small_ref[...] = _dot(xn_ref[...], ws_ref[...])

    is_conv = (j < conv_front) | (j >= conv_back)

    @pl.when(jnp.logical_not(is_conv))
    def _():
        o_ref[...] = _dot(xn_ref[...], w_ref[...]).astype(o_ref.dtype)

    @pl.when(is_conv)
    def _():
        first = lax.rem(i, tiles_per_seq) == 0

        @pl.when(first)
        def _():
            u_ref[0:HALO, :] = jnp.zeros((HALO, u_ref.shape[1]), F32)

        @pl.when(jnp.logical_not(first))
        def _():
            u_ref[0:HALO, :] = halo_ref[j]

        tail = u_ref[0:HALO, :]
        for r in range(0, tm, IN_ROWS):
            cur = _dot(xn_ref[r:r + IN_ROWS, :], w_ref[...])
            ext = jnp.concatenate([tail, cur], axis=0)
            acc = cb_ref[...] + cw_ref[SSM_CONV - 1:SSM_CONV, :] * cur
            for t in range(SSM_CONV - 1):
                acc = acc + cw_ref[t:t + 1, :] * ext[HALO - 3 + t:HALO - 3 + t + IN_ROWS, :]
            o_ref[r:r + IN_ROWS, :] = _silu(acc).astype(o_ref.dtype)
            tail = cur[IN_ROWS - HALO:, :]
        halo_ref[j] = tail


def in_proj(x, nw, w, w_small, conv_w, conv_b, *, seq):
    m, k = x.shape
    assert GDN_CONV == SSM_CONV and P_GDN_QKV == 0 and P_SSM_X % IN_TN == 0 and (3 * GDN_WIDTH) % IN_TN == 0
    n_tiles = P_COLS // IN_TN
    return pl.pallas_call(
        functools.partial(_in_proj_kernel, conv_front=3 * GDN_WIDTH // IN_TN, conv_back=P_SSM_X // IN_TN,
                          tiles_per_seq=seq // IN_TM),
        grid=(m // IN_TM, n_tiles),
        in_specs=[pl.BlockSpec((IN_TM, k), lambda i, j: (i, 0)),
                  pl.BlockSpec((1, k), lambda i, j: (0, 0)),
                  pl.BlockSpec((k, IN_TN), lambda i, j: (0, j)),
                  pl.BlockSpec((k, LANES), lambda i, j: (0, 0)),
                  pl.BlockSpec((SSM_CONV, IN_TN), lambda i, j: (0, j)),
                  pl.BlockSpec((1, IN_TN), lambda i, j: (0, j))],
        out_specs=[pl.BlockSpec((IN_TM, IN_TN), lambda i, j: (i, j)),
                   pl.BlockSpec((IN_TM, LANES), lambda i, j: (i, 0))],
        out_shape=[jax.ShapeDtypeStruct((m, P_COLS), BF16), jax.ShapeDtypeStruct((m, LANES), F32)],
        scratch_shapes=[pltpu.VMEM((IN_TM, k), BF16),
                        pltpu.VMEM((HALO + IN_TM, IN_TN), F32),
                        pltpu.VMEM((n_tiles, HALO, IN_TN), F32)],
        compiler_params=pltpu.CompilerParams(
            dimension_semantics=("arbitrary", "arbitrary"), vmem_limit_bytes=VMEM_LIMIT),
        name="in_proj",
    )(x, nw.reshape(1, k), w, w_small, conv_w, conv_b)


def _matmul_res_norm_kernel(a_ref, w_ref, h_ref, nw_ref, ho_ref, no_ref):
    h = h_ref[...] + _dot(a_ref[...], w_ref[...])
    ho_ref[...] = h
    ms = jnp.mean(h * h, axis=-1, keepdims=True)
    no_ref[...] = (h * lax.rsqrt(ms + RMS_EPS) * nw_ref[...]).astype(no_ref.dtype)


def matmul_res_norm(a, w, h, nw, *, tm, name):
    m, k = a.shape
    _, n = w.shape
    row = lambda i: (i, 0)
    return pl.pallas_call(
        _matmul_res_norm_kernel,
        grid=(m // tm,),
        in_specs=[pl.BlockSpec((tm, k), row),
                  pl.BlockSpec((k, n), lambda i: (0, 0)),
                  pl.BlockSpec((tm, n), row),
                  pl.BlockSpec((1, n), lambda i: (0, 0))],
        out_specs=[pl.BlockSpec((tm, n), row), pl.BlockSpec((tm, n), row)],
        out_shape=[jax.ShapeDtypeStruct((m, n), F32), jax.ShapeDtypeStruct((m, n), BF16)],
        compiler_params=pltpu.CompilerParams(
            dimension_semantics=("parallel",), vmem_limit_bytes=VMEM_LIMIT),
        name=name,
    )(a, w, h, nw.reshape(1, n))


def _matmul_res_kernel(a_ref, w_ref, h_ref, o_ref):
    o_ref[...] = h_ref[...] + _dot(a_ref[...], w_ref[...])


def matmul_res(a, w, h, *, tm, tn, name):
    m, k = a.shape
    _, n = w.shape
    return pl.pallas_call(
        _matmul_res_kernel,
        grid=(n // tn, m // tm),
        in_specs=[pl.BlockSpec((tm, k), lambda j, i: (i, 0)),
                  pl.BlockSpec((k, tn), lambda j, i: (0, j)),
                  pl.BlockSpec((tm, tn), lambda j, i: (i, j))],
        out_specs=pl.BlockSpec((tm, tn), lambda j, i: (i, j)),
        out_shape=jax.ShapeDtypeStruct((m, n), F32),
        compiler_params=pltpu.CompilerParams(
            dimension_semantics=("parallel", "parallel"), vmem_limit_bytes=VMEM_LIMIT),
        name=name,
    )(a, w, h)


DA_BIAS_LANES = 3


def _diff_attn_kernel(slopes_ref, lamp_ref, subln_ref, q_ref, k_ref, v_ref, o_ref,
                      k1_ref, k2_ref, s_ref, st_ref, p_ref, *, tq, lambda_init):
    hd = DA_HEAD_DIM
    seq = q_ref.shape[0]
    nq = seq // tq
    half = tq // 2
    slope = slopes_ref[pl.program_id(1)]
    lp = lamp_ref[...]
    lam = (jnp.exp(jnp.sum(lp[0:1] * lp[1:2], axis=-1, keepdims=True))
           - jnp.exp(jnp.sum(lp[2:3] * lp[3:4], axis=-1, keepdims=True)) + lambda_init)

    lane_k = lax.broadcasted_iota(jnp.int32, (seq, 2 * hd), 1)
    bias = (slope * LOG2E) * lax.broadcasted_iota(jnp.int32, (seq, 2 * hd), 0).astype(F32)
    pieces = []
    rest = bias
    for _ in range(DA_BIAS_LANES):
        piece = rest.astype(BF16).astype(F32)
        pieces.append(piece)
        rest = rest - piece
    kf = k_ref[...].astype(F32)
    k1 = jnp.where(lane_k < hd, kf, 0.0)
    k2 = jnp.where(lane_k >= hd, kf, 0.0)
    for t, piece in enumerate(pieces):
        k1 = jnp.where(lane_k == hd + t, piece, k1)
        k2 = jnp.where(lane_k == t, piece, k2)
    k1_ref[...] = k1.astype(BF16)
    k2_ref[...] = k2.astype(BF16)

    lane_q = lax.broadcasted_iota(jnp.int32, (tq, 2 * hd), 1)
    causal = (lax.broadcasted_iota(jnp.int32, (tq, tq), 0)
              >= lax.broadcasted_iota(jnp.int32, (tq, tq), 1))

    def fold(x):
        return x[:, :half], x[:, half:]

    for qi in range(nq):
        rows = slice(qi * tq, (qi + 1) * tq)
        qf = q_ref[rows, :].astype(F32) * (hd ** -0.5 * LOG2E)
        q1 = jnp.where(lane_q < hd, qf, jnp.where(lane_q < hd + DA_BIAS_LANES, 1.0, 0.0)).astype(BF16)
        q2 = jnp.where(lane_q >= hd, qf, jnp.where(lane_q < DA_BIAS_LANES, 1.0, 0.0)).astype(BF16)

        mx = [None, None]
        for j in range(qi + 1):
            keys = slice(j * tq, (j + 1) * tq)
            for mp, (qm, km_ref) in enumerate(((q1, k1_ref), (q2, k2_ref))):
                s = _dot_nt(qm, km_ref[keys, :])
                if j == qi:
                    s = jnp.where(causal, s, -jnp.inf)
                s_ref[mp, j] = s
                a, b = fold(s)
                ab = jnp.maximum(a, b)
                mx[mp] = ab if mx[mp] is None else jnp.maximum(mx[mp], ab)
        for mp in range(2):
            st_ref[mp] = jnp.broadcast_to(jnp.max(mx[mp], axis=-1, keepdims=True), (tq, half))

        sm = [None, None]
        for j in range(qi + 1):
            for mp in range(2):
                m = st_ref[mp]
                a, b = fold(s_ref[mp, j])
                pa = jnp.exp2(a - m)
                pb = jnp.exp2(b - m)
                p_ref[mp, :, j * tq:(j + 1) * tq] = jnp.concatenate([pa, pb], axis=1).astype(BF16)
                sm[mp] = pa + pb if sm[mp] is None else sm[mp] + (pa + pb)
        l1 = jnp.sum(sm[0], axis=-1, keepdims=True)
        l2 = jnp.sum(sm[1], axis=-1, keepdims=True)

        kv = (qi + 1) * tq
        o = (_dot(p_ref[0, :, :kv], v_ref[:kv, :]) * (1.0 / l1)
             - _dot(p_ref[1, :, :kv], v_ref[:kv, :]) * (lam / l2))
        ms = jnp.mean(o * o, axis=-1, keepdims=True)
        o = o * lax.rsqrt(ms + RMS_EPS) * subln_ref[...] * (1.0 - lambda_init)
        o_ref[rows, :] = o.astype(o_ref.dtype)


def diff_attention(p3, lam_params, subln_w, lambda_init, *, tq=256):
    bsz, seq, _ = p3.shape
    hw = 2 * DA_HEAD_DIM
    nq = seq // tq
    slopes = jnp.asarray([2.0 ** (-8.0 * (h + 1) / DA_HEADS) for h in range(DA_HEADS)], F32)
    qb, kb, vb = P_DA_Q // hw, P_DA_K // hw, P_DA_V // hw
    return pl.pallas_call(
        functools.partial(_diff_attn_kernel, tq=tq, lambda_init=lambda_init),
        grid=(bsz, DA_HEADS),
        in_specs=[pl.BlockSpec(memory_space=pltpu.SMEM),
                  pl.BlockSpec((4, DA_HEAD_DIM), lambda b, h: (0, 0)),
                  pl.BlockSpec((1, hw), lambda b, h: (0, 0)),
                  pl.BlockSpec((None, seq, hw), lambda b, h: (b, 0, qb + h)),
                  pl.BlockSpec((None, seq, hw), lambda b, h: (b, 0, kb + h)),
                  pl.BlockSpec((None, seq, hw), lambda b, h: (b, 0, vb + h))],
        out_specs=pl.BlockSpec((None, seq, hw), lambda b, h: (b, 0, h)),
        out_shape=jax.ShapeDtypeStruct((bsz, seq, DA_WIDTH), BF16),
        scratch_shapes=[pltpu.VMEM((seq, hw), BF16),
                        pltpu.VMEM((seq, hw), BF16),
                        pltpu.VMEM((2, nq, tq, tq), F32),
                        pltpu.VMEM((2, tq, tq // 2), F32),
                        pltpu.VMEM((2, tq, seq), BF16)],
        compiler_params=pltpu.CompilerParams(
            dimension_semantics=("parallel", "parallel"), vmem_limit_bytes=VMEM_LIMIT),
        name="diff_attention",
    )(slopes, lam_params, subln_w.reshape(1, hw), p3, p3, p3)


def _ssd_kernel(x_ref, bc_ref, z_ref, sm_ref, dtb_ref, alog_ref, dskip_ref, nw_ref,
                o_ref, state_ref, y_ref):
    q = SSM_CHUNK
    c = pl.program_id(1)

    @pl.when(c == 0)
    def _():
        state_ref[...] = jnp.zeros_like(state_ref)

    xs_b = x_ref[...]
    xs = xs_b.astype(F32)
    bc = bc_ref[...]

    dt = _softplus(sm_ref[...] + dtb_ref[...])
    da = dt * (-jnp.exp(alog_ref[...]))
    row = lax.broadcasted_iota(jnp.int32, (q, q), 0)
    colm = lax.broadcasted_iota(jnp.int32, (q, q), 1)
    causal = row >= colm
    tri = causal.astype(F32)
    a_cs = _dot_f32(tri, da)
    a_cs_t = a_cs.T
    dt_t = dt.T
    lane = lax.broadcasted_iota(jnp.int32, (1, LANES), 1)

    for g in range(SSM_GROUPS):
        bm_b = bc[:, g * SSM_STATE:(g + 1) * SSM_STATE]
        cm_b = bc[:, (SSM_GROUPS + g) * SSM_STATE:(SSM_GROUPS + g + 1) * SSM_STATE]
        cm = cm_b.astype(F32)
        cb = _dot_nt(cm_b, bm_b)
        bm_t = bm_b.astype(F32).T
        hpg = SSM_HEADS // SSM_GROUPS
        for pr in range(hpg // 2):
            pair = g * (hpg // 2) + pr
            x_pair = xs_b[:, pair * LANES:(pair + 1) * LANES]
            st = state_ref[pair]
            rhs = jnp.concatenate([x_pair, st.astype(BF16)], axis=0)
            ys, sts, cds = [], [], []
            for sub in range(2):
                h = 2 * pair + sub
                acol = a_cs[:, h:h + 1]
                arow = a_cs_t[h:h + 1, :]
                dtrow = dt_t[h:h + 1, :]
                decay = jnp.exp(jnp.where(causal, acol - arow, -jnp.inf))
                sc = (cb * decay * dtrow).astype(BF16)
                c_in = (cm * jnp.exp(acol)).astype(BF16)
                ys.append(_dot(jnp.concatenate([sc, c_in], axis=1), rhs))
                a_last = arow[:, q - 1:q]
                wrow = jnp.exp(a_last - arow) * dtrow
                sts.append(_dot((bm_t * wrow).astype(BF16), x_pair))
                cds.append(jnp.exp(a_last))
            first = lane < SSM_HEAD_DIM
            y_ref[:, pair * LANES:(pair + 1) * LANES] = jnp.where(first, ys[0], ys[1])
            state_ref[pair] = (st * jnp.where(first, cds[0], cds[1])
                               + jnp.where(first, sts[0], sts[1]))

    y = y_ref[...] + xs * dskip_ref[...]
    y = y * _silu(z_ref[...].astype(F32))
    gw = SSM_INNER // SSM_GROUPS
    for g in range(SSM_GROUPS):
        yg = y[:, g * gw:(g + 1) * gw]
        ms = jnp.mean(yg * yg, axis=-1, keepdims=True)
        o_ref[:, g * gw:(g + 1) * gw] = (
            yg * lax.rsqrt(ms + RMS_EPS) * nw_ref[:, g * gw:(g + 1) * gw]).astype(o_ref.dtype)


def _pad_row(v, offset):
    return jnp.zeros((1, LANES), F32).at[0, offset:offset + v.shape[0]].set(v.astype(F32))


def mamba2_ssd(p3, small3, dt_bias, a_log, d_skip, norm_w):
    bsz, seq, _ = p3.shape
    q = SSM_CHUNK
    nc = seq // q
    const = lambda b, c: (0, 0)
    return pl.pallas_call(
        _ssd_kernel,
        grid=(bsz, nc),
        in_specs=[pl.BlockSpec((None, q, SSM_INNER), lambda b, c: (b, c, P_SSM_X // SSM_INNER)),
                  pl.BlockSpec((None, q, SSM_BC), lambda b, c: (b, c, P_SSM_BC // SSM_BC)),
                  pl.BlockSpec((None, q, SSM_INNER), lambda b, c: (b, c, P_SSM_Z // SSM_INNER)),
                  pl.BlockSpec((None, q, LANES), lambda b, c: (b, c, 0)),
                  pl.BlockSpec((1, LANES), const),
                  pl.BlockSpec((1, LANES), const),
                  pl.BlockSpec((1, SSM_INNER), const),
                  pl.BlockSpec((1, SSM_INNER), const)],
        out_specs=pl.BlockSpec((None, q, SSM_INNER), lambda b, c: (b, c, 0)),
        out_shape=jax.ShapeDtypeStruct((bsz, seq, SSM_INNER), BF16),
        scratch_shapes=[pltpu.VMEM((SSM_HEADS // 2, SSM_STATE, LANES), F32),
                        pltpu.VMEM((q, SSM_INNER), F32)],
        compiler_params=pltpu.CompilerParams(
            dimension_semantics=("parallel", "arbitrary"), vmem_limit_bytes=VMEM_LIMIT),
        name="mamba2_ssd",
    )(p3, p3, p3, small3, _pad_row(dt_bias, S_DT),
      _pad_row(a_log, S_DT), jnp.repeat(d_skip.astype(F32), SSM_HEAD_DIM).reshape(1, SSM_INNER),
      norm_w.reshape(1, SSM_INNER))


def _gdn_kernel(act_ref, z_ref, sm_ref, dtb_ref, alog_ref, nw_ref, o_ref, state_ref):
    cs = GDN_CHUNK
    rr = GDN_ROWS
    nch = rr // cs
    d = GDN_HEAD_DIM
    sh = int(math.log2(cs))
    step = pl.program_id(1)

    @pl.when(step == 0)
    def _():
        state_ref[...] = jnp.zeros_like(state_ref)

    sm = sm_ref[...]
    beta = 1.0 / (1.0 + jnp.exp(-sm))
    gl = -jnp.exp(alog_ref[...]) * _softplus(sm + dtb_ref[...])
    row = lax.broadcasted_iota(jnp.int32, (rr, rr), 0)
    colm = lax.broadcasted_iota(jnp.int32, (rr, rr), 1)
    same_blk = lax.shift_right_logical(row, sh) == lax.shift_right_logical(colm, sh)
    g_cs = _dot_f32((same_blk & (row >= colm)).astype(F32), gl)
    g_t = g_cs.T
    beta_t = beta.T

    l_idx = lax.broadcasted_iota(jnp.int32, (cs, rr), 0)
    j_idx = lax.broadcasted_iota(jnp.int32, (cs, rr), 1)
    s_idx = jnp.bitwise_and(j_idx, cs - 1)
    blk = lax.shift_right_logical(j_idx, sh)
    incl_cat = l_idx >= s_idx
    strict_cat = l_idx > s_idx
    eye_cat = (l_idx == s_idx).astype(F32)
    blk_row = lax.shift_right_logical(lax.broadcasted_iota(jnp.int32, (1, rr), 1), sh)

    def to_cat(x):
        if x.shape[1] != rr:
            x = jnp.concatenate([x] * (rr // x.shape[1]), axis=1)
        out = x[(nch - 1) * cs:]
        for c in reversed(range(nch - 1)):
            out = jnp.where(blk == c, x[c * cs:(c + 1) * cs], out)
        return out

    def to_bd(x_cat):
        return jnp.where(same_blk, jnp.concatenate([x_cat] * nch, axis=0), jnp.zeros((), x_cat.dtype))

    heads = range(GDN_HEADS)
    qn, kn_b, v_b, kn_t, attn_cat, p_cat, t_cat, grow, eg_full = ([None] * GDN_HEADS for _ in range(9))
    for h in heads:
        qh = act_ref[:, h * d:(h + 1) * d].astype(F32)
        kh = act_ref[:, GDN_WIDTH + h * d:GDN_WIDTH + (h + 1) * d].astype(F32)
        qn[h] = (qh * lax.rsqrt(jnp.sum(qh * qh, axis=-1, keepdims=True) + L2_EPS) * (d ** -0.5)).astype(BF16)
        kn = kh * lax.rsqrt(jnp.sum(kh * kh, axis=-1, keepdims=True) + L2_EPS)
        kn_b[h] = kn.astype(BF16)
        kn_t[h] = kn.T
        v_b[h] = act_ref[:, 2 * GDN_WIDTH + h * d:2 * GDN_WIDTH + (h + 1) * d]
        gcol_full = jnp.broadcast_to(g_cs[:, S_DECAY + h:S_DECAY + h + 1], (rr, LANES))
        bcol_full = jnp.broadcast_to(beta[:, S_BETA + h:S_BETA + h + 1], (rr, LANES))
        eg_full[h] = jnp.exp(gcol_full)
        grow[h] = g_t[S_DECAY + h:S_DECAY + h + 1, :]
        dec = jnp.exp(jnp.where(incl_cat, to_cat(gcol_full) - grow[h], 0.0))
        kk = to_cat(_dot_nt(kn_b[h], kn_b[h]))
        qk = to_cat(_dot_nt(qn[h], kn_b[h]))
        attn_cat[h] = jnp.where(incl_cat, qk * dec, 0.0)
        p_cat[h] = jnp.where(strict_cat, -(kk * dec * to_cat(bcol_full)), 0.0)
        t_cat[h] = eye_cat + p_cat[h]

    p_bd = [to_bd(p_cat[h].astype(BF16)) for h in heads]
    for _ in range(sh - 1):
        for h in heads:
            p_cat[h] = _dot(p_cat[h].astype(BF16), p_bd[h])
        for h in heads:
            p_bd[h] = to_bd(p_cat[h].astype(BF16))
        for h in heads:
            t_cat[h] = t_cat[h] + _dot(t_cat[h].astype(BF16), p_bd[h])

    u, w = [None] * GDN_HEADS, [None] * GDN_HEADS
    for h in heads:
        brow = beta_t[S_BETA + h:S_BETA + h + 1, :]
        u[h] = _dot(to_bd((t_cat[h] * brow).astype(BF16)), v_b[h])
        w[h] = _dot(to_bd((t_cat[h] * (brow * jnp.exp(grow[h]))).astype(BF16)), kn_b[h])

    st = [state_ref[h] for h in heads]
    zero_blk = jnp.zeros((cs, d), BF16)
    for c in range(nch):
        rows = slice(c * cs, (c + 1) * cs)
        for h in heads:
            ws = _dot(jnp.concatenate([w[h][rows].astype(BF16), qn[h][rows]], axis=0), st[h].astype(BF16))
            v_new = (u[h][rows] - ws[:cs]).astype(BF16)
            rhs = jnp.concatenate([zero_blk] * c + [v_new] + [zero_blk] * (nch - 1 - c), axis=0)
            g_last = grow[h][:, (c + 1) * cs - 1:(c + 1) * cs]
            e_row = jnp.exp(jnp.where(blk_row == c, g_last - grow[h], 0.0))
            lhs = jnp.concatenate(
                [jnp.where(blk == c, attn_cat[h], 0.0),
                 jnp.where(blk_row == c, kn_t[h] * e_row, 0.0)], axis=0).astype(BF16)
            r = _dot(lhs, rhs)
            o = eg_full[h][rows] * ws[cs:] + r[:cs]
            st[h] = st[h] * jnp.exp(g_last) + r[cs:]
            ms = jnp.mean(o * o, axis=-1, keepdims=True)
            zh = z_ref[rows, h * d:(h + 1) * d].astype(F32)
            o_ref[rows, h * d:(h + 1) * d] = (
                o * lax.rsqrt(ms + RMS_EPS) * nw_ref[...] * _silu(zh)).astype(o_ref.dtype)
    for h in heads:
        state_ref[h] = st[h]


def gated_deltanet(p3, small3, dt_bias, a_log, norm_w):
    bsz, seq, _ = p3.shape
    rr = GDN_ROWS
    w3 = 3 * GDN_WIDTH
    const = lambda b, c: (0, 0)
    return pl.pallas_call(
        _gdn_kernel,
        grid=(bsz, seq // rr),
        in_specs=[pl.BlockSpec((None, rr, w3), lambda b, c: (b, c, P_GDN_QKV // w3)),
                  pl.BlockSpec((None, rr, GDN_WIDTH), lambda b, c: (b, c, P_GDN_Z // GDN_WIDTH)),
                  pl.BlockSpec((None, rr, LANES), lambda b, c: (b, c, 0)),
                  pl.BlockSpec((1, LANES), const),
                  pl.BlockSpec((1, LANES), const),
                  pl.BlockSpec((1, GDN_HEAD_DIM), const)],
        out_specs=pl.BlockSpec((None, rr, GDN_WIDTH), lambda b, c: (b, c, 0)),
        out_shape=jax.ShapeDtypeStruct((bsz, seq, GDN_WIDTH), BF16),
        scratch_shapes=[pltpu.VMEM((GDN_HEADS, GDN_HEAD_DIM, GDN_HEAD_DIM), F32)],
        compiler_params=pltpu.CompilerParams(
            dimension_semantics=("parallel", "arbitrary"), vmem_limit_bytes=VMEM_LIMIT),
        name="gated_deltanet",
    )(p3, p3, small3, _pad_row(dt_bias, S_DECAY), _pad_row(a_log, S_DECAY),
      norm_w.reshape(1, GDN_HEAD_DIM))


def _merge_kernel(oa_ref, os_ref, og_ref, wb_ref, g0_ref, g1_ref, g2_ref, o_ref):
    acc = None
    for br, (x_ref, g_ref) in enumerate(((oa_ref, g0_ref), (os_ref, g1_ref), (og_ref, g2_ref))):
        gate = 1.0 / (1.0 + jnp.exp(-g_ref[...].astype(F32)))
        term = gate * _dot(x_ref[...], wb_ref[br].astype(BF16))
        acc = term if acc is None else acc + term
    o_ref[...] = acc.astype(o_ref.dtype)


def merge_branches(o_da, o_ssm, o_gdn, wb3, p, *, tm=1024, tn=512):
    m = o_da.shape[0]
    gb = P_GATES // tn
    nb = D_MODEL // tn
    act = pl.BlockSpec((tm, DA_WIDTH), lambda i, j: (i, 0))
    return pl.pallas_call(
        _merge_kernel,
        grid=(m // tm, nb),
        in_specs=[act, act, act,
                  pl.BlockSpec((N_BRANCH, DA_WIDTH, tn), lambda i, j: (0, 0, j)),
                  pl.BlockSpec((tm, tn), lambda i, j: (i, gb + j)),
                  pl.BlockSpec((tm, tn), lambda i, j: (i, gb + nb + j)),
                  pl.BlockSpec((tm, tn), lambda i, j: (i, gb + 2 * nb + j))],
        out_specs=pl.BlockSpec((tm, tn), lambda i, j: (i, j)),
        out_shape=jax.ShapeDtypeStruct((m, D_MODEL), BF16),
        compiler_params=pltpu.CompilerParams(
            dimension_semantics=("parallel", "arbitrary"), vmem_limit_bytes=VMEM_LIMIT),
        name="merge_branches",
    )(o_da, o_ssm, o_gdn, wb3, p, p, p)


def _ffn_up_kernel(x_ref, wg_ref, wv_ref, cwg_ref, cwv_ref, cbg_ref, cbv_ref, o_ref,
                   wgb_ref, wvb_ref, *, rows):
    seq = x_ref.shape[0]
    wgb_ref[...] = wg_ref[...].astype(BF16)
    wvb_ref[...] = wv_ref[...].astype(BF16)

    def conv(cur, tail, cw_ref, cb_ref):
        ext = jnp.concatenate([tail, cur], axis=0)
        acc = cb_ref[...] + cw_ref[FFN_CONV - 1:FFN_CONV, :] * cur
        for t in range(FFN_CONV - 1):
            acc = acc + cw_ref[t:t + 1, :] * ext[HALO - 2 + t:HALO - 2 + t + rows, :]
        return acc

    tail_g = jnp.zeros((HALO, o_ref.shape[1]), F32)
    tail_v = tail_g
    for r in range(0, seq, rows):
        xr = x_ref[r:r + rows, :]
        ug = _dot(xr, wgb_ref[...])
        uv = _dot(xr, wvb_ref[...])
        gate = conv(ug, tail_g, cwg_ref, cbg_ref)
        val = conv(uv, tail_v, cwv_ref, cbv_ref)
        o_ref[r:r + rows, :] = (_silu(gate) * val).astype(o_ref.dtype)
        tail_g = ug[rows - HALO:, :]
        tail_v = uv[rows - HALO:, :]


def ffn_up_proj(hn3, w_up, conv_w, conv_b, *, tf=512, rows=128):
    bsz, seq, d = hn3.shape
    nf = D_FF // tf
    cb = conv_b.reshape(1, 2 * D_FF)
    return pl.pallas_call(
        functools.partial(_ffn_up_kernel, rows=rows),
        grid=(bsz, nf),
        in_specs=[pl.BlockSpec((None, seq, d), lambda b, f: (b, 0, 0)),
                  pl.BlockSpec((d, tf), lambda b, f: (0, f)),
                  pl.BlockSpec((d, tf), lambda b, f: (0, nf + f)),
                  pl.BlockSpec((FFN_CONV, tf), lambda b, f: (0, f)),
                  pl.BlockSpec((FFN_CONV, tf), lambda b, f: (0, nf + f)),
                  pl.BlockSpec((1, tf), lambda b, f: (0, f)),
                  pl.BlockSpec((1, tf), lambda b, f: (0, nf + f))],
        out_specs=pl.BlockSpec((None, seq, tf), lambda b, f: (b, 0, f)),
        out_shape=jax.ShapeDtypeStruct((bsz, seq, D_FF), BF16),
        scratch_shapes=[pltpu.VMEM((d, tf), BF16), pltpu.VMEM((d, tf), BF16)],
        compiler_params=pltpu.CompilerParams(
            dimension_semantics=("parallel", "arbitrary"), vmem_limit_bytes=VMEM_LIMIT),
        name="ffn_up",
    )(hn3, w_up, w_up, conv_w, conv_w, cb, cb)


def _reorder_w_in(w):
    sizes = (DA_WIDTH, DA_WIDTH, DA_WIDTH, SSM_INNER, SSM_INNER + SSM_BC, SSM_HEADS,
             3 * GDN_WIDTH, GDN_WIDTH, GDN_HEADS, GDN_HEADS, N_BRANCH * D_MODEL)
    offs = [0]
    for s in sizes:
        offs.append(offs[-1] + s)
    seg = lambda i: w[:, offs[i]:offs[i + 1]]
    da_q, da_k, da_v, ssm_z, ssm_xbc, ssm_dt, gdn_qkv, gdn_z, gdn_b, gdn_a, gates = (
        seg(i) for i in range(len(sizes)))
    big = jnp.concatenate([gdn_qkv, ssm_z, gdn_z, da_q, da_k, da_v, gates, ssm_xbc], axis=1)
    small = jnp.concatenate(
        [ssm_dt, gdn_b, gdn_a,
         jnp.zeros((w.shape[0], LANES - SSM_HEADS - 2 * GDN_HEADS), w.dtype)], axis=1)
    return big.astype(BF16), small.astype(BF16)


def kernel(x, norm_mix, w_in, da_lambda, da_subln, ssm_conv_w, ssm_conv_b, ssm_dt_bias, ssm_a_log,
           ssm_d, ssm_norm, gdn_conv_w, gdn_dt_bias, gdn_a_log, gdn_norm, w_branch, w_out, norm_ffn,
           ffn_up, ffn_conv_w, ffn_conv_b, ffn_down, norm_final):
    bsz, seq, d = x.shape
    m = bsz * seq
    h = x.reshape(m, d)
    mid = P_SSM_X - 3 * GDN_WIDTH
    for l in range(DEPTH):
        w_big, w_small = _reorder_w_in(w_in[l])
        conv_w = jnp.concatenate([gdn_conv_w[l], jnp.zeros((SSM_CONV, mid), F32), ssm_conv_w[l]], axis=1)
        conv_b = jnp.concatenate([jnp.zeros((P_SSM_X,), F32), ssm_conv_b[l]]).reshape(1, P_COLS)
        p, small = in_proj(h, norm_mix[l], w_big, w_small, conv_w, conv_b, seq=seq)
        p3 = p.reshape(bsz, seq, P_COLS)
        small3 = small.reshape(bsz, seq, LANES)
        lambda_init = 0.8 - 0.6 * math.exp(-0.3 * l)
        o_da = diff_attention(p3, da_lambda[l], da_subln[l], lambda_init)
        o_ssm = mamba2_ssd(p3, small3, ssm_dt_bias[l], ssm_a_log[l], ssm_d[l], ssm_norm[l])
        o_gdn = gated_deltanet(p3, small3, gdn_dt_bias[l], gdn_a_log[l], gdn_norm[l])
        wb3 = w_branch[l].reshape(N_BRANCH, DA_WIDTH, D_MODEL)
        merged = merge_branches(o_da.reshape(m, DA_WIDTH), o_ssm.reshape(m, SSM_INNER),
                                o_gdn.reshape(m, GDN_WIDTH), wb3, p)
        h, hn = matmul_res_norm(merged, w_out[l].astype(BF16), h, norm_ffn[l], tm=512, name="out_proj")
        act = ffn_up_proj(hn.reshape(bsz, seq, d), ffn_up[l], ffn_conv_w[l], ffn_conv_b[l])
        h = matmul_res(act.reshape(m, D_FF), ffn_down[l].astype(BF16), h, tm=512, tn=1024, name="ffn_down")
    return rmsnorm(h, norm_final, out_dtype=F32).reshape(bsz, seq, d)
```

```python
import functools
import math

import jax
import jax.numpy as jnp
from jax import lax
from jax.experimental import pallas as pl
from jax.experimental.pallas import tpu as pltpu

F32 = jnp.float32
BF16 = jnp.bfloat16

D_MODEL = 2048
DEPTH = 2
DA_HEADS = 8
DA_HEAD_DIM = 64
DA_WIDTH = DA_HEADS * 2 * DA_HEAD_DIM
SSM_HEADS = 16
SSM_HEAD_DIM = 64
SSM_INNER = SSM_HEADS * SSM_HEAD_DIM
SSM_GROUPS = 2
SSM_STATE = 128
SSM_CONV = 4
SSM_CHUNK = 128
SSM_BC = 2 * SSM_GROUPS * SSM_STATE
GDN_HEADS = 8
GDN_HEAD_DIM = 128
GDN_WIDTH = GDN_HEADS * GDN_HEAD_DIM
GDN_CONV = 4
GDN_CHUNK = 64
GDN_ROWS = 256
GDN_SEQS = 1
D_FF = 5632
FFN_CONV = 3
N_BRANCH = 3
RMS_EPS = 1e-6
L2_EPS = 1e-6
LOG2E = math.log2(math.e)

LANES = 128
HALO = 8
VMEM_LIMIT = 56 * 1024 * 1024

P_GDN_QKV = 0
P_SSM_Z = P_GDN_QKV + 3 * GDN_WIDTH
P_GDN_Z = P_SSM_Z + SSM_INNER
P_DA_Q = P_GDN_Z + GDN_WIDTH
P_DA_K = P_DA_Q + DA_WIDTH
P_DA_V = P_DA_K + DA_WIDTH
P_GATES = P_DA_V + DA_WIDTH
P_SSM_X = P_GATES + N_BRANCH * D_MODEL
P_SSM_BC = P_SSM_X + SSM_INNER
P_COLS = P_SSM_BC + SSM_BC
S_DT = 0
S_BETA = SSM_HEADS
S_DECAY = SSM_HEADS + GDN_HEADS


def _silu(x):
    return x / (1.0 + jnp.exp(-x))


def _softplus(x):
    return jnp.maximum(x, 0.0) + jnp.log1p(jnp.exp(-jnp.abs(x)))


def _dot(a, b):
    return jnp.dot(a, b, preferred_element_type=F32)


def _dot_nt(a, b):
    return lax.dot_general(a, b, (((1,), (1,)), ((), ())), preferred_element_type=F32)


def _dot_f32(a, b):
    return jnp.dot(a, b, preferred_element_type=F32, precision=lax.Precision.HIGHEST)


def _rmsnorm_kernel(x_ref, w_ref, o_ref):
    x = x_ref[...]
    ms = jnp.mean(x * x, axis=-1, keepdims=True)
    o_ref[...] = (x * lax.rsqrt(ms + RMS_EPS) * w_ref[...]).astype(o_ref.dtype)


def rmsnorm(x, w, *, tm=512, out_dtype=BF16):
    m, d = x.shape
    return pl.pallas_call(
        _rmsnorm_kernel,
        grid=(m // tm,),
        in_specs=[pl.BlockSpec((tm, d), lambda i: (i, 0)),
                  pl.BlockSpec((1, d), lambda i: (0, 0))],
        out_specs=pl.BlockSpec((tm, d), lambda i: (i, 0)),
        out_shape=jax.ShapeDtypeStruct((m, d), out_dtype),
        name="rmsnorm",
    )(x, w.reshape(1, d))


IN_TM = 1024
IN_TN = 512
IN_ROWS = 256


def _in_proj_kernel(x_ref, nw_ref, w_ref, ws_ref, cw_ref, cb_ref, o_ref, small_ref, xn_ref, u_ref, u2_ref, halo_ref,
                    *, conv_front, conv_back, tiles_per_seq):
    i = pl.program_id(0)
    j = pl.program_id(1)
    tm = x_ref.shape[0]

    @pl.when(j == 0)
    def _():
        x = x_ref[...]
        ms = jnp.mean(x * x, axis=-1, keepdims=True)
        xn_ref[...] = (x * lax.rsqrt(ms + RMS_EPS) * nw_ref[...]).astype(xn_ref.dtype)
        small_ref[...] = _dot(xn_ref[...], ws_ref[...])

    is_conv = (j < conv_front) | (j >= conv_back)

    @pl.when(jnp.logical_not(is_conv))
    def _():
        o_ref[...] = _dot(xn_ref[...], w_ref[...]).astype(o_ref.dtype)

    @pl.when(is_conv)
    def _():
        first = lax.rem(i, tiles_per_seq) == 0

        @pl.when(first)
        def _():
            u_ref[0:HALO, :] = jnp.zeros((HALO, u_ref.shape[1]), F32)

        @pl.when(jnp.logical_not(first))
        def _():
            u_ref[0:HALO, :] = halo_ref[j]

        bufs = (u_ref, u2_ref)
        nblk = tm // IN_ROWS

        def project(b):
            bufs[b % 2][HALO:HALO + IN_ROWS, :] = _dot(xn_ref[b * IN_ROWS:(b + 1) * IN_ROWS, :], w_ref[...])

        def activate(b):
            buf = bufs[b % 2]
            acc = cb_ref[...] + cw_ref[0:1, :] * buf[HALO - 3:HALO - 3 + IN_ROWS, :]
            for t in range(1, SSM_CONV):
                acc = acc + cw_ref[t:t + 1, :] * buf[HALO - 3 + t:HALO - 3 + t + IN_ROWS, :]
            o_ref[b * IN_ROWS:(b + 1) * IN_ROWS, :] = _silu(acc).astype(o_ref.dtype)

        project(0)
        for b in range(1, nblk):
            bufs[b % 2][0:HALO, :] = bufs[(b - 1) % 2][IN_ROWS:IN_ROWS + HALO, :]
            project(b)
            activate(b - 1)
        activate(nblk - 1)
        halo_ref[j] = bufs[(nblk - 1) % 2][IN_ROWS:IN_ROWS + HALO, :]


def in_proj(x, nw, w, w_small, conv_w, conv_b, *, seq):
    m, k = x.shape
    assert GDN_CONV == SSM_CONV and P_GDN_QKV == 0 and P_SSM_X % IN_TN == 0 and (3 * GDN_WIDTH) % IN_TN == 0
    n_tiles = P_COLS // IN_TN
    return pl.pallas_call(
        functools.partial(_in_proj_kernel, conv_front=3 * GDN_WIDTH // IN_TN, conv_back=P_SSM_X // IN_TN,
                          tiles_per_seq=seq // IN_TM),
        grid=(m // IN_TM, n_tiles),
        in_specs=[pl.BlockSpec((IN_TM, k), lambda i, j: (i, 0)),
                  pl.BlockSpec((1, k), lambda i, j: (0, 0)),
                  pl.BlockSpec((k, IN_TN), lambda i, j: (0, j)),
                  pl.BlockSpec((k, LANES), lambda i, j: (0, 0)),
                  pl.BlockSpec((SSM_CONV, IN_TN), lambda i, j: (0, j)),
                  pl.BlockSpec((1, IN_TN), lambda i, j: (0, j))],
        out_specs=[pl.BlockSpec((IN_TM, IN_TN), lambda i, j: (i, j)),
                   pl.BlockSpec((IN_TM, LANES), lambda i, j: (i, 0))],
        out_shape=[jax.ShapeDtypeStruct((m, P_COLS), BF16), jax.ShapeDtypeStruct((m, LANES), F32)],
        scratch_shapes=[pltpu.VMEM((IN_TM, k), BF16),
                        pltpu.VMEM((HALO + IN_ROWS, IN_TN), F32),
                        pltpu.VMEM((HALO + IN_ROWS, IN_TN), F32),
                        pltpu.VMEM((n_tiles, HALO, IN_TN), F32)],
        compiler_params=pltpu.CompilerParams(
            dimension_semantics=("arbitrary", "arbitrary"), vmem_limit_bytes=VMEM_LIMIT),
        name="in_proj",
    )(x, nw.reshape(1, k), w, w_small, conv_w, conv_b)


def _matmul_res_norm_kernel(a_ref, w_ref, h_ref, nw_ref, ho_ref, no_ref):
    h = h_ref[...] + _dot(a_ref[...], w_ref[...])
    ho_ref[...] = h
    ms = jnp.mean(h * h, axis=-1, keepdims=True)
    no_ref[...] = (h * lax.rsqrt(ms + RMS_EPS) * nw_ref[...]).astype(no_ref.dtype)


def matmul_res_norm(a, w, h, nw, *, tm, name):
    m, k = a.shape
    _, n = w.shape
    row = lambda i: (i, 0)
    return pl.pallas_call(
        _matmul_res_norm_kernel,
        grid=(m // tm,),
        in_specs=[pl.BlockSpec((tm, k), row),
                  pl.BlockSpec((k, n), lambda i: (0, 0)),
                  pl.BlockSpec((tm, n), row),
                  pl.BlockSpec((1, n), lambda i: (0, 0))],
        out_specs=[pl.BlockSpec((tm, n), row), pl.BlockSpec((tm, n), row)],
        out_shape=[jax.ShapeDtypeStruct((m, n), F32), jax.ShapeDtypeStruct((m, n), BF16)],
        compiler_params=pltpu.CompilerParams(
            dimension_semantics=("parallel",), vmem_limit_bytes=VMEM_LIMIT),
        name=name,
    )(a, w, h, nw.reshape(1, n))


def _matmul_res_kernel(a_ref, w_ref, h_ref, o_ref):
    o_ref[...] = h_ref[...] + _dot(a_ref[...], w_ref[...])


def matmul_res(a, w, h, *, tm, tn, name):
    m, k = a.shape
    _, n = w.shape
    return pl.pallas_call(
        _matmul_res_kernel,
        grid=(n // tn, m // tm),
        in_specs=[pl.BlockSpec((tm, k), lambda j, i: (i, 0)),
                  pl.BlockSpec((k, tn), lambda j, i: (0, j)),
                  pl.BlockSpec((tm, tn), lambda j, i: (i, j))],
        out_specs=pl.BlockSpec((tm, tn), lambda j, i: (i, j)),
        out_shape=jax.ShapeDtypeStruct((m, n), F32),
        compiler_params=pltpu.CompilerParams(
            dimension_semantics=("parallel", "parallel"), vmem_limit_bytes=VMEM_LIMIT),
        name=name,
    )(a, w, h)


DA_BIAS_LANES = 3


def _diff_attn_kernel(slopes_ref, lamp_ref, subln_ref, q_ref, k_ref, v_ref, o_ref,
                      k1_ref, k2_ref, s_ref, st_ref, p_ref, *, tq, lambda_init):
    hd = DA_HEAD_DIM
    seq = q_ref.shape[0]
    nq = seq // tq
    half = tq // 2
    slope = slopes_ref[pl.program_id(1)]
    lp = lamp_ref[...]
    lam = (jnp.exp(jnp.sum(lp[0:1] * lp[1:2], axis=-1, keepdims=True))
           - jnp.exp(jnp.sum(lp[2:3] * lp[3:4], axis=-1, keepdims=True)) + lambda_init)

    lane_k = lax.broadcasted_iota(jnp.int32, (seq, 2 * hd), 1)
    bias = (slope * LOG2E) * lax.broadcasted_iota(jnp.int32, (seq, 2 * hd), 0).astype(F32)
    pieces = []
    rest = bias
    for _ in range(DA_BIAS_LANES):
        piece = rest.astype(BF16).astype(F32)
        pieces.append(piece)
        rest = rest - piece
    kf = k_ref[...].astype(F32)
    k1 = jnp.where(lane_k < hd, kf, 0.0)
    k2 = jnp.where(lane_k >= hd, kf, 0.0)
    for t, piece in enumerate(pieces):
        k1 = jnp.where(lane_k == hd + t, piece, k1)
        k2 = jnp.where(lane_k == t, piece, k2)
    k1_ref[...] = k1.astype(BF16)
    k2_ref[...] = k2.astype(BF16)

    lane_q = lax.broadcasted_iota(jnp.int32, (tq, 2 * hd), 1)
    causal = (lax.broadcasted_iota(jnp.int32, (tq, tq), 0)
              >= lax.broadcasted_iota(jnp.int32, (tq, tq), 1))

    def fold(x):
        return x[:, :half], x[:, half:]

    for qi in range(nq):
        rows = slice(qi * tq, (qi + 1) * tq)
        qf = q_ref[rows, :].astype(F32) * (hd ** -0.5 * LOG2E)
        q1 = jnp.where(lane_q < hd, qf, jnp.where(lane_q < hd + DA_BIAS_LANES, 1.0, 0.0)).astype(BF16)
        q2 = jnp.where(lane_q >= hd, qf, jnp.where(lane_q < DA_BIAS_LANES, 1.0, 0.0)).astype(BF16)

        mx = [None, None]
        for j in range(qi + 1):
            keys = slice(j * tq, (j + 1) * tq)
            for mp, (qm, km_ref) in enumerate(((q1, k1_ref), (q2, k2_ref))):
                s = _dot_nt(qm, km_ref[keys, :])
                if j == qi:
                    s = jnp.where(causal, s, -jnp.inf)
                s_ref[mp, j] = s
                a, b = fold(s)
                ab = jnp.maximum(a, b)
                mx[mp] = ab if mx[mp] is None else jnp.maximum(mx[mp], ab)
        for mp in range(2):
            st_ref[mp] = jnp.broadcast_to(jnp.max(mx[mp], axis=-1, keepdims=True), (tq, half))

        sm = [None, None]
        for j in range(qi + 1):
            for mp in range(2):
                m = st_ref[mp]
                a, b = fold(s_ref[mp, j])
                pa = jnp.exp2(a - m)
                pb = jnp.exp2(b - m)
                p_ref[mp, :, j * tq:(j + 1) * tq] = jnp.concatenate([pa, pb], axis=1).astype(BF16)
                sm[mp] = pa + pb if sm[mp] is None else sm[mp] + (pa + pb)
        l1 = jnp.sum(sm[0], axis=-1, keepdims=True)
        l2 = jnp.sum(sm[1], axis=-1, keepdims=True)

        kv = (qi + 1) * tq
        o = (_dot(p_ref[0, :, :kv], v_ref[:kv, :]) * (1.0 / l1)
             - _dot(p_ref[1, :, :kv], v_ref[:kv, :]) * (lam / l2))
        ms = jnp.mean(o * o, axis=-1, keepdims=True)
        o = o * lax.rsqrt(ms + RMS_EPS) * subln_ref[...] * (1.0 - lambda_init)
        o_ref[rows, :] = o.astype(o_ref.dtype)


def diff_attention(p3, lam_params, subln_w, lambda_init, *, tq=256):
    bsz, seq, _ = p3.shape
    hw = 2 * DA_HEAD_DIM
    nq = seq // tq
    slopes = jnp.asarray([2.0 ** (-8.0 * (h + 1) / DA_HEADS) for h in range(DA_HEADS)], F32)
    qb, kb, vb = P_DA_Q // hw, P_DA_K // hw, P_DA_V // hw
    return pl.pallas_call(
        functools.partial(_diff_attn_kernel, tq=tq, lambda_init=lambda_init),
        grid=(bsz, DA_HEADS),
        in_specs=[pl.BlockSpec(memory_space=pltpu.SMEM),
                  pl.BlockSpec((4, DA_HEAD_DIM), lambda b, h: (0, 0)),
                  pl.BlockSpec((1, hw), lambda b, h: (0, 0)),
                  pl.BlockSpec((None, seq, hw), lambda b, h: (b, 0, qb + h)),
                  pl.BlockSpec((None, seq, hw), lambda b, h: (b, 0, kb + h)),
                  pl.BlockSpec((None, seq, hw), lambda b, h: (b, 0, vb + h))],
        out_specs=pl.BlockSpec((None, seq, hw), lambda b, h: (b, 0, h)),
        out_shape=jax.ShapeDtypeStruct((bsz, seq, DA_WIDTH), BF16),
        scratch_shapes=[pltpu.VMEM((seq, hw), BF16),
                        pltpu.VMEM((seq, hw), BF16),
                        pltpu.VMEM((2, nq, tq, tq), F32),
                        pltpu.VMEM((2, tq, tq // 2), F32),
                        pltpu.VMEM((2, tq, seq), BF16)],
        compiler_params=pltpu.CompilerParams(
            dimension_semantics=("parallel", "parallel"), vmem_limit_bytes=VMEM_LIMIT),
        name="diff_attention",
    )(slopes, lam_params, subln_w.reshape(1, hw), p3, p3, p3)


def _ssd_kernel(x_ref, bc_ref, z_ref, sm_ref, dtb_ref, alog_ref, dskip_ref, nw_ref,
                o_ref, state_ref, y_ref):
    q = SSM_CHUNK
    c = pl.program_id(1)

    @pl.when(c == 0)
    def _():
        state_ref[...] = jnp.zeros_like(state_ref)

    xs_b = x_ref[...]
    xs = xs_b.astype(F32)
    bc = bc_ref[...]

    dt = _softplus(sm_ref[...] + dtb_ref[...])
    da = dt * (-jnp.exp(alog_ref[...]))
    row = lax.broadcasted_iota(jnp.int32, (q, q), 0)
    colm = lax.broadcasted_iota(jnp.int32, (q, q), 1)
    causal = row >= colm
    tri = causal.astype(F32)
    a_cs = _dot_f32(tri, da)
    a_cs_t = a_cs.T
    dt_t = dt.T
    lane = lax.broadcasted_iota(jnp.int32, (1, LANES), 1)

    for g in range(SSM_GROUPS):
        bm_b = bc[:, g * SSM_STATE:(g + 1) * SSM_STATE]
        cm_b = bc[:, (SSM_GROUPS + g) * SSM_STATE:(SSM_GROUPS + g + 1) * SSM_STATE]
        cm = cm_b.astype(F32)
        cb = _dot_nt(cm_b, bm_b)
        bm_t = bm_b.astype(F32).T
        hpg = SSM_HEADS // SSM_GROUPS
        for pr in range(hpg // 2):
            pair = g * (hpg // 2) + pr
            x_pair = xs_b[:, pair * LANES:(pair + 1) * LANES]
            st = state_ref[pair]
            rhs = jnp.concatenate([x_pair, st.astype(BF16)], axis=0)
            ys, sts, cds = [], [], []
            for sub in range(2):
                h = 2 * pair + sub
                acol = a_cs[:, h:h + 1]
                arow = a_cs_t[h:h + 1, :]
                dtrow = dt_t[h:h + 1, :]
                decay = jnp.exp(jnp.where(causal, acol - arow, -jnp.inf))
                sc = (cb * decay * dtrow).astype(BF16)
                c_in = (cm * jnp.exp(acol)).astype(BF16)
                ys.append(_dot(jnp.concatenate([sc, c_in], axis=1), rhs))
                a_last = arow[:, q - 1:q]
                wrow = jnp.exp(a_last - arow) * dtrow
                sts.append(_dot((bm_t * wrow).astype(BF16), x_pair))
                cds.append(jnp.exp(a_last))
            first = lane < SSM_HEAD_DIM
            y_ref[:, pair * LANES:(pair + 1) * LANES] = jnp.where(first, ys[0], ys[1])
            state_ref[pair] = (st * jnp.where(first, cds[0], cds[1])
                               + jnp.where(first, sts[0], sts[1]))

    y = y_ref[...] + xs * dskip_ref[...]
    y = y * _silu(z_ref[...].astype(F32))
    gw = SSM_INNER // SSM_GROUPS
    for g in range(SSM_GROUPS):
        yg = y[:, g * gw:(g + 1) * gw]
        ms = jnp.mean(yg * yg, axis=-1, keepdims=True)
        o_ref[:, g * gw:(g + 1) * gw] = (
            yg * lax.rsqrt(ms + RMS_EPS) * nw_ref[:, g * gw:(g + 1) * gw]).astype(o_ref.dtype)


def _pad_row(v, offset):
    return jnp.zeros((1, LANES), F32).at[0, offset:offset + v.shape[0]].set(v.astype(F32))


def mamba2_ssd(p3, small3, dt_bias, a_log, d_skip, norm_w):
    bsz, seq, _ = p3.shape
    q = SSM_CHUNK
    nc = seq // q
    const = lambda b, c: (0, 0)
    return pl.pallas_call(
        _ssd_kernel,
        grid=(bsz, nc),
        in_specs=[pl.BlockSpec((None, q, SSM_INNER), lambda b, c: (b, c, P_SSM_X // SSM_INNER)),
                  pl.BlockSpec((None, q, SSM_BC), lambda b, c: (b, c, P_SSM_BC // SSM_BC)),
                  pl.BlockSpec((None, q, SSM_INNER), lambda b, c: (b, c, P_SSM_Z // SSM_INNER)),
                  pl.BlockSpec((None, q, LANES), lambda b, c: (b, c, 0)),
                  pl.BlockSpec((1, LANES), const),
                  pl.BlockSpec((1, LANES), const),
                  pl.BlockSpec((1, SSM_INNER), const),
                  pl.BlockSpec((1, SSM_INNER), const)],
        out_specs=pl.BlockSpec((None, q, SSM_INNER), lambda b, c: (b, c, 0)),
        out_shape=jax.ShapeDtypeStruct((bsz, seq, SSM_INNER), BF16),
        scratch_shapes=[pltpu.VMEM((SSM_HEADS // 2, SSM_STATE, LANES), F32),
                        pltpu.VMEM((q, SSM_INNER), F32)],
        compiler_params=pltpu.CompilerParams(
            dimension_semantics=("parallel", "arbitrary"), vmem_limit_bytes=VMEM_LIMIT),
        name="mamba2_ssd",
    )(p3, p3, p3, small3, _pad_row(dt_bias, S_DT),
      _pad_row(a_log, S_DT), jnp.repeat(d_skip.astype(F32), SSM_HEAD_DIM).reshape(1, SSM_INNER),
      norm_w.reshape(1, SSM_INNER))


def _gdn_kernel(act_ref, z_ref, sm_ref, dtb_ref, alog_ref, nw_ref, o_ref, state_ref):
    cs = GDN_CHUNK
    rr = GDN_ROWS
    nch = rr // cs
    d = GDN_HEAD_DIM
    sh = int(math.log2(cs))
    step = pl.program_id(1)

    @pl.when(step == 0)
    def _():
        state_ref[...] = jnp.zeros_like(state_ref)

    nseq = act_ref.shape[0]
    row = lax.broadcasted_iota(jnp.int32, (rr, rr), 0)
    colm = lax.broadcasted_iota(jnp.int32, (rr, rr), 1)
    same_blk = lax.shift_right_logical(row, sh) == lax.shift_right_logical(colm, sh)
    tri_bd = (same_blk & (row >= colm)).astype(F32)
    beta, g_cs, g_t, beta_t = [], [], [], []
    for sq in range(nseq):
        sm = sm_ref[sq]
        beta.append(1.0 / (1.0 + jnp.exp(-sm)))
        gl = -jnp.exp(alog_ref[...]) * _softplus(sm + dtb_ref[...])
        g_cs.append(_dot_f32(tri_bd, gl))
        g_t.append(g_cs[sq].T)
        beta_t.append(beta[sq].T)

    l_idx = lax.broadcasted_iota(jnp.int32, (cs, rr), 0)
    j_idx = lax.broadcasted_iota(jnp.int32, (cs, rr), 1)
    s_idx = jnp.bitwise_and(j_idx, cs - 1)
    blk = lax.shift_right_logical(j_idx, sh)
    incl_cat = l_idx >= s_idx
    strict_cat = l_idx > s_idx
    eye_cat = (l_idx == s_idx).astype(F32)
    blk_row = lax.shift_right_logical(lax.broadcasted_iota(jnp.int32, (1, rr), 1), sh)

    def to_cat(x):
        if x.shape[1] != rr:
            x = jnp.concatenate([x] * (rr // x.shape[1]), axis=1)
        out = x[(nch - 1) * cs:]
        for c in reversed(range(nch - 1)):
            out = jnp.where(blk == c, x[c * cs:(c + 1) * cs], out)
        return out

    def to_bd(x_cat):
        return jnp.where(same_blk, jnp.concatenate([x_cat] * nch, axis=0), jnp.zeros((), x_cat.dtype))

    pairs = [(sq, h) for sq in range(nseq) for h in range(GDN_HEADS)]
    items = range(len(pairs))
    qn, kn_b, v_b, kn_t, attn_cat, p_cat, t_cat, grow, eg_full = ([None] * len(pairs) for _ in range(9))
    for i in items:
        sq, h = pairs[i]
        qh = act_ref[sq, :, h * d:(h + 1) * d].astype(F32)
        kh = act_ref[sq, :, GDN_WIDTH + h * d:GDN_WIDTH + (h + 1) * d].astype(F32)
        qn[i] = (qh * lax.rsqrt(jnp.sum(qh * qh, axis=-1, keepdims=True) + L2_EPS) * (d ** -0.5)).astype(BF16)
        kn = kh * lax.rsqrt(jnp.sum(kh * kh, axis=-1, keepdims=True) + L2_EPS)
        kn_b[i] = kn.astype(BF16)
        kn_t[i] = kn.T
        v_b[i] = act_ref[sq, :, 2 * GDN_WIDTH + h * d:2 * GDN_WIDTH + (h + 1) * d]
        gcol_full = jnp.broadcast_to(g_cs[sq][:, S_DECAY + h:S_DECAY + h + 1], (rr, LANES))
        bcol_full = jnp.broadcast_to(beta[sq][:, S_BETA + h:S_BETA + h + 1], (rr, LANES))
        eg_full[i] = jnp.exp(gcol_full)
        grow[i] = g_t[sq][S_DECAY + h:S_DECAY + h + 1, :]
        dec = jnp.exp(jnp.where(incl_cat, to_cat(gcol_full) - grow[i], 0.0))
        kk = to_cat(_dot_nt(kn_b[i], kn_b[i]))
        qk = to_cat(_dot_nt(qn[i], kn_b[i]))
        attn_cat[i] = jnp.where(incl_cat, qk * dec, 0.0)
        p_cat[i] = jnp.where(strict_cat, -(kk * dec * to_cat(bcol_full)), 0.0)
        t_cat[i] = eye_cat + p_cat[i]

    p_bd = [to_bd(p_cat[i].astype(BF16)) for i in items]
    for _ in range(sh - 1):
        for i in items:
            p_cat[i] = _dot(p_cat[i].astype(BF16), p_bd[i])
        for i in items:
            p_bd[i] = to_bd(p_cat[i].astype(BF16))
        for i in items:
            t_cat[i] = t_cat[i] + _dot(t_cat[i].astype(BF16), p_bd[i])

    u, w = [None] * len(pairs), [None] * len(pairs)
    for i in items:
        sq, h = pairs[i]
        brow = beta_t[sq][S_BETA + h:S_BETA + h + 1, :]
        u[i] = _dot(to_bd((t_cat[i] * brow).astype(BF16)), v_b[i])
        w[i] = _dot(to_bd((t_cat[i] * (brow * jnp.exp(grow[i]))).astype(BF16)), kn_b[i])

    st = [state_ref[i] for i in items]
    zero_blk = jnp.zeros((cs, d), BF16)
    for c in range(nch):
        rows = slice(c * cs, (c + 1) * cs)
        for i in items:
            sq, h = pairs[i]
            ws = _dot(jnp.concatenate([w[i][rows].astype(BF16), qn[i][rows]], axis=0), st[i].astype(BF16))
            v_new = (u[i][rows] - ws[:cs]).astype(BF16)
            rhs = jnp.concatenate([zero_blk] * c + [v_new] + [zero_blk] * (nch - 1 - c), axis=0)
            g_last = grow[i][:, (c + 1) * cs - 1:(c + 1) * cs]
            e_row = jnp.exp(jnp.where(blk_row == c, g_last - grow[i], 0.0))
            lhs = jnp.concatenate(
                [jnp.where(blk == c, attn_cat[i], 0.0),
                 jnp.where(blk_row == c, kn_t[i] * e_row, 0.0)], axis=0).astype(BF16)
            r = _dot(lhs, rhs)
            o = eg_full[i][rows] * ws[cs:] + r[:cs]
            st[i] = st[i] * jnp.exp(g_last) + r[cs:]
            ms = jnp.mean(o * o, axis=-1, keepdims=True)
            zh = z_ref[sq, rows, h * d:(h + 1) * d].astype(F32)
            o_ref[sq, rows, h * d:(h + 1) * d] = (
                o * lax.rsqrt(ms + RMS_EPS) * nw_ref[...] * _silu(zh)).astype(o_ref.dtype)
    for i in items:
        state_ref[i] = st[i]


def gated_deltanet(p3, small3, dt_bias, a_log, norm_w):
    bsz, seq, _ = p3.shape
    rr = GDN_ROWS
    ns = GDN_SEQS if bsz % GDN_SEQS == 0 else 1
    w3 = 3 * GDN_WIDTH
    const = lambda b, c: (0, 0)
    return pl.pallas_call(
        _gdn_kernel,
        grid=(bsz // ns, seq // rr),
        in_specs=[pl.BlockSpec((ns, rr, w3), lambda b, c: (b, c, P_GDN_QKV // w3)),
                  pl.BlockSpec((ns, rr, GDN_WIDTH), lambda b, c: (b, c, P_GDN_Z // GDN_WIDTH)),
                  pl.BlockSpec((ns, rr, LANES), lambda b, c: (b, c, 0)),
                  pl.BlockSpec((1, LANES), const),
                  pl.BlockSpec((1, LANES), const),
                  pl.BlockSpec((1, GDN_HEAD_DIM), const)],
        out_specs=pl.BlockSpec((ns, rr, GDN_WIDTH), lambda b, c: (b, c, 0)),
        out_shape=jax.ShapeDtypeStruct((bsz, seq, GDN_WIDTH), BF16),
        scratch_shapes=[pltpu.VMEM((ns * GDN_HEADS, GDN_HEAD_DIM, GDN_HEAD_DIM), F32)],
        compiler_params=pltpu.CompilerParams(
            dimension_semantics=("parallel", "arbitrary"), vmem_limit_bytes=VMEM_LIMIT),
        name="gated_deltanet",
    )(p3, p3, small3, _pad_row(dt_bias, S_DECAY), _pad_row(a_log, S_DECAY),
      norm_w.reshape(1, GDN_HEAD_DIM))


def _merge_kernel(oa_ref, os_ref, og_ref, wb_ref, g0_ref, g1_ref, g2_ref, o_ref):
    acc = None
    for br, (x_ref, g_ref) in enumerate(((oa_ref, g0_ref), (os_ref, g1_ref), (og_ref, g2_ref))):
        gate = 1.0 / (1.0 + jnp.exp(-g_ref[...].astype(F32)))
        term = gate * _dot(x_ref[...], wb_ref[br].astype(BF16))
        acc = term if acc is None else acc + term
    o_ref[...] = acc.astype(o_ref.dtype)


def merge_branches(o_da, o_ssm, o_gdn, wb3, p, *, tm=1024, tn=512):
    m = o_da.shape[0]
    gb = P_GATES // tn
    nb = D_MODEL // tn
    act = pl.BlockSpec((tm, DA_WIDTH), lambda i, j: (i, 0))
    return pl.pallas_call(
        _merge_kernel,
        grid=(m // tm, nb),
        in_specs=[act, act, act,
                  pl.BlockSpec((N_BRANCH, DA_WIDTH, tn), lambda i, j: (0, 0, j)),
                  pl.BlockSpec((tm, tn), lambda i, j: (i, gb + j)),
                  pl.BlockSpec((tm, tn), lambda i, j: (i, gb + nb + j)),
                  pl.BlockSpec((tm, tn), lambda i, j: (i, gb + 2 * nb + j))],
        out_specs=pl.BlockSpec((tm, tn), lambda i, j: (i, j)),
        out_shape=jax.ShapeDtypeStruct((m, D_MODEL), BF16),
        compiler_params=pltpu.CompilerParams(
            dimension_semantics=("parallel", "arbitrary"), vmem_limit_bytes=VMEM_LIMIT),
        name="merge_branches",
    )(o_da, o_ssm, o_gdn, wb3, p, p, p)


def _ffn_up_kernel(x_ref, wg_ref, wv_ref, cwg_ref, cwv_ref, cbg_ref, cbv_ref, o_ref, ug_ref, uv_ref,
                   wgb_ref, wvb_ref, *, rows):
    seq = x_ref.shape[0]
    pad = HALO
    ug_ref[0:pad, :] = jnp.zeros((pad, ug_ref.shape[1]), F32)
    uv_ref[0:pad, :] = jnp.zeros((pad, uv_ref.shape[1]), F32)
    wgb_ref[...] = wg_ref[...].astype(BF16)
    wvb_ref[...] = wv_ref[...].astype(BF16)

    def project(r):
        xr = x_ref[r:r + rows, :]
        ug_ref[pad + r:pad + r + rows, :] = _dot(xr, wgb_ref[...])
        uv_ref[pad + r:pad + r + rows, :] = _dot(xr, wvb_ref[...])

    def conv(u_ref, cw_ref, cb_ref, r):
        acc = cb_ref[...] + cw_ref[0:1, :] * u_ref[pad + r - 2:pad + r - 2 + rows, :]
        for t in range(1, FFN_CONV):
            acc = acc + cw_ref[t:t + 1, :] * u_ref[pad + r - 2 + t:pad + r - 2 + t + rows, :]
        return acc

    def activate(r):
        gate = conv(ug_ref, cwg_ref, cbg_ref, r)
        val = conv(uv_ref, cwv_ref, cbv_ref, r)
        o_ref[r:r + rows, :] = (_silu(gate) * val).astype(o_ref.dtype)

    project(0)
    for r in range(rows, seq, rows):
        activate(r - rows)
        project(r)
    activate(seq - rows)


def ffn_up_proj(hn3, w_up, conv_w, conv_b, *, tf=512, rows=256):
    bsz, seq, d = hn3.shape
    nf = D_FF // tf
    cb = conv_b.reshape(1, 2 * D_FF)
    return pl.pallas_call(
        functools.partial(_ffn_up_kernel, rows=rows),
        grid=(bsz, nf),
        in_specs=[pl.BlockSpec((None, seq, d), lambda b, f: (b, 0, 0)),
                  pl.BlockSpec((d, tf), lambda b, f: (0, f)),
                  pl.BlockSpec((d, tf), lambda b, f: (0, nf + f)),
                  pl.BlockSpec((FFN_CONV, tf), lambda b, f: (0, f)),
                  pl.BlockSpec((FFN_CONV, tf), lambda b, f: (0, nf + f)),
                  pl.BlockSpec((1, tf), lambda b, f: (0, f)),
                  pl.BlockSpec((1, tf), lambda b, f: (0, nf + f))],
        out_specs=pl.BlockSpec((None, seq, tf), lambda b, f: (b, 0, f)),
        out_shape=jax.ShapeDtypeStruct((bsz, seq, D_FF), BF16),
        scratch_shapes=[pltpu.VMEM((HALO + seq, tf), F32), pltpu.VMEM((HALO + seq, tf), F32),
                        pltpu.VMEM((d, tf), BF16), pltpu.VMEM((d, tf), BF16)],
        compiler_params=pltpu.CompilerParams(
            dimension_semantics=("parallel", "arbitrary"), vmem_limit_bytes=VMEM_LIMIT),
        name="ffn_up",
    )(hn3, w_up, w_up, conv_w, conv_w, cb, cb)


def _reorder_w_in(w):
    sizes = (DA_WIDTH, DA_WIDTH, DA_WIDTH, SSM_INNER, SSM_INNER + SSM_BC, SSM_HEADS,
             3 * GDN_WIDTH, GDN_WIDTH, GDN_HEADS, GDN_HEADS, N_BRANCH * D_MODEL)
    offs = [0]
    for s in sizes:
        offs.append(offs[-1] + s)
    seg = lambda i: w[:, offs[i]:offs[i + 1]]
    da_q, da_k, da_v, ssm_z, ssm_xbc, ssm_dt, gdn_qkv, gdn_z, gdn_b, gdn_a, gates = (
        seg(i) for i in range(len(sizes)))
    big = jnp.concatenate([gdn_qkv, ssm_z, gdn_z, da_q, da_k, da_v, gates, ssm_xbc], axis=1)
    small = jnp.concatenate(
        [ssm_dt, gdn_b, gdn_a,
         jnp.zeros((w.shape[0], LANES - SSM_HEADS - 2 * GDN_HEADS), w.dtype)], axis=1)
    return big.astype(BF16), small.astype(BF16)


def kernel(x, norm_mix, w_in, da_lambda, da_subln, ssm_conv_w, ssm_conv_b, ssm_dt_bias, ssm_a_log,
           ssm_d, ssm_norm, gdn_conv_w, gdn_dt_bias, gdn_a_log, gdn_norm, w_branch, w_out, norm_ffn,
           ffn_up, ffn_conv_w, ffn_conv_b, ffn_down, norm_final):
    bsz, seq, d = x.shape
    m = bsz * seq
    h = x.reshape(m, d)
    mid = P_SSM_X - 3 * GDN_WIDTH
    for l in range(DEPTH):
        w_big, w_small = _reorder_w_in(w_in[l])
        conv_w = jnp.concatenate([gdn_conv_w[l], jnp.zeros((SSM_CONV, mid), F32), ssm_conv_w[l]], axis=1)
        conv_b = jnp.concatenate([jnp.zeros((P_SSM_X,), F32), ssm_conv_b[l]]).reshape(1, P_COLS)
        p, small = in_proj(h, norm_mix[l], w_big, w_small, conv_w, conv_b, seq=seq)
        p3 = p.reshape(bsz, seq, P_COLS)
        small3 = small.reshape(bsz, seq, LANES)
        lambda_init = 0.8 - 0.6 * math.exp(-0.3 * l)
        o_da = diff_attention(p3, da_lambda[l], da_subln[l], lambda_init)
        o_ssm = mamba2_ssd(p3, small3, ssm_dt_bias[l], ssm_a_log[l], ssm_d[l], ssm_norm[l])
        o_gdn = gated_deltanet(p3, small3, gdn_dt_bias[l], gdn_a_log[l], gdn_norm[l])
        wb3 = w_branch[l].reshape(N_BRANCH, DA_WIDTH, D_MODEL)
        merged = merge_branches(o_da.reshape(m, DA_WIDTH), o_ssm.reshape(m, SSM_INNER),
                                o_gdn.reshape(m, GDN_WIDTH), wb3, p)
        h, hn = matmul_res_norm(merged, w_out[l].astype(BF16), h, norm_ffn[l], tm=512, name="out_proj")
        act = ffn_up_proj(hn.reshape(bsz, seq, d), ffn_up[l], ffn_conv_w[l], ffn_conv_b[l])
        h = matmul_res(act.reshape(m, D_FF), ffn_down[l].astype(BF16), h, tm=512, tn=1024, name="ffn_down")
    return rmsnorm(h, norm_final, out_dtype=F32).reshape(bsz, seq, d)
```

```python
import functools
import math

import jax
import jax.numpy as jnp
from jax import lax
from jax.experimental import pallas as pl
from jax.experimental.pallas import tpu as pltpu

F32 = jnp.float32
BF16 = jnp.bfloat16

D_MODEL = 2048
DEPTH = 2
DA_HEADS = 8
DA_HEAD_DIM = 64
DA_WIDTH = DA_HEADS * 2 * DA_HEAD_DIM
SSM_HEADS = 16
SSM_HEAD_DIM = 64
SSM_INNER = SSM_HEADS * SSM_HEAD_DIM
SSM_GROUPS = 2
SSM_STATE = 128
SSM_CONV = 4
SSM_CHUNK = 128
SSM_BC = 2 * SSM_GROUPS * SSM_STATE
GDN_HEADS = 8
GDN_HEAD_DIM = 128
GDN_WIDTH = GDN_HEADS * GDN_HEAD_DIM
GDN_CONV = 4
GDN_CHUNK = 64
GDN_ROWS = 256
GDN_SEQS = 1
D_FF = 5632
FFN_CONV = 3
N_BRANCH = 3
RMS_EPS = 1e-6
L2_EPS = 1e-6
LOG2E = math.log2(math.e)

LANES = 128
HALO = 8
VMEM_LIMIT = 56 * 1024 * 1024

P_GDN_QKV = 0
P_SSM_Z = P_GDN_QKV + 3 * GDN_WIDTH
P_GDN_Z = P_SSM_Z + SSM_INNER
P_DA_Q = P_GDN_Z + GDN_WIDTH
P_DA_K = P_DA_Q + DA_WIDTH
P_DA_V = P_DA_K + DA_WIDTH
P_GATES = P_DA_V + DA_WIDTH
P_SSM_X = P_GATES + N_BRANCH * D_MODEL
P_SSM_BC = P_SSM_X + SSM_INNER
P_COLS = P_SSM_BC + SSM_BC
S_DT = 0
S_BETA = SSM_HEADS
S_DECAY = SSM_HEADS + GDN_HEADS


def _silu(x):
    return x / (1.0 + jnp.exp(-x))


def _softplus(x):
    return jnp.maximum(x, 0.0) + jnp.log1p(jnp.exp(-jnp.abs(x)))


def _dot(a, b):
    return jnp.dot(a, b, preferred_element_type=F32)


def _dot_nt(a, b):
    return lax.dot_general(a, b, (((1,), (1,)), ((), ())), preferred_element_type=F32)


def _dot_f32(a, b):
    return jnp.dot(a, b, preferred_element_type=F32, precision=lax.Precision.HIGHEST)


def _rmsnorm_kernel(x_ref, w_ref, o_ref):
    x = x_ref[...]
    ms = jnp.mean(x * x, axis=-1, keepdims=True)
    o_ref[...] = (x * lax.rsqrt(ms + RMS_EPS) * w_ref[...]).astype(o_ref.dtype)


def rmsnorm(x, w, *, tm=512, out_dtype=BF16):
    m, d = x.shape
    return pl.pallas_call(
        _rmsnorm_kernel,
        grid=(m // tm,),
        in_specs=[pl.BlockSpec((tm, d), lambda i: (i, 0)),
                  pl.BlockSpec((1, d), lambda i: (0, 0))],
        out_specs=pl.BlockSpec((tm, d), lambda i: (i, 0)),
        out_shape=jax.ShapeDtypeStruct((m, d), out_dtype),
        name="rmsnorm",
    )(x, w.reshape(1, d))


IN_TM = 1024
IN_TN = 512
IN_ROWS = 256


def _in_proj_kernel(x_ref, nw_ref, w_ref, ws_ref, cw_ref, cb_ref, o_ref, small_ref, xn_ref, u_ref, u2_ref, halo_ref,
                    *, conv_front, conv_back, tiles_per_seq):
    i = pl.program_id(0)
    j = pl.program_id(1)
    tm = x_ref.shape[0]

    @pl.when(j == 0)
    def _():
        x = x_ref[...]
        ms = jnp.mean(x * x, axis=-1, keepdims=True)
        xn_ref[...] = (x * lax.rsqrt(ms + RMS_EPS) * nw_ref[...]).astype(xn_ref.dtype)
        small_ref[...] = _dot(xn_ref[...], ws_ref[...])

    is_conv = (j < conv_front) | (j >= conv_back)

    @pl.when(jnp.logical_not(is_conv))
    def _():
        o_ref[...] = _dot(xn_ref[...], w_ref[...]).astype(o_ref.dtype)

    @pl.when(is_conv)
    def _():
        first = lax.rem(i, tiles_per_seq) == 0

        @pl.when(first)
        def _():
            u_ref[0:HALO, :] = jnp.zeros((HALO, u_ref.shape[1]), F32)

        @pl.when(jnp.logical_not(first))
        def _():
            u_ref[0:HALO, :] = halo_ref[j]

        bufs = (u_ref, u2_ref)
        nblk = tm // IN_ROWS

        def project(b):
            bufs[b % 2][HALO:HALO + IN_ROWS, :] = _dot(xn_ref[b * IN_ROWS:(b + 1) * IN_ROWS, :], w_ref[...])

        def activate(b):
            buf = bufs[b % 2]
            acc = cb_ref[...] + cw_ref[0:1, :] * buf[HALO - 3:HALO - 3 + IN_ROWS, :]
            for t in range(1, SSM_CONV):
                acc = acc + cw_ref[t:t + 1, :] * buf[HALO - 3 + t:HALO - 3 + t + IN_ROWS, :]
            o_ref[b * IN_ROWS:(b + 1) * IN_ROWS, :] = _silu(acc).astype(o_ref.dtype)

        project(0)
        for b in range(1, nblk):
            bufs[b % 2][0:HALO, :] = bufs[(b - 1) % 2][IN_ROWS:IN_ROWS + HALO, :]
            project(b)
            activate(b - 1)
        activate(nblk - 1)
        halo_ref[j] = bufs[(nblk - 1) % 2][IN_ROWS:IN_ROWS + HALO, :]


def in_proj(x, nw, w, w_small, conv_w, conv_b, *, seq):
    m, k = x.shape
    assert GDN_CONV == SSM_CONV and P_GDN_QKV == 0 and P_SSM_X % IN_TN == 0 and (3 * GDN_WIDTH) % IN_TN == 0
    n_tiles = P_COLS // IN_TN
    return pl.pallas_call(
        functools.partial(_in_proj_kernel, conv_front=3 * GDN_WIDTH // IN_TN, conv_back=P_SSM_X // IN_TN,
                          tiles_per_seq=seq // IN_TM),
        grid=(m // IN_TM, n_tiles),
        in_specs=[pl.BlockSpec((IN_TM, k), lambda i, j: (i, 0)),
                  pl.BlockSpec((1, k), lambda i, j: (0, 0)),
                  pl.BlockSpec((k, IN_TN), lambda i, j: (0, j)),
                  pl.BlockSpec((k, LANES), lambda i, j: (0, 0)),
                  pl.BlockSpec((SSM_CONV, IN_TN), lambda i, j: (0, j)),
                  pl.BlockSpec((1, IN_TN), lambda i, j: (0, j))],
        out_specs=[pl.BlockSpec((IN_TM, IN_TN), lambda i, j: (i, j)),
                   pl.BlockSpec((IN_TM, LANES), lambda i, j: (i, 0))],
        out_shape=[jax.ShapeDtypeStruct((m, P_COLS), BF16), jax.ShapeDtypeStruct((m, LANES), F32)],
        scratch_shapes=[pltpu.VMEM((IN_TM, k), BF16),
                        pltpu.VMEM((HALO + IN_ROWS, IN_TN), F32),
                        pltpu.VMEM((HALO + IN_ROWS, IN_TN), F32),
                        pltpu.VMEM((n_tiles, HALO, IN_TN), F32)],
        compiler_params=pltpu.CompilerParams(
            dimension_semantics=("arbitrary", "arbitrary"), vmem_limit_bytes=VMEM_LIMIT),
        name="in_proj",
    )(x, nw.reshape(1, k), w, w_small, conv_w, conv_b)


def _matmul_res_norm_kernel(a_ref, w_ref, h_ref, nw_ref, ho_ref, no_ref):
    h = h_ref[...] + _dot(a_ref[...], w_ref[...])
    ho_ref[...] = h
    ms = jnp.mean(h * h, axis=-1, keepdims=True)
    no_ref[...] = (h * lax.rsqrt(ms + RMS_EPS) * nw_ref[...]).astype(no_ref.dtype)


def matmul_res_norm(a, w, h, nw, *, tm, name):
    m, k = a.shape
    _, n = w.shape
    row = lambda i: (i, 0)
    return pl.pallas_call(
        _matmul_res_norm_kernel,
        grid=(m // tm,),
        in_specs=[pl.BlockSpec((tm, k), row),
                  pl.BlockSpec((k, n), lambda i: (0, 0)),
                  pl.BlockSpec((tm, n), row),
                  pl.BlockSpec((1, n), lambda i: (0, 0))],
        out_specs=[pl.BlockSpec((tm, n), row), pl.BlockSpec((tm, n), row)],
        out_shape=[jax.ShapeDtypeStruct((m, n), F32), jax.ShapeDtypeStruct((m, n), BF16)],
        compiler_params=pltpu.CompilerParams(
            dimension_semantics=("parallel",), vmem_limit_bytes=VMEM_LIMIT),
        name=name,
    )(a, w, h, nw.reshape(1, n))


def _matmul_res_kernel(a_ref, w_ref, h_ref, o_ref):
    o_ref[...] = h_ref[...] + _dot(a_ref[...], w_ref[...])


def matmul_res(a, w, h, *, tm, tn, name):
    m, k = a.shape
    _, n = w.shape
    return pl.pallas_call(
        _matmul_res_kernel,
        grid=(n // tn, m // tm),
        in_specs=[pl.BlockSpec((tm, k), lambda j, i: (i, 0)),
                  pl.BlockSpec((k, tn), lambda j, i: (0, j)),
                  pl.BlockSpec((tm, tn), lambda j, i: (i, j))],
        out_specs=pl.BlockSpec((tm, tn), lambda j, i: (i, j)),
        out_shape=jax.ShapeDtypeStruct((m, n), F32),
        compiler_params=pltpu.CompilerParams(
            dimension_semantics=("parallel", "parallel"), vmem_limit_bytes=VMEM_LIMIT),
        name=name,
    )(a, w, h)


DA_BIAS_LANES = 3


def _diff_attn_kernel(slopes_ref, lamp_ref, subln_ref, q_ref, k_ref, v_ref, o_ref,
                      k1_ref, k2_ref, s_ref, st_ref, p_ref, *, tq, lambda_init):
    hd = DA_HEAD_DIM
    seq = q_ref.shape[0]
    nq = seq // tq
    half = tq // 2
    slope = slopes_ref[pl.program_id(1)]
    lp = lamp_ref[...]
    lam = (jnp.exp(jnp.sum(lp[0:1] * lp[1:2], axis=-1, keepdims=True))
           - jnp.exp(jnp.sum(lp[2:3] * lp[3:4], axis=-1, keepdims=True)) + lambda_init)

    lane_k = lax.broadcasted_iota(jnp.int32, (seq, 2 * hd), 1)
    bias = (slope * LOG2E) * lax.broadcasted_iota(jnp.int32, (seq, 2 * hd), 0).astype(F32)
    pieces = []
    rest = bias
    for _ in range(DA_BIAS_LANES):
        piece = rest.astype(BF16).astype(F32)
        pieces.append(piece)
        rest = rest - piece
    kf = k_ref[...].astype(F32)
    k1 = jnp.where(lane_k < hd, kf, 0.0)
    k2 = jnp.where(lane_k >= hd, kf, 0.0)
    for t, piece in enumerate(pieces):
        k1 = jnp.where(lane_k == hd + t, piece, k1)
        k2 = jnp.where(lane_k == t, piece, k2)
    k1_ref[...] = k1.astype(BF16)
    k2_ref[...] = k2.astype(BF16)

    lane_q = lax.broadcasted_iota(jnp.int32, (tq, 2 * hd), 1)
    causal = (lax.broadcasted_iota(jnp.int32, (tq, tq), 0)
              >= lax.broadcasted_iota(jnp.int32, (tq, tq), 1))

    def fold(x):
        return x[:, :half], x[:, half:]

    for qi in range(nq):
        rows = slice(qi * tq, (qi + 1) * tq)
        qf = q_ref[rows, :].astype(F32) * (hd ** -0.5 * LOG2E)
        q1 = jnp.where(lane_q < hd, qf, jnp.where(lane_q < hd + DA_BIAS_LANES, 1.0, 0.0)).astype(BF16)
        q2 = jnp.where(lane_q >= hd, qf, jnp.where(lane_q < DA_BIAS_LANES, 1.0, 0.0)).astype(BF16)

        mx = [None, None]
        for j in range(qi + 1):
            keys = slice(j * tq, (j + 1) * tq)
            for mp, (qm, km_ref) in enumerate(((q1, k1_ref), (q2, k2_ref))):
                s = _dot_nt(qm, km_ref[keys, :])
                if j == qi:
                    s = jnp.where(causal, s, -jnp.inf)
                s_ref[mp, j] = s
                a, b = fold(s)
                ab = jnp.maximum(a, b)
                mx[mp] = ab if mx[mp] is None else jnp.maximum(mx[mp], ab)
        for mp in range(2):
            st_ref[mp] = jnp.broadcast_to(jnp.max(mx[mp], axis=-1, keepdims=True), (tq, half))

        sm = [None, None]
        for j in range(qi + 1):
            for mp in range(2):
                m = st_ref[mp]
                a, b = fold(s_ref[mp, j])
                pa = jnp.exp2(a - m)
                pb = jnp.exp2(b - m)
                p_ref[mp, :, j * tq:(j + 1) * tq] = jnp.concatenate([pa, pb], axis=1).astype(BF16)
                sm[mp] = pa + pb if sm[mp] is None else sm[mp] + (pa + pb)
        l1 = jnp.sum(sm[0], axis=-1, keepdims=True)
        l2 = jnp.sum(sm[1], axis=-1, keepdims=True)

        kv = (qi + 1) * tq
        o = (_dot(p_ref[0, :, :kv], v_ref[:kv, :]) * (1.0 / l1)
             - _dot(p_ref[1, :, :kv], v_ref[:kv, :]) * (lam / l2))
        ms = jnp.mean(o * o, axis=-1, keepdims=True)
        o = o * lax.rsqrt(ms + RMS_EPS) * subln_ref[...] * (1.0 - lambda_init)
        o_ref[rows, :] = o.astype(o_ref.dtype)


def diff_attention(p3, lam_params, subln_w, lambda_init, *, tq=256):
    bsz, seq, _ = p3.shape
    hw = 2 * DA_HEAD_DIM
    nq = seq // tq
    slopes = jnp.asarray([2.0 ** (-8.0 * (h + 1) / DA_HEADS) for h in range(DA_HEADS)], F32)
    qb, kb, vb = P_DA_Q // hw, P_DA_K // hw, P_DA_V // hw
    return pl.pallas_call(
        functools.partial(_diff_attn_kernel, tq=tq, lambda_init=lambda_init),
        grid=(bsz, DA_HEADS),
        in_specs=[pl.BlockSpec(memory_space=pltpu.SMEM),
                  pl.BlockSpec((4, DA_HEAD_DIM), lambda b, h: (0, 0)),
                  pl.BlockSpec((1, hw), lambda b, h: (0, 0)),
                  pl.BlockSpec((None, seq, hw), lambda b, h: (b, 0, qb + h)),
                  pl.BlockSpec((None, seq, hw), lambda b, h: (b, 0, kb + h)),
                  pl.BlockSpec((None, seq, hw), lambda b, h: (b, 0, vb + h))],
        out_specs=pl.BlockSpec((None, seq, hw), lambda b, h: (b, 0, h)),
        out_shape=jax.ShapeDtypeStruct((bsz, seq, DA_WIDTH), BF16),
        scratch_shapes=[pltpu.VMEM((seq, hw), BF16),
                        pltpu.VMEM((seq, hw), BF16),
                        pltpu.VMEM((2, nq, tq, tq), F32),
                        pltpu.VMEM((2, tq, tq // 2), F32),
                        pltpu.VMEM((2, tq, seq), BF16)],
        compiler_params=pltpu.CompilerParams(
            dimension_semantics=("parallel", "parallel"), vmem_limit_bytes=VMEM_LIMIT),
        name="diff_attention",
    )(slopes, lam_params, subln_w.reshape(1, hw), p3, p3, p3)


def _ssd_kernel(x_ref, bc_ref, z_ref, sm_ref, dtb_ref, alog_ref, dskip_ref, nw_ref,
                o_ref, state_ref, y_ref):
    q = SSM_CHUNK
    c = pl.program_id(1)

    @pl.when(c == 0)
    def _():
        state_ref[...] = jnp.zeros_like(state_ref)

    xs_b = x_ref[...]
    xs = xs_b.astype(F32)
    bc = bc_ref[...]

    dt = _softplus(sm_ref[...] + dtb_ref[...])
    da = dt * (-jnp.exp(alog_ref[...]))
    row = lax.broadcasted_iota(jnp.int32, (q, q), 0)
    colm = lax.broadcasted_iota(jnp.int32, (q, q), 1)
    causal = row >= colm
    tri = causal.astype(F32)
    a_cs = _dot_f32(tri, da)
    a_cs_t = a_cs.T
    dt_t = dt.T
    lane = lax.broadcasted_iota(jnp.int32, (1, LANES), 1)

    for g in range(SSM_GROUPS):
        bm_b = bc[:, g * SSM_STATE:(g + 1) * SSM_STATE]
        cm_b = bc[:, (SSM_GROUPS + g) * SSM_STATE:(SSM_GROUPS + g + 1) * SSM_STATE]
        cm = cm_b.astype(F32)
        cb = _dot_nt(cm_b, bm_b)
        bm_t = bm_b.astype(F32).T
        hpg = SSM_HEADS // SSM_GROUPS
        for pr in range(hpg // 2):
            pair = g * (hpg // 2) + pr
            x_pair = xs_b[:, pair * LANES:(pair + 1) * LANES]
            st = state_ref[pair]
            rhs = jnp.concatenate([x_pair, st.astype(BF16)], axis=0)
            ys, sts, cds = [], [], []
            for sub in range(2):
                h = 2 * pair + sub
                acol = a_cs[:, h:h + 1]
                arow = a_cs_t[h:h + 1, :]
                dtrow = dt_t[h:h + 1, :]
                decay = jnp.exp(jnp.where(causal, acol - arow, -jnp.inf))
                sc = (cb * decay * dtrow).astype(BF16)
                c_in = (cm * jnp.exp(acol)).astype(BF16)
                ys.append(_dot(jnp.concatenate([sc, c_in], axis=1), rhs))
                a_last = arow[:, q - 1:q]
                wrow = jnp.exp(a_last - arow) * dtrow
                sts.append(_dot((bm_t * wrow).astype(BF16), x_pair))
                cds.append(jnp.exp(a_last))
            first = lane < SSM_HEAD_DIM
            y_ref[:, pair * LANES:(pair + 1) * LANES] = jnp.where(first, ys[0], ys[1])
            state_ref[pair] = (st * jnp.where(first, cds[0], cds[1])
                               + jnp.where(first, sts[0], sts[1]))

    y = y_ref[...] + xs * dskip_ref[...]
    y = y * _silu(z_ref[...].astype(F32))
    gw = SSM_INNER // SSM_GROUPS
    for g in range(SSM_GROUPS):
        yg = y[:, g * gw:(g + 1) * gw]
        ms = jnp.mean(yg * yg, axis=-1, keepdims=True)
        o_ref[:, g * gw:(g + 1) * gw] = (
            yg * lax.rsqrt(ms + RMS_EPS) * nw_ref[:, g * gw:(g + 1) * gw]).astype(o_ref.dtype)


def _pad_row(v, offset):
    return jnp.zeros((1, LANES), F32).at[0, offset:offset + v.shape[0]].set(v.astype(F32))


def mamba2_ssd(p3, small3, dt_bias, a_log, d_skip, norm_w):
    bsz, seq, _ = p3.shape
    q = SSM_CHUNK
    nc = seq // q
    const = lambda b, c: (0, 0)
    return pl.pallas_call(
        _ssd_kernel,
        grid=(bsz, nc),
        in_specs=[pl.BlockSpec((None, q, SSM_INNER), lambda b, c: (b, c, P_SSM_X // SSM_INNER)),
                  pl.BlockSpec((None, q, SSM_BC), lambda b, c: (b, c, P_SSM_BC // SSM_BC)),
                  pl.BlockSpec((None, q, SSM_INNER), lambda b, c: (b, c, P_SSM_Z // SSM_INNER)),
                  pl.BlockSpec((None, q, LANES), lambda b, c: (b, c, 0)),
                  pl.BlockSpec((1, LANES), const),
                  pl.BlockSpec((1, LANES), const),
                  pl.BlockSpec((1, SSM_INNER), const),
                  pl.BlockSpec((1, SSM_INNER), const)],
        out_specs=pl.BlockSpec((None, q, SSM_INNER), lambda b, c: (b, c, 0)),
        out_shape=jax.ShapeDtypeStruct((bsz, seq, SSM_INNER), BF16),
        scratch_shapes=[pltpu.VMEM((SSM_HEADS // 2, SSM_STATE, LANES), F32),
                        pltpu.VMEM((q, SSM_INNER), F32)],
        compiler_params=pltpu.CompilerParams(
            dimension_semantics=("parallel", "arbitrary"), vmem_limit_bytes=VMEM_LIMIT),
        name="mamba2_ssd",
    )(p3, p3, p3, small3, _pad_row(dt_bias, S_DT),
      _pad_row(a_log, S_DT), jnp.repeat(d_skip.astype(F32), SSM_HEAD_DIM).reshape(1, SSM_INNER),
      norm_w.reshape(1, SSM_INNER))


def _gdn_kernel(act_ref, z_ref, sm_ref, dtb_ref, alog_ref, nw_ref, o_ref, state_ref):
    cs = GDN_CHUNK
    rr = GDN_ROWS
    nch = rr // cs
    d = GDN_HEAD_DIM
    sh = int(math.log2(cs))
    step = pl.program_id(1)

    @pl.when(step == 0)
    def _():
        state_ref[...] = jnp.zeros_like(state_ref)

    nseq = act_ref.shape[0]
    row = lax.broadcasted_iota(jnp.int32, (rr, rr), 0)
    colm = lax.broadcasted_iota(jnp.int32, (rr, rr), 1)
    same_blk = lax.shift_right_logical(row, sh) == lax.shift_right_logical(colm, sh)
    tri_bd = (same_blk & (row >= colm)).astype(F32)
    beta, g_cs, g_t, beta_t = [], [], [], []
    for sq in range(nseq):
        sm = sm_ref[sq]
        beta.append(1.0 / (1.0 + jnp.exp(-sm)))
        gl = -jnp.exp(alog_ref[...]) * _softplus(sm + dtb_ref[...])
        g_cs.append(_dot_f32(tri_bd, gl))
        g_t.append(g_cs[sq].T)
        beta_t.append(beta[sq].T)

    l_idx = lax.broadcasted_iota(jnp.int32, (cs, rr), 0)
    j_idx = lax.broadcasted_iota(jnp.int32, (cs, rr), 1)
    s_idx = jnp.bitwise_and(j_idx, cs - 1)
    blk = lax.shift_right_logical(j_idx, sh)
    incl_cat = l_idx >= s_idx
    strict_cat = l_idx > s_idx
    eye_cat = (l_idx == s_idx).astype(F32)
    blk_row = lax.shift_right_logical(lax.broadcasted_iota(jnp.int32, (1, rr), 1), sh)

    def to_cat(x):
        if x.shape[1] != rr:
            x = jnp.concatenate([x] * (rr // x.shape[1]), axis=1)
        out = x[(nch - 1) * cs:]
        for c in reversed(range(nch - 1)):
            out = jnp.where(blk == c, x[c * cs:(c + 1) * cs], out)
        return out

    def to_bd(x_cat):
        return jnp.where(same_blk, jnp.concatenate([x_cat] * nch, axis=0), jnp.zeros((), x_cat.dtype))

    pairs = [(sq, h) for sq in range(nseq) for h in range(GDN_HEADS)]
    items = range(len(pairs))
    qn, kn_b, v_b, kn_t, attn_cat, p_cat, t_cat, grow, eg_full = ([None] * len(pairs) for _ in range(9))
    for i in items:
        sq, h = pairs[i]
        qh = act_ref[sq, :, h * d:(h + 1) * d].astype(F32)
        kh = act_ref[sq, :, GDN_WIDTH + h * d:GDN_WIDTH + (h + 1) * d].astype(F32)
        qn[i] = (qh * lax.rsqrt(jnp.sum(qh * qh, axis=-1, keepdims=True) + L2_EPS) * (d ** -0.5)).astype(BF16)
        kn = kh * lax.rsqrt(jnp.sum(kh * kh, axis=-1, keepdims=True) + L2_EPS)
        kn_b[i] = kn.astype(BF16)
        kn_t[i] = kn.T
        v_b[i] = act_ref[sq, :, 2 * GDN_WIDTH + h * d:2 * GDN_WIDTH + (h + 1) * d]
        gcol_full = jnp.broadcast_to(g_cs[sq][:, S_DECAY + h:S_DECAY + h + 1], (rr, LANES))
        bcol_full = jnp.broadcast_to(beta[sq][:, S_BETA + h:S_BETA + h + 1], (rr, LANES))
        eg_full[i] = jnp.exp(gcol_full)
        grow[i] = g_t[sq][S_DECAY + h:S_DECAY + h + 1, :]
        dec = jnp.exp(jnp.where(incl_cat, to_cat(gcol_full) - grow[i], 0.0))
        kk = to_cat(_dot_nt(kn_b[i], kn_b[i]))
        qk = to_cat(_dot_nt(qn[i], kn_b[i]))
        attn_cat[i] = jnp.where(incl_cat, qk * dec, 0.0)
        p_cat[i] = jnp.where(strict_cat, -(kk * dec * to_cat(bcol_full)), 0.0)
        t_cat[i] = eye_cat + p_cat[i]

    p_bd = [to_bd(p_cat[i].astype(BF16)) for i in items]
    for _ in range(sh - 1):
        for i in items:
            p_cat[i] = _dot(p_cat[i].astype(BF16), p_bd[i])
        for i in items:
            p_bd[i] = to_bd(p_cat[i].astype(BF16))
        for i in items:
            t_cat[i] = t_cat[i] + _dot(t_cat[i].astype(BF16), p_bd[i])

    u, w = [None] * len(pairs), [None] * len(pairs)
    for i in items:
        sq, h = pairs[i]
        brow = beta_t[sq][S_BETA + h:S_BETA + h + 1, :]
        u[i] = _dot(to_bd((t_cat[i] * brow).astype(BF16)), v_b[i])
        w[i] = _dot(to_bd((t_cat[i] * (brow * jnp.exp(grow[i]))).astype(BF16)), kn_b[i])

    st = [state_ref[i] for i in items]
    zero_blk = jnp.zeros((cs, d), BF16)
    for c in range(nch):
        rows = slice(c * cs, (c + 1) * cs)
        for i in items:
            sq, h = pairs[i]
            ws = _dot(jnp.concatenate([w[i][rows].astype(BF16), qn[i][rows]], axis=0), st[i].astype(BF16))
            v_new = (u[i][rows] - ws[:cs]).astype(BF16)
            rhs = jnp.concatenate([zero_blk] * c + [v_new] + [zero_blk] * (nch - 1 - c), axis=0)
            g_last = grow[i][:, (c + 1) * cs - 1:(c + 1) * cs]
            e_row = jnp.exp(jnp.where(blk_row == c, g_last - grow[i], 0.0))
            lhs = jnp.concatenate(
                [jnp.where(blk == c, attn_cat[i], 0.0),
                 jnp.where(blk_row == c, kn_t[i] * e_row, 0.0)], axis=0).astype(BF16)
            r = _dot(lhs, rhs)
            o = eg_full[i][rows] * ws[cs:] + r[:cs]
            st[i] = st[i] * jnp.exp(g_last) + r[cs:]
            ms = jnp.mean(o * o, axis=-1, keepdims=True)
            zh = z_ref[sq, rows, h * d:(h + 1) * d].astype(F32)
            o_ref[sq, rows, h * d:(h + 1) * d] = (
                o * lax.rsqrt(ms + RMS_EPS) * nw_ref[...] * _silu(zh)).astype(o_ref.dtype)
    for i in items:
        state_ref[i] = st[i]


def gated_deltanet(p3, small3, dt_bias, a_log, norm_w):
    bsz, seq, _ = p3.shape
    rr = GDN_ROWS
    ns = GDN_SEQS if bsz % GDN_SEQS == 0 else 1
    w3 = 3 * GDN_WIDTH
    const = lambda b, c: (0, 0)
    return pl.pallas_call(
        _gdn_kernel,
        grid=(bsz // ns, seq // rr),
        in_specs=[pl.BlockSpec((ns, rr, w3), lambda b, c: (b, c, P_GDN_QKV // w3)),
                  pl.BlockSpec((ns, rr, GDN_WIDTH), lambda b, c: (b, c, P_GDN_Z // GDN_WIDTH)),
                  pl.BlockSpec((ns, rr, LANES), lambda b, c: (b, c, 0)),
                  pl.BlockSpec((1, LANES), const),
                  pl.BlockSpec((1, LANES), const),
                  pl.BlockSpec((1, GDN_HEAD_DIM), const)],
        out_specs=pl.BlockSpec((ns, rr, GDN_WIDTH), lambda b, c: (b, c, 0)),
        out_shape=jax.ShapeDtypeStruct((bsz, seq, GDN_WIDTH), BF16),
        scratch_shapes=[pltpu.VMEM((ns * GDN_HEADS, GDN_HEAD_DIM, GDN_HEAD_DIM), F32)],
        compiler_params=pltpu.CompilerParams(
            dimension_semantics=("parallel", "arbitrary"), vmem_limit_bytes=VMEM_LIMIT),
        name="gated_deltanet",
    )(p3, p3, small3, _pad_row(dt_bias, S_DECAY), _pad_row(a_log, S_DECAY),
      norm_w.reshape(1, GDN_HEAD_DIM))


def _merge_kernel(oa_ref, os_ref, og_ref, wb_ref, g0_ref, g1_ref, g2_ref, o_ref):
    acc = None
    for br, (x_ref, g_ref) in enumerate(((oa_ref, g0_ref), (os_ref, g1_ref), (og_ref, g2_ref))):
        gate = 1.0 / (1.0 + jnp.exp(-g_ref[...].astype(F32)))
        term = gate * _dot(x_ref[...], wb_ref[br].astype(BF16))
        acc = term if acc is None else acc + term
    o_ref[...] = acc.astype(o_ref.dtype)


def merge_branches(o_da, o_ssm, o_gdn, wb3, p, *, tm=1024, tn=512):
    m = o_da.shape[0]
    gb = P_GATES // tn
    nb = D_MODEL // tn
    act = pl.BlockSpec((tm, DA_WIDTH), lambda i, j: (i, 0))
    return pl.pallas_call(
        _merge_kernel,
        grid=(m // tm, nb),
        in_specs=[act, act, act,
                  pl.BlockSpec((N_BRANCH, DA_WIDTH, tn), lambda i, j: (0, 0, j)),
                  pl.BlockSpec((tm, tn), lambda i, j: (i, gb + j)),
                  pl.BlockSpec((tm, tn), lambda i, j: (i, gb + nb + j)),
                  pl.BlockSpec((tm, tn), lambda i, j: (i, gb + 2 * nb + j))],
        out_specs=pl.BlockSpec((tm, tn), lambda i, j: (i, j)),
        out_shape=jax.ShapeDtypeStruct((m, D_MODEL), BF16),
        compiler_params=pltpu.CompilerParams(
            dimension_semantics=("parallel", "arbitrary"), vmem_limit_bytes=VMEM_LIMIT),
        name="merge_branches",
    )(o_da, o_ssm, o_gdn, wb3, p, p, p)


def _ffn_up_kernel(x_ref, wg_ref, wv_ref, cwg_ref, cwv_ref, cbg_ref, cbv_ref, o_ref,
                   ug0_ref, ug1_ref, uv0_ref, uv1_ref, wgb_ref, wvb_ref, *, rows):
    seq = x_ref.shape[0]
    nblk = seq // rows
    ug = (ug0_ref, ug1_ref)
    uv = (uv0_ref, uv1_ref)
    ug0_ref[0:HALO, :] = jnp.zeros((HALO, ug0_ref.shape[1]), F32)
    uv0_ref[0:HALO, :] = jnp.zeros((HALO, uv0_ref.shape[1]), F32)
    wgb_ref[...] = wg_ref[...].astype(BF16)
    wvb_ref[...] = wv_ref[...].astype(BF16)

    def project(b):
        xr = x_ref[b * rows:(b + 1) * rows, :]
        ug[b % 2][HALO:HALO + rows, :] = _dot(xr, wgb_ref[...])
        uv[b % 2][HALO:HALO + rows, :] = _dot(xr, wvb_ref[...])

    def conv(buf, cw_ref, cb_ref):
        acc = cb_ref[...] + cw_ref[0:1, :] * buf[HALO - 2:HALO - 2 + rows, :]
        for t in range(1, FFN_CONV):
            acc = acc + cw_ref[t:t + 1, :] * buf[HALO - 2 + t:HALO - 2 + t + rows, :]
        return acc

    def activate(b):
        gate = conv(ug[b % 2], cwg_ref, cbg_ref)
        val = conv(uv[b % 2], cwv_ref, cbv_ref)
        o_ref[b * rows:(b + 1) * rows, :] = (_silu(gate) * val).astype(o_ref.dtype)

    project(0)
    for b in range(1, nblk):
        ug[b % 2][0:HALO, :] = ug[(b - 1) % 2][rows:rows + HALO, :]
        uv[b % 2][0:HALO, :] = uv[(b - 1) % 2][rows:rows + HALO, :]
        project(b)
        activate(b - 1)
    activate(nblk - 1)


def ffn_up_proj(hn3, w_up, conv_w, conv_b, *, tf=512, rows=256):
    bsz, seq, d = hn3.shape
    nf = D_FF // tf
    cb = conv_b.reshape(1, 2 * D_FF)
    return pl.pallas_call(
        functools.partial(_ffn_up_kernel, rows=rows),
        grid=(bsz, nf),
        in_specs=[pl.BlockSpec((None, seq, d), lambda b, f: (b, 0, 0)),
                  pl.BlockSpec((d, tf), lambda b, f: (0, f)),
                  pl.BlockSpec((d, tf), lambda b, f: (0, nf + f)),
                  pl.BlockSpec((FFN_CONV, tf), lambda b, f: (0, f)),
                  pl.BlockSpec((FFN_CONV, tf), lambda b, f: (0, nf + f)),
                  pl.BlockSpec((1, tf), lambda b, f: (0, f)),
                  pl.BlockSpec((1, tf), lambda b, f: (0, nf + f))],
        out_specs=pl.BlockSpec((None, seq, tf), lambda b, f: (b, 0, f)),
        out_shape=jax.ShapeDtypeStruct((bsz, seq, D_FF), BF16),
        scratch_shapes=[pltpu.VMEM((HALO + rows, tf), F32)] * 4
                       + [pltpu.VMEM((d, tf), BF16), pltpu.VMEM((d, tf), BF16)],
        compiler_params=pltpu.CompilerParams(
            dimension_semantics=("parallel", "arbitrary"), vmem_limit_bytes=VMEM_LIMIT),
        name="ffn_up",
    )(hn3, w_up, w_up, conv_w, conv_w, cb, cb)


def _reorder_w_in(w):
    sizes = (DA_WIDTH, DA_WIDTH, DA_WIDTH, SSM_INNER, SSM_INNER + SSM_BC, SSM_HEADS,
             3 * GDN_WIDTH, GDN_WIDTH, GDN_HEADS, GDN_HEADS, N_BRANCH * D_MODEL)
    offs = [0]
    for s in sizes:
        offs.append(offs[-1] + s)
    seg = lambda i: w[:, offs[i]:offs[i + 1]]
    da_q, da_k, da_v, ssm_z, ssm_xbc, ssm_dt, gdn_qkv, gdn_z, gdn_b, gdn_a, gates = (
        seg(i) for i in range(len(sizes)))
    big = jnp.concatenate([gdn_qkv, ssm_z, gdn_z, da_q, da_k, da_v, gates, ssm_xbc], axis=1)
    small = jnp.concatenate(
        [ssm_dt, gdn_b, gdn_a,
         jnp.zeros((w.shape[0], LANES - SSM_HEADS - 2 * GDN_HEADS), w.dtype)], axis=1)
    return big.astype(BF16), small.astype(BF16)


def kernel(x, norm_mix, w_in, da_lambda, da_subln, ssm_conv_w, ssm_conv_b, ssm_dt_bias, ssm_a_log,
           ssm_d, ssm_norm, gdn_conv_w, gdn_dt_bias, gdn_a_log, gdn_norm, w_branch, w_out, norm_ffn,
           ffn_up, ffn_conv_w, ffn_conv_b, ffn_down, norm_final):
    bsz, seq, d = x.shape
    m = bsz * seq
    h = x.reshape(m, d)
    mid = P_SSM_X - 3 * GDN_WIDTH
    for l in range(DEPTH):
        w_big, w_small = _reorder_w_in(w_in[l])
        conv_w = jnp.concatenate([gdn_conv_w[l], jnp.zeros((SSM_CONV, mid), F32), ssm_conv_w[l]], axis=1)
        conv_b = jnp.concatenate([jnp.zeros((P_SSM_X,), F32), ssm_conv_b[l]]).reshape(1, P_COLS)
        p, small = in_proj(h, norm_mix[l], w_big, w_small, conv_w, conv_b, seq=seq)
        p3 = p.reshape(bsz, seq, P_COLS)
        small3 = small.reshape(bsz, seq, LANES)
        lambda_init = 0.8 - 0.6 * math.exp(-0.3 * l)
        o_da = diff_attention(p3, da_lambda[l], da_subln[l], lambda_init)
        o_ssm = mamba2_ssd(p3, small3, ssm_dt_bias[l], ssm_a_log[l], ssm_d[l], ssm_norm[l])
        o_gdn = gated_deltanet(p3, small3, gdn_dt_bias[l], gdn_a_log[l], gdn_norm[l])
        wb3 = w_branch[l].reshape(N_BRANCH, DA_WIDTH, D_MODEL)
        merged = merge_branches(o_da.reshape(m, DA_WIDTH), o_ssm.reshape(m, SSM_INNER),
                                o_gdn.reshape(m, GDN_WIDTH), wb3, p)
        h, hn = matmul_res_norm(merged, w_out[l].astype(BF16), h, norm_ffn[l], tm=512, name="out_proj")
        act = ffn_up_proj(hn.reshape(bsz, seq, d), ffn_up[l], ffn_conv_w[l], ffn_conv_b[l])
        h = matmul_res(act.reshape(m, D_FF), ffn_down[l].astype(BF16), h, tm=512, tn=1024, name="ffn_down")
    return rmsnorm(h, norm_final, out_dtype=F32).reshape(bsz, seq, d)
```

```python
import functools
import math

import jax
import jax.numpy as jnp
from jax import lax
from jax.experimental import pallas as pl
from jax.experimental.pallas import tpu as pltpu

F32 = jnp.float32
BF16 = jnp.bfloat16

D_MODEL = 2048
DEPTH = 2
DA_HEADS = 8
DA_HEAD_DIM = 64
DA_WIDTH = DA_HEADS * 2 * DA_HEAD_DIM
SSM_HEADS = 16
SSM_HEAD_DIM = 64
SSM_INNER = SSM_HEADS * SSM_HEAD_DIM
SSM_GROUPS = 2
SSM_STATE = 128
SSM_CONV = 4
SSM_CHUNK = 128
SSM_BC = 2 * SSM_GROUPS * SSM_STATE
GDN_HEADS = 8
GDN_HEAD_DIM = 128
GDN_WIDTH = GDN_HEADS * GDN_HEAD_DIM
GDN_CONV = 4
GDN_CHUNK = 64
GDN_ROWS = 256
GDN_SEQS = 1
D_FF = 5632
FFN_CONV = 3
N_BRANCH = 3
RMS_EPS = 1e-6
L2_EPS = 1e-6
LOG2E = math.log2(math.e)

LANES = 128
HALO = 8
VMEM_LIMIT = 56 * 1024 * 1024

P_GDN_QKV = 0
P_SSM_Z = P_GDN_QKV + 3 * GDN_WIDTH
P_GDN_Z = P_SSM_Z + SSM_INNER
P_DA_Q = P_GDN_Z + GDN_WIDTH
P_DA_K = P_DA_Q + DA_WIDTH
P_DA_V = P_DA_K + DA_WIDTH
P_GATES = P_DA_V + DA_WIDTH
P_SSM_X = P_GATES + N_BRANCH * D_MODEL
P_SSM_BC = P_SSM_X + SSM_INNER
P_COLS = P_SSM_BC + SSM_BC
S_DT = 0
S_BETA = SSM_HEADS
S_DECAY = SSM_HEADS + GDN_HEADS


def _silu(x):
    return x / (1.0 + jnp.exp(-x))


def _softplus(x):
    return jnp.maximum(x, 0.0) + jnp.log1p(jnp.exp(-jnp.abs(x)))


def _dot(a, b):
    return jnp.dot(a, b, preferred_element_type=F32)


def _dot_nt(a, b):
    return lax.dot_general(a, b, (((1,), (1,)), ((), ())), preferred_element_type=F32)


def _dot_f32(a, b):
    return jnp.dot(a, b, preferred_element_type=F32, precision=lax.Precision.HIGHEST)


def _rmsnorm_kernel(x_ref, w_ref, o_ref):
    x = x_ref[...]
    ms = jnp.mean(x * x, axis=-1, keepdims=True)
    o_ref[...] = (x * lax.rsqrt(ms + RMS_EPS) * w_ref[...]).astype(o_ref.dtype)


def rmsnorm(x, w, *, tm=512, out_dtype=BF16):
    m, d = x.shape
    return pl.pallas_call(
        _rmsnorm_kernel,
        grid=(m // tm,),
        in_specs=[pl.BlockSpec((tm, d), lambda i: (i, 0)),
                  pl.BlockSpec((1, d), lambda i: (0, 0))],
        out_specs=pl.BlockSpec((tm, d), lambda i: (i, 0)),
        out_shape=jax.ShapeDtypeStruct((m, d), out_dtype),
        name="rmsnorm",
    )(x, w.reshape(1, d))


def _cast_kernel(x_ref, o_ref):
    o_ref[...] = x_ref[...].astype(o_ref.dtype)


def cast_bf16(w, *, tm=512):
    r, c = w.shape
    return pl.pallas_call(
        _cast_kernel,
        grid=(r // tm,),
        in_specs=[pl.BlockSpec((tm, c), lambda i: (i, 0))],
        out_specs=pl.BlockSpec((tm, c), lambda i: (i, 0)),
        out_shape=jax.ShapeDtypeStruct((r, c), BF16),
        name="cast_bf16",
    )(w)


IN_TM = 1024
IN_TN = 512
IN_ROWS = 256


def _in_proj_kernel(x_ref, nw_ref, w_ref, ws_ref, cwg_ref, cws_ref, cbs_ref, o_ref, small_ref,
                    xn_ref, u_ref, u2_ref, halo_ref, *, conv_front, conv_back, tiles_per_seq):
    i = pl.program_id(0)
    j = pl.program_id(1)
    tm = x_ref.shape[0]

    @pl.when(j == 0)
    def _():
        x = x_ref[...]
        ms = jnp.mean(x * x, axis=-1, keepdims=True)
        xn_ref[...] = (x * lax.rsqrt(ms + RMS_EPS) * nw_ref[...]).astype(xn_ref.dtype)
        small_ref[...] = _dot(xn_ref[...], ws_ref[...])

    is_conv = (j < conv_front) | (j >= conv_back)

    @pl.when(jnp.logical_not(is_conv))
    def _():
        o_ref[...] = _dot(xn_ref[...], w_ref[...]).astype(o_ref.dtype)

    @pl.when(is_conv)
    def _():
        first = lax.rem(i, tiles_per_seq) == 0

        @pl.when(first)
        def _():
            u_ref[0:HALO, :] = jnp.zeros((HALO, u_ref.shape[1]), F32)

        @pl.when(jnp.logical_not(first))
        def _():
            u_ref[0:HALO, :] = halo_ref[j]

        bufs = (u_ref, u2_ref)
        nblk = tm // IN_ROWS
        front = j < conv_front
        cw = jnp.where(front, cwg_ref[...], cws_ref[...])
        cb = jnp.where(front, 0.0, cbs_ref[...])

        def project(b):
            bufs[b % 2][HALO:HALO + IN_ROWS, :] = _dot(xn_ref[b * IN_ROWS:(b + 1) * IN_ROWS, :], w_ref[...])

        def activate(b):
            buf = bufs[b % 2]
            acc = cb + cw[0:1, :] * buf[HALO - 3:HALO - 3 + IN_ROWS, :]
            for t in range(1, SSM_CONV):
                acc = acc + cw[t:t + 1, :] * buf[HALO - 3 + t:HALO - 3 + t + IN_ROWS, :]
            o_ref[b * IN_ROWS:(b + 1) * IN_ROWS, :] = _silu(acc).astype(o_ref.dtype)

        project(0)
        for b in range(1, nblk):
            bufs[b % 2][0:HALO, :] = bufs[(b - 1) % 2][IN_ROWS:IN_ROWS + HALO, :]
            project(b)
            activate(b - 1)
        activate(nblk - 1)
        halo_ref[j] = bufs[(nblk - 1) % 2][IN_ROWS:IN_ROWS + HALO, :]


def in_proj(x, nw, w, w_small, gdn_cw, ssm_cw, ssm_cb, *, seq):
    m, k = x.shape
    assert GDN_CONV == SSM_CONV and P_GDN_QKV == 0 and P_SSM_X % IN_TN == 0 and (3 * GDN_WIDTH) % IN_TN == 0
    n_tiles = P_COLS // IN_TN
    front = 3 * GDN_WIDTH // IN_TN
    back = P_SSM_X // IN_TN
    gdn_tile = lambda i, j: (0, jnp.minimum(j, front - 1))
    ssm_tile = lambda i, j: (0, jnp.clip(j - back, 0, n_tiles - back - 1))
    return pl.pallas_call(
        functools.partial(_in_proj_kernel, conv_front=front, conv_back=back, tiles_per_seq=seq // IN_TM),
        grid=(m // IN_TM, n_tiles),
        in_specs=[pl.BlockSpec((IN_TM, k), lambda i, j: (i, 0)),
                  pl.BlockSpec((1, k), lambda i, j: (0, 0)),
                  pl.BlockSpec((k, IN_TN), lambda i, j: (0, j)),
                  pl.BlockSpec((k, LANES), lambda i, j: (0, 0)),
                  pl.BlockSpec((GDN_CONV, IN_TN), gdn_tile),
                  pl.BlockSpec((SSM_CONV, IN_TN), ssm_tile),
                  pl.BlockSpec((1, IN_TN), ssm_tile)],
        out_specs=[pl.BlockSpec((IN_TM, IN_TN), lambda i, j: (i, j)),
                   pl.BlockSpec((IN_TM, LANES), lambda i, j: (i, 0))],
        out_shape=[jax.ShapeDtypeStruct((m, P_COLS), BF16), jax.ShapeDtypeStruct((m, LANES), F32)],
        scratch_shapes=[pltpu.VMEM((IN_TM, k), BF16),
                        pltpu.VMEM((HALO + IN_ROWS, IN_TN), F32),
                        pltpu.VMEM((HALO + IN_ROWS, IN_TN), F32),
                        pltpu.VMEM((n_tiles, HALO, IN_TN), F32)],
        compiler_params=pltpu.CompilerParams(
            dimension_semantics=("arbitrary", "arbitrary"), vmem_limit_bytes=VMEM_LIMIT),
        name="in_proj",
    )(x, nw.reshape(1, k), w, w_small, gdn_cw, ssm_cw, ssm_cb.reshape(1, -1))


def _matmul_res_norm_kernel(a_ref, w_ref, h_ref, nw_ref, ho_ref, no_ref):
    h = h_ref[...] + _dot(a_ref[...], w_ref[...])
    ho_ref[...] = h
    ms = jnp.mean(h * h, axis=-1, keepdims=True)
    no_ref[...] = (h * lax.rsqrt(ms + RMS_EPS) * nw_ref[...]).astype(no_ref.dtype)


def matmul_res_norm(a, w, h, nw, *, tm, name):
    m, k = a.shape
    _, n = w.shape
    row = lambda i: (i, 0)
    return pl.pallas_call(
        _matmul_res_norm_kernel,
        grid=(m // tm,),
        in_specs=[pl.BlockSpec((tm, k), row),
                  pl.BlockSpec((k, n), lambda i: (0, 0)),
                  pl.BlockSpec((tm, n), row),
                  pl.BlockSpec((1, n), lambda i: (0, 0))],
        out_specs=[pl.BlockSpec((tm, n), row), pl.BlockSpec((tm, n), row)],
        out_shape=[jax.ShapeDtypeStruct((m, n), F32), jax.ShapeDtypeStruct((m, n), BF16)],
        compiler_params=pltpu.CompilerParams(
            dimension_semantics=("parallel",), vmem_limit_bytes=VMEM_LIMIT),
        name=name,
    )(a, w, h, nw.reshape(1, n))


def _matmul_res_kernel(a_ref, w_ref, h_ref, o_ref):
    o_ref[...] = h_ref[...] + _dot(a_ref[...], w_ref[...])


def matmul_res(a, w, h, *, tm, tn, name):
    m, k = a.shape
    _, n = w.shape
    return pl.pallas_call(
        _matmul_res_kernel,
        grid=(n // tn, m // tm),
        in_specs=[pl.BlockSpec((tm, k), lambda j, i: (i, 0)),
                  pl.BlockSpec((k, tn), lambda j, i: (0, j)),
                  pl.BlockSpec((tm, tn), lambda j, i: (i, j))],
        out_specs=pl.BlockSpec((tm, tn), lambda j, i: (i, j)),
        out_shape=jax.ShapeDtypeStruct((m, n), F32),
        compiler_params=pltpu.CompilerParams(
            dimension_semantics=("parallel", "parallel"), vmem_limit_bytes=VMEM_LIMIT),
        name=name,
    )(a, w, h)


DA_BIAS_LANES = 3


def _diff_attn_kernel(slopes_ref, lamp_ref, subln_ref, q_ref, k_ref, v_ref, o_ref,
                      k1_ref, k2_ref, s_ref, st_ref, p_ref, *, tq, lambda_init):
    hd = DA_HEAD_DIM
    seq = q_ref.shape[0]
    nq = seq // tq
    half = tq // 2
    slope = slopes_ref[pl.program_id(1)]
    lp = lamp_ref[...]
    lam = (jnp.exp(jnp.sum(lp[0:1] * lp[1:2], axis=-1, keepdims=True))
           - jnp.exp(jnp.sum(lp[2:3] * lp[3:4], axis=-1, keepdims=True)) + lambda_init)

    lane_k = lax.broadcasted_iota(jnp.int32, (seq, 2 * hd), 1)
    bias = (slope * LOG2E) * lax.broadcasted_iota(jnp.int32, (seq, 2 * hd), 0).astype(F32)
    pieces = []
    rest = bias
    for _ in range(DA_BIAS_LANES):
        piece = rest.astype(BF16).astype(F32)
        pieces.append(piece)
        rest = rest - piece
    kf = k_ref[...].astype(F32)
    k1 = jnp.where(lane_k < hd, kf, 0.0)
    k2 = jnp.where(lane_k >= hd, kf, 0.0)
    for t, piece in enumerate(pieces):
        k1 = jnp.where(lane_k == hd + t, piece, k1)
        k2 = jnp.where(lane_k == t, piece, k2)
    k1_ref[...] = k1.astype(BF16)
    k2_ref[...] = k2.astype(BF16)

    lane_q = lax.broadcasted_iota(jnp.int32, (tq, 2 * hd), 1)
    causal = (lax.broadcasted_iota(jnp.int32, (tq, tq), 0)
              >= lax.broadcasted_iota(jnp.int32, (tq, tq), 1))

    def fold(x):
        return x[:, :half], x[:, half:]

    for qi in range(nq):
        rows = slice(qi * tq, (qi + 1) * tq)
        qf = q_ref[rows, :].astype(F32) * (hd ** -0.5 * LOG2E)
        q1 = jnp.where(lane_q < hd, qf, jnp.where(lane_q < hd + DA_BIAS_LANES, 1.0, 0.0)).astype(BF16)
        q2 = jnp.where(lane_q >= hd, qf, jnp.where(lane_q < DA_BIAS_LANES, 1.0, 0.0)).astype(BF16)

        mx = [None, None]
        for j in range(qi + 1):
            keys = slice(j * tq, (j + 1) * tq)
            for mp, (qm, km_ref) in enumerate(((q1, k1_ref), (q2, k2_ref))):
                s = _dot_nt(qm, km_ref[keys, :])
                if j == qi:
                    s = jnp.where(causal, s, -jnp.inf)
                s_ref[mp, j] = s
                a, b = fold(s)
                ab = jnp.maximum(a, b)
                mx[mp] = ab if mx[mp] is None else jnp.maximum(mx[mp], ab)
        for mp in range(2):
            st_ref[mp] = jnp.broadcast_to(jnp.max(mx[mp], axis=-1, keepdims=True), (tq, half))

        sm = [None, None]
        for j in range(qi + 1):
            for mp in range(2):
                m = st_ref[mp]
                a, b = fold(s_ref[mp, j])
                pa = jnp.exp2(a - m)
                pb = jnp.exp2(b - m)
                p_ref[mp, :, j * tq:(j + 1) * tq] = jnp.concatenate([pa, pb], axis=1).astype(BF16)
                sm[mp] = pa + pb if sm[mp] is None else sm[mp] + (pa + pb)
        l1 = jnp.sum(sm[0], axis=-1, keepdims=True)
        l2 = jnp.sum(sm[1], axis=-1, keepdims=True)

        kv = (qi + 1) * tq
        o = (_dot(p_ref[0, :, :kv], v_ref[:kv, :]) * (1.0 / l1)
             - _dot(p_ref[1, :, :kv], v_ref[:kv, :]) * (lam / l2))
        ms = jnp.mean(o * o, axis=-1, keepdims=True)
        o = o * lax.rsqrt(ms + RMS_EPS) * subln_ref[...] * (1.0 - lambda_init)
        o_ref[rows, :] = o.astype(o_ref.dtype)


def diff_attention(p3, lam_params, subln_w, lambda_init, *, tq=256):
    bsz, seq, _ = p3.shape
    hw = 2 * DA_HEAD_DIM
    nq = seq // tq
    slopes = jnp.asarray([2.0 ** (-8.0 * (h + 1) / DA_HEADS) for h in range(DA_HEADS)], F32)
    qb, kb, vb = P_DA_Q // hw, P_DA_K // hw, P_DA_V // hw
    return pl.pallas_call(
        functools.partial(_diff_attn_kernel, tq=tq, lambda_init=lambda_init),
        grid=(bsz, DA_HEADS),
        in_specs=[pl.BlockSpec(memory_space=pltpu.SMEM),
                  pl.BlockSpec((4, DA_HEAD_DIM), lambda b, h: (0, 0)),
                  pl.BlockSpec((1, hw), lambda b, h: (0, 0)),
                  pl.BlockSpec((None, seq, hw), lambda b, h: (b, 0, qb + h)),
                  pl.BlockSpec((None, seq, hw), lambda b, h: (b, 0, kb + h)),
                  pl.BlockSpec((None, seq, hw), lambda b, h: (b, 0, vb + h))],
        out_specs=pl.BlockSpec((None, seq, hw), lambda b, h: (b, 0, h)),
        out_shape=jax.ShapeDtypeStruct((bsz, seq, DA_WIDTH), BF16),
        scratch_shapes=[pltpu.VMEM((seq, hw), BF16),
                        pltpu.VMEM((seq, hw), BF16),
                        pltpu.VMEM((2, nq, tq, tq), F32),
                        pltpu.VMEM((2, tq, tq // 2), F32),
                        pltpu.VMEM((2, tq, seq), BF16)],
        compiler_params=pltpu.CompilerParams(
            dimension_semantics=("parallel", "parallel"), vmem_limit_bytes=VMEM_LIMIT),
        name="diff_attention",
    )(slopes, lam_params, subln_w.reshape(1, hw), p3, p3, p3)


def _ssd_kernel(x_ref, bc_ref, z_ref, sm_ref, dtb_ref, alog_ref, dskip_ref, nw_ref,
                o_ref, state_ref, y_ref):
    q = SSM_CHUNK
    c = pl.program_id(1)

    @pl.when(c == 0)
    def _():
        state_ref[...] = jnp.zeros_like(state_ref)

    xs_b = x_ref[...]
    xs = xs_b.astype(F32)
    bc = bc_ref[...]

    dt = _softplus(sm_ref[...] + dtb_ref[...])
    da = dt * (-jnp.exp(alog_ref[...]))
    row = lax.broadcasted_iota(jnp.int32, (q, q), 0)
    colm = lax.broadcasted_iota(jnp.int32, (q, q), 1)
    causal = row >= colm
    tri = causal.astype(F32)
    a_cs = _dot_f32(tri, da)
    a_cs_t = a_cs.T
    dt_t = dt.T
    lane = lax.broadcasted_iota(jnp.int32, (1, LANES), 1)

    for g in range(SSM_GROUPS):
        bm_b = bc[:, g * SSM_STATE:(g + 1) * SSM_STATE]
        cm_b = bc[:, (SSM_GROUPS + g) * SSM_STATE:(SSM_GROUPS + g + 1) * SSM_STATE]
        cm = cm_b.astype(F32)
        cb = _dot_nt(cm_b, bm_b)
        bm_t = bm_b.astype(F32).T
        hpg = SSM_HEADS // SSM_GROUPS
        for pr in range(hpg // 2):
            pair = g * (hpg // 2) + pr
            x_pair = xs_b[:, pair * LANES:(pair + 1) * LANES]
            st = state_ref[pair]
            rhs = jnp.concatenate([x_pair, st.astype(BF16)], axis=0)
            ys, sts, cds = [], [], []
            for sub in range(2):
                h = 2 * pair + sub
                acol = a_cs[:, h:h + 1]
                arow = a_cs_t[h:h + 1, :]
                dtrow = dt_t[h:h + 1, :]
                decay = jnp.exp(jnp.where(causal, acol - arow, -jnp.inf))
                sc = (cb * decay * dtrow).astype(BF16)
                c_in = (cm * jnp.exp(acol)).astype(BF16)
                ys.append(_dot(jnp.concatenate([sc, c_in], axis=1), rhs))
                a_last = arow[:, q - 1:q]
                wrow = jnp.exp(a_last - arow) * dtrow
                sts.append(_dot((bm_t * wrow).astype(BF16), x_pair))
                cds.append(jnp.exp(a_last))
            first = lane < SSM_HEAD_DIM
            y_ref[:, pair * LANES:(pair + 1) * LANES] = jnp.where(first, ys[0], ys[1])
            state_ref[pair] = (st * jnp.where(first, cds[0], cds[1])
                               + jnp.where(first, sts[0], sts[1]))

    y = y_ref[...] + xs * dskip_ref[...]
    y = y * _silu(z_ref[...].astype(F32))
    gw = SSM_INNER // SSM_GROUPS
    for g in range(SSM_GROUPS):
        yg = y[:, g * gw:(g + 1) * gw]
        ms = jnp.mean(yg * yg, axis=-1, keepdims=True)
        o_ref[:, g * gw:(g + 1) * gw] = (
            yg * lax.rsqrt(ms + RMS_EPS) * nw_ref[:, g * gw:(g + 1) * gw]).astype(o_ref.dtype)


def _pad_row(v, offset):
    return jnp.zeros((1, LANES), F32).at[0, offset:offset + v.shape[0]].set(v.astype(F32))


def mamba2_ssd(p3, small3, dt_bias, a_log, d_skip, norm_w):
    bsz, seq, _ = p3.shape
    q = SSM_CHUNK
    nc = seq // q
    const = lambda b, c: (0, 0)
    return pl.pallas_call(
        _ssd_kernel,
        grid=(bsz, nc),
        in_specs=[pl.BlockSpec((None, q, SSM_INNER), lambda b, c: (b, c, P_SSM_X // SSM_INNER)),
                  pl.BlockSpec((None, q, SSM_BC), lambda b, c: (b, c, P_SSM_BC // SSM_BC)),
                  pl.BlockSpec((None, q, SSM_INNER), lambda b, c: (b, c, P_SSM_Z // SSM_INNER)),
                  pl.BlockSpec((None, q, LANES), lambda b, c: (b, c, 0)),
                  pl.BlockSpec((1, LANES), const),
                  pl.BlockSpec((1, LANES), const),
                  pl.BlockSpec((1, SSM_INNER), const),
                  pl.BlockSpec((1, SSM_INNER), const)],
        out_specs=pl.BlockSpec((None, q, SSM_INNER), lambda b, c: (b, c, 0)),
        out_shape=jax.ShapeDtypeStruct((bsz, seq, SSM_INNER), BF16),
        scratch_shapes=[pltpu.VMEM((SSM_HEADS // 2, SSM_STATE, LANES), F32),
                        pltpu.VMEM((q, SSM_INNER), F32)],
        compiler_params=pltpu.CompilerParams(
            dimension_semantics=("parallel", "arbitrary"), vmem_limit_bytes=VMEM_LIMIT),
        name="mamba2_ssd",
    )(p3, p3, p3, small3, _pad_row(dt_bias, S_DT),
      _pad_row(a_log, S_DT), jnp.repeat(d_skip.astype(F32), SSM_HEAD_DIM).reshape(1, SSM_INNER),
      norm_w.reshape(1, SSM_INNER))


def _gdn_kernel(act_ref, z_ref, sm_ref, dtb_ref, alog_ref, nw_ref, o_ref, state_ref):
    cs = GDN_CHUNK
    rr = GDN_ROWS
    nch = rr // cs
    d = GDN_HEAD_DIM
    sh = int(math.log2(cs))
    step = pl.program_id(1)

    @pl.when(step == 0)
    def _():
        state_ref[...] = jnp.zeros_like(state_ref)

    nseq = act_ref.shape[0]
    row = lax.broadcasted_iota(jnp.int32, (rr, rr), 0)
    colm = lax.broadcasted_iota(jnp.int32, (rr, rr), 1)
    same_blk = lax.shift_right_logical(row, sh) == lax.shift_right_logical(colm, sh)
    tri_bd = (same_blk & (row >= colm)).astype(F32)
    beta, g_cs, g_t, beta_t = [], [], [], []
    for sq in range(nseq):
        sm = sm_ref[sq]
        beta.append(1.0 / (1.0 + jnp.exp(-sm)))
        gl = -jnp.exp(alog_ref[...]) * _softplus(sm + dtb_ref[...])
        g_cs.append(_dot_f32(tri_bd, gl))
        g_t.append(g_cs[sq].T)
        beta_t.append(beta[sq].T)

    l_idx = lax.broadcasted_iota(jnp.int32, (cs, rr), 0)
    j_idx = lax.broadcasted_iota(jnp.int32, (cs, rr), 1)
    s_idx = jnp.bitwise_and(j_idx, cs - 1)
    blk = lax.shift_right_logical(j_idx, sh)
    incl_cat = l_idx >= s_idx
    strict_cat = l_idx > s_idx
    eye_cat = (l_idx == s_idx).astype(F32)
    blk_row = lax.shift_right_logical(lax.broadcasted_iota(jnp.int32, (1, rr), 1), sh)

    def to_cat(x):
        if x.shape[1] != rr:
            x = jnp.concatenate([x] * (rr // x.shape[1]), axis=1)
        out = x[(nch - 1) * cs:]
        for c in reversed(range(nch - 1)):
            out = jnp.where(blk == c, x[c * cs:(c + 1) * cs], out)
        return out

    def to_bd(x_cat):
        return jnp.where(same_blk, jnp.concatenate([x_cat] * nch, axis=0), jnp.zeros((), x_cat.dtype))

    pairs = [(sq, h) for sq in range(nseq) for h in range(GDN_HEADS)]
    items = range(len(pairs))
    qn, kn_b, v_b, kn_t, attn_cat, p_cat, t_cat, grow, eg_full = ([None] * len(pairs) for _ in range(9))
    for i in items:
        sq, h = pairs[i]
        qh = act_ref[sq, :, h * d:(h + 1) * d].astype(F32)
        kh = act_ref[sq, :, GDN_WIDTH + h * d:GDN_WIDTH + (h + 1) * d].astype(F32)
        qn[i] = (qh * lax.rsqrt(jnp.sum(qh * qh, axis=-1, keepdims=True) + L2_EPS) * (d ** -0.5)).astype(BF16)
        kn = kh * lax.rsqrt(jnp.sum(kh * kh, axis=-1, keepdims=True) + L2_EPS)
        kn_b[i] = kn.astype(BF16)
        kn_t[i] = kn.T
        v_b[i] = act_ref[sq, :, 2 * GDN_WIDTH + h * d:2 * GDN_WIDTH + (h + 1) * d]
        gcol_full = jnp.broadcast_to(g_cs[sq][:, S_DECAY + h:S_DECAY + h + 1], (rr, LANES))
        bcol_full = jnp.broadcast_to(beta[sq][:, S_BETA + h:S_BETA + h + 1], (rr, LANES))
        eg_full[i] = jnp.exp(gcol_full)
        grow[i] = g_t[sq][S_DECAY + h:S_DECAY + h + 1, :]
        dec = jnp.exp(jnp.where(incl_cat, to_cat(gcol_full) - grow[i], 0.0))
        kk = to_cat(_dot_nt(kn_b[i], kn_b[i]))
        qk = to_cat(_dot_nt(qn[i], kn_b[i]))
        attn_cat[i] = jnp.where(incl_cat, qk * dec, 0.0)
        p_cat[i] = jnp.where(strict_cat, -(kk * dec * to_cat(bcol_full)), 0.0)
        t_cat[i] = eye_cat + p_cat[i]

    p_bd = [to_bd(p_cat[i].astype(BF16)) for i in items]
    for _ in range(sh - 1):
        for i in items:
            p_cat[i] = _dot(p_cat[i].astype(BF16), p_bd[i])
        for i in items:
            p_bd[i] = to_bd(p_cat[i].astype(BF16))
        for i in items:
            t_cat[i] = t_cat[i] + _dot(t_cat[i].astype(BF16), p_bd[i])

    u, w = [None] * len(pairs), [None] * len(pairs)
    for i in items:
        sq, h = pairs[i]
        brow = beta_t[sq][S_BETA + h:S_BETA + h + 1, :]
        u[i] = _dot(to_bd((t_cat[i] * brow).astype(BF16)), v_b[i])
        w[i] = _dot(to_bd((t_cat[i] * (brow * jnp.exp(grow[i]))).astype(BF16)), kn_b[i])

    st = [state_ref[i] for i in items]
    zero_blk = jnp.zeros((cs, d), BF16)
    for c in range(nch):
        rows = slice(c * cs, (c + 1) * cs)
        for i in items:
            sq, h = pairs[i]
            ws = _dot(jnp.concatenate([w[i][rows].astype(BF16), qn[i][rows]], axis=0), st[i].astype(BF16))
            v_new = (u[i][rows] - ws[:cs]).astype(BF16)
            rhs = jnp.concatenate([zero_blk] * c + [v_new] + [zero_blk] * (nch - 1 - c), axis=0)
            g_last = grow[i][:, (c + 1) * cs - 1:(c + 1) * cs]
            e_row = jnp.exp(jnp.where(blk_row == c, g_last - grow[i], 0.0))
            lhs = jnp.concatenate(
                [jnp.where(blk == c, attn_cat[i], 0.0),
                 jnp.where(blk_row == c, kn_t[i] * e_row, 0.0)], axis=0).astype(BF16)
            r = _dot(lhs, rhs)
            o = eg_full[i][rows] * ws[cs:] + r[:cs]
            st[i] = st[i] * jnp.exp(g_last) + r[cs:]
            ms = jnp.mean(o * o, axis=-1, keepdims=True)
            zh = z_ref[sq, rows, h * d:(h + 1) * d].astype(F32)
            o_ref[sq, rows, h * d:(h + 1) * d] = (
                o * lax.rsqrt(ms + RMS_EPS) * nw_ref[...] * _silu(zh)).astype(o_ref.dtype)
    for i in items:
        state_ref[i] = st[i]


def gated_deltanet(p3, small3, dt_bias, a_log, norm_w):
    bsz, seq, _ = p3.shape
    rr = GDN_ROWS
    ns = GDN_SEQS if bsz % GDN_SEQS == 0 else 1
    w3 = 3 * GDN_WIDTH
    const = lambda b, c: (0, 0)
    return pl.pallas_call(
        _gdn_kernel,
        grid=(bsz // ns, seq // rr),
        in_specs=[pl.BlockSpec((ns, rr, w3), lambda b, c: (b, c, P_GDN_QKV // w3)),
                  pl.BlockSpec((ns, rr, GDN_WIDTH), lambda b, c: (b, c, P_GDN_Z // GDN_WIDTH)),
                  pl.BlockSpec((ns, rr, LANES), lambda b, c: (b, c, 0)),
                  pl.BlockSpec((1, LANES), const),
                  pl.BlockSpec((1, LANES), const),
                  pl.BlockSpec((1, GDN_HEAD_DIM), const)],
        out_specs=pl.BlockSpec((ns, rr, GDN_WIDTH), lambda b, c: (b, c, 0)),
        out_shape=jax.ShapeDtypeStruct((bsz, seq, GDN_WIDTH), BF16),
        scratch_shapes=[pltpu.VMEM((ns * GDN_HEADS, GDN_HEAD_DIM, GDN_HEAD_DIM), F32)],
        compiler_params=pltpu.CompilerParams(
            dimension_semantics=("parallel", "arbitrary"), vmem_limit_bytes=VMEM_LIMIT),
        name="gated_deltanet",
    )(p3, p3, small3, _pad_row(dt_bias, S_DECAY), _pad_row(a_log, S_DECAY),
      norm_w.reshape(1, GDN_HEAD_DIM))


def _merge_kernel(oa_ref, os_ref, og_ref, wb_ref, g0_ref, g1_ref, g2_ref, o_ref):
    acc = None
    for br, (x_ref, g_ref) in enumerate(((oa_ref, g0_ref), (os_ref, g1_ref), (og_ref, g2_ref))):
        gate = 1.0 / (1.0 + jnp.exp(-g_ref[...].astype(F32)))
        term = gate * _dot(x_ref[...], wb_ref[br].astype(BF16))
        acc = term if acc is None else acc + term
    o_ref[...] = acc.astype(o_ref.dtype)


def merge_branches(o_da, o_ssm, o_gdn, wb3, p, *, tm=1024, tn=512):
    m = o_da.shape[0]
    gb = P_GATES // tn
    nb = D_MODEL // tn
    act = pl.BlockSpec((tm, DA_WIDTH), lambda i, j: (i, 0))
    return pl.pallas_call(
        _merge_kernel,
        grid=(m // tm, nb),
        in_specs=[act, act, act,
                  pl.BlockSpec((N_BRANCH, DA_WIDTH, tn), lambda i, j: (0, 0, j)),
                  pl.BlockSpec((tm, tn), lambda i, j: (i, gb + j)),
                  pl.BlockSpec((tm, tn), lambda i, j: (i, gb + nb + j)),
                  pl.BlockSpec((tm, tn), lambda i, j: (i, gb + 2 * nb + j))],
        out_specs=pl.BlockSpec((tm, tn), lambda i, j: (i, j)),
        out_shape=jax.ShapeDtypeStruct((m, D_MODEL), BF16),
        compiler_params=pltpu.CompilerParams(
            dimension_semantics=("parallel", "arbitrary"), vmem_limit_bytes=VMEM_LIMIT),
        name="merge_branches",
    )(o_da, o_ssm, o_gdn, wb3, p, p, p)


def _ffn_up_kernel(x_ref, wg_ref, wv_ref, cwg_ref, cwv_ref, cbg_ref, cbv_ref, o_ref,
                   ug0_ref, ug1_ref, uv0_ref, uv1_ref, wgb_ref, wvb_ref, *, rows):
    seq = x_ref.shape[0]
    nblk = seq // rows
    ug = (ug0_ref, ug1_ref)
    uv = (uv0_ref, uv1_ref)
    ug0_ref[0:HALO, :] = jnp.zeros((HALO, ug0_ref.shape[1]), F32)
    uv0_ref[0:HALO, :] = jnp.zeros((HALO, uv0_ref.shape[1]), F32)
    wgb_ref[...] = wg_ref[...].astype(BF16)
    wvb_ref[...] = wv_ref[...].astype(BF16)

    def project(b):
        xr = x_ref[b * rows:(b + 1) * rows, :]
        ug[b % 2][HALO:HALO + rows, :] = _dot(xr, wgb_ref[...])
        uv[b % 2][HALO:HALO + rows, :] = _dot(xr, wvb_ref[...])

    def conv(buf, cw_ref, cb_ref):
        acc = cb_ref[...] + cw_ref[0:1, :] * buf[HALO - 2:HALO - 2 + rows, :]
        for t in range(1, FFN_CONV):
            acc = acc + cw_ref[t:t + 1, :] * buf[HALO - 2 + t:HALO - 2 + t + rows, :]
        return acc

    def activate(b):
        gate = conv(ug[b % 2], cwg_ref, cbg_ref)
        val = conv(uv[b % 2], cwv_ref, cbv_ref)
        o_ref[b * rows:(b + 1) * rows, :] = (_silu(gate) * val).astype(o_ref.dtype)

    project(0)
    for b in range(1, nblk):
        ug[b % 2][0:HALO, :] = ug[(b - 1) % 2][rows:rows + HALO, :]
        uv[b % 2][0:HALO, :] = uv[(b - 1) % 2][rows:rows + HALO, :]
        project(b)
        activate(b - 1)
    activate(nblk - 1)


def ffn_up_proj(hn3, w_up, conv_w, conv_b, *, tf=512, rows=256):
    bsz, seq, d = hn3.shape
    nf = D_FF // tf
    cb = conv_b.reshape(1, 2 * D_FF)
    return pl.pallas_call(
        functools.partial(_ffn_up_kernel, rows=rows),
        grid=(bsz, nf),
        in_specs=[pl.BlockSpec((None, seq, d), lambda b, f: (b, 0, 0)),
                  pl.BlockSpec((d, tf), lambda b, f: (0, f)),
                  pl.BlockSpec((d, tf), lambda b, f: (0, nf + f)),
                  pl.BlockSpec((FFN_CONV, tf), lambda b, f: (0, f)),
                  pl.BlockSpec((FFN_CONV, tf), lambda b, f: (0, nf + f)),
                  pl.BlockSpec((1, tf), lambda b, f: (0, f)),
                  pl.BlockSpec((1, tf), lambda b, f: (0, nf + f))],
        out_specs=pl.BlockSpec((None, seq, tf), lambda b, f: (b, 0, f)),
        out_shape=jax.ShapeDtypeStruct((bsz, seq, D_FF), BF16),
        scratch_shapes=[pltpu.VMEM((HALO + rows, tf), F32)] * 4
                       + [pltpu.VMEM((d, tf), BF16), pltpu.VMEM((d, tf), BF16)],
        compiler_params=pltpu.CompilerParams(
            dimension_semantics=("parallel", "arbitrary"), vmem_limit_bytes=VMEM_LIMIT),
        name="ffn_up",
    )(hn3, w_up, w_up, conv_w, conv_w, cb, cb)


def _reorder_w_in(w):
    sizes = (DA_WIDTH, DA_WIDTH, DA_WIDTH, SSM_INNER, SSM_INNER + SSM_BC, SSM_HEADS,
             3 * GDN_WIDTH, GDN_WIDTH, GDN_HEADS, GDN_HEADS, N_BRANCH * D_MODEL)
    offs = [0]
    for s in sizes:
        offs.append(offs[-1] + s)
    seg = lambda i: w[:, offs[i]:offs[i + 1]]
    da_q, da_k, da_v, ssm_z, ssm_xbc, ssm_dt, gdn_qkv, gdn_z, gdn_b, gdn_a, gates = (
        seg(i) for i in range(len(sizes)))
    big = jnp.concatenate([gdn_qkv, ssm_z, gdn_z, da_q, da_k, da_v, gates, ssm_xbc], axis=1)
    small = jnp.concatenate(
        [ssm_dt, gdn_b, gdn_a,
         jnp.zeros((w.shape[0], LANES - SSM_HEADS - 2 * GDN_HEADS), w.dtype)], axis=1)
    return big.astype(BF16), small.astype(BF16)


def kernel(x, norm_mix, w_in, da_lambda, da_subln, ssm_conv_w, ssm_conv_b, ssm_dt_bias, ssm_a_log,
           ssm_d, ssm_norm, gdn_conv_w, gdn_dt_bias, gdn_a_log, gdn_norm, w_branch, w_out, norm_ffn,
           ffn_up, ffn_conv_w, ffn_conv_b, ffn_down, norm_final):
    bsz, seq, d = x.shape
    m = bsz * seq
    h = x.reshape(m, d)
    w_out_b = cast_bf16(w_out.reshape(DEPTH * d, d)).reshape(DEPTH, d, d)
    ffn_down_b = cast_bf16(ffn_down.reshape(DEPTH * D_FF, d)).reshape(DEPTH, D_FF, d)
    for l in range(DEPTH):
        w_big, w_small = _reorder_w_in(w_in[l])
        p, small = in_proj(h, norm_mix[l], w_big, w_small, gdn_conv_w[l], ssm_conv_w[l], ssm_conv_b[l], seq=seq)
        p3 = p.reshape(bsz, seq, P_COLS)
        small3 = small.reshape(bsz, seq, LANES)
        lambda_init = 0.8 - 0.6 * math.exp(-0.3 * l)
        o_da = diff_attention(p3, da_lambda[l], da_subln[l], lambda_init)
        o_ssm = mamba2_ssd(p3, small3, ssm_dt_bias[l], ssm_a_log[l], ssm_d[l], ssm_norm[l])
        o_gdn = gated_deltanet(p3, small3, gdn_dt_bias[l], gdn_a_log[l], gdn_norm[l])
        wb3 = w_branch[l].reshape(N_BRANCH, DA_WIDTH, D_MODEL)
        merged = merge_branches(o_da.reshape(m, DA_WIDTH), o_ssm.reshape(m, SSM_INNER),
                                o_gdn.reshape(m, GDN_WIDTH), wb3, p)
        h, hn = matmul_res_norm(merged, w_out_b[l], h, norm_ffn[l], tm=512, name="out_proj")
        act = ffn_up_proj(hn.reshape(bsz, seq, d), ffn_up[l], ffn_conv_w[l], ffn_conv_b[l])
        h = matmul_res(act.reshape(m, D_FF), ffn_down_b[l], h, tm=512, tn=1024, name="ffn_down")
    return rmsnorm(h, norm_final, out_dtype=F32).reshape(bsz, seq, d)
```

```python
import functools
import math

import jax
import jax.numpy as jnp
from jax import lax
from jax.experimental import pallas as pl
from jax.experimental.pallas import tpu as pltpu

F32 = jnp.float32
BF16 = jnp.bfloat16

D_MODEL = 2048
DEPTH = 2
DA_HEADS = 8
DA_HEAD_DIM = 64
DA_WIDTH = DA_HEADS * 2 * DA_HEAD_DIM
SSM_HEADS = 16
SSM_HEAD_DIM = 64
SSM_INNER = SSM_HEADS * SSM_HEAD_DIM
SSM_GROUPS = 2
SSM_STATE = 128
SSM_CONV = 4
SSM_CHUNK = 128
SSM_BC = 2 * SSM_GROUPS * SSM_STATE
GDN_HEADS = 8
GDN_HEAD_DIM = 128
GDN_WIDTH = GDN_HEADS * GDN_HEAD_DIM
GDN_CONV = 4
GDN_CHUNK = 64
GDN_ROWS = 256
GDN_SEQS = 1
D_FF = 5632
FFN_CONV = 3
N_BRANCH = 3
RMS_EPS = 1e-6
L2_EPS = 1e-6
LOG2E = math.log2(math.e)

LANES = 128
HALO = 8
VMEM_LIMIT = 56 * 1024 * 1024

P_GDN_QKV = 0
P_SSM_Z = P_GDN_QKV + 3 * GDN_WIDTH
P_GDN_Z = P_SSM_Z + SSM_INNER
P_DA_Q = P_GDN_Z + GDN_WIDTH
P_DA_K = P_DA_Q + DA_WIDTH
P_DA_V = P_DA_K + DA_WIDTH
P_GATES = P_DA_V + DA_WIDTH
P_SSM_X = P_GATES + N_BRANCH * D_MODEL
P_SSM_BC = P_SSM_X + SSM_INNER
P_COLS = P_SSM_BC + SSM_BC
S_DT = 0
S_BETA = SSM_HEADS
S_DECAY = SSM_HEADS + GDN_HEADS


def _silu(x):
    return x / (1.0 + jnp.exp(-x))


def _softplus(x):
    return jnp.maximum(x, 0.0) + jnp.log1p(jnp.exp(-jnp.abs(x)))


def _dot(a, b):
    return jnp.dot(a, b, preferred_element_type=F32)


def _dot_nt(a, b):
    return lax.dot_general(a, b, (((1,), (1,)), ((), ())), preferred_element_type=F32)


def _dot_f32(a, b):
    return jnp.dot(a, b, preferred_element_type=F32, precision=lax.Precision.HIGHEST)


def _rmsnorm_kernel(x_ref, w_ref, o_ref):
    x = x_ref[...]
    ms = jnp.mean(x * x, axis=-1, keepdims=True)
    o_ref[...] = (x * lax.rsqrt(ms + RMS_EPS) * w_ref[...]).astype(o_ref.dtype)


def rmsnorm(x, w, *, tm=512, out_dtype=BF16):
    m, d = x.shape
    return pl.pallas_call(
        _rmsnorm_kernel,
        grid=(m // tm,),
        in_specs=[pl.BlockSpec((tm, d), lambda i: (i, 0)),
                  pl.BlockSpec((1, d), lambda i: (0, 0))],
        out_specs=pl.BlockSpec((tm, d), lambda i: (i, 0)),
        out_shape=jax.ShapeDtypeStruct((m, d), out_dtype),
        name="rmsnorm",
    )(x, w.reshape(1, d))


IN_TM = 1024
IN_TN = 512
IN_ROWS = 256


def _in_proj_kernel(x_ref, nw_ref, w_ref, ws_ref, cwg_ref, cws_ref, cbs_ref, o_ref, small_ref,
                    xn_ref, u_ref, u2_ref, halo_ref, *, conv_front, conv_back, tiles_per_seq):
    i = pl.program_id(0)
    j = pl.program_id(1)
    tm = x_ref.shape[0]

    @pl.when(j == 0)
    def _():
        x = x_ref[...]
        ms = jnp.mean(x * x, axis=-1, keepdims=True)
        xn_ref[...] = (x * lax.rsqrt(ms + RMS_EPS) * nw_ref[...]).astype(xn_ref.dtype)
        small_ref[...] = _dot(xn_ref[...], ws_ref[...])

    is_conv = (j < conv_front) | (j >= conv_back)

    @pl.when(jnp.logical_not(is_conv))
    def _():
        o_ref[...] = _dot(xn_ref[...], w_ref[...]).astype(o_ref.dtype)

    @pl.when(is_conv)
    def _():
        first = lax.rem(i, tiles_per_seq) == 0

        @pl.when(first)
        def _():
            u_ref[0:HALO, :] = jnp.zeros((HALO, u_ref.shape[1]), F32)

        @pl.when(jnp.logical_not(first))
        def _():
            u_ref[0:HALO, :] = halo_ref[j]

        bufs = (u_ref, u2_ref)
        nblk = tm // IN_ROWS
        front = j < conv_front
        cw = jnp.where(front, cwg_ref[...], cws_ref[...])
        cb = jnp.where(front, 0.0, cbs_ref[...])

        def project(b):
            bufs[b % 2][HALO:HALO + IN_ROWS, :] = _dot(xn_ref[b * IN_ROWS:(b + 1) * IN_ROWS, :], w_ref[...])

        def activate(b):
            buf = bufs[b % 2]
            acc = cb + cw[0:1, :] * buf[HALO - 3:HALO - 3 + IN_ROWS, :]
            for t in range(1, SSM_CONV):
                acc = acc + cw[t:t + 1, :] * buf[HALO - 3 + t:HALO - 3 + t + IN_ROWS, :]
            o_ref[b * IN_ROWS:(b + 1) * IN_ROWS, :] = _silu(acc).astype(o_ref.dtype)

        project(0)
        for b in range(1, nblk):
            bufs[b % 2][0:HALO, :] = bufs[(b - 1) % 2][IN_ROWS:IN_ROWS + HALO, :]
            project(b)
            activate(b - 1)
        activate(nblk - 1)
        halo_ref[j] = bufs[(nblk - 1) % 2][IN_ROWS:IN_ROWS + HALO, :]


def in_proj(x, nw, w, w_small, gdn_cw, ssm_cw, ssm_cb, *, seq):
    m, k = x.shape
    assert GDN_CONV == SSM_CONV and P_GDN_QKV == 0 and P_SSM_X % IN_TN == 0 and (3 * GDN_WIDTH) % IN_TN == 0
    n_tiles = P_COLS // IN_TN
    front = 3 * GDN_WIDTH // IN_TN
    back = P_SSM_X // IN_TN
    gdn_tile = lambda i, j: (0, jnp.minimum(j, front - 1))
    ssm_tile = lambda i, j: (0, jnp.clip(j - back, 0, n_tiles - back - 1))
    return pl.pallas_call(
        functools.partial(_in_proj_kernel, conv_front=front, conv_back=back, tiles_per_seq=seq // IN_TM),
        grid=(m // IN_TM, n_tiles),
        in_specs=[pl.BlockSpec((IN_TM, k), lambda i, j: (i, 0)),
                  pl.BlockSpec((1, k), lambda i, j: (0, 0)),
                  pl.BlockSpec((k, IN_TN), lambda i, j: (0, j)),
                  pl.BlockSpec((k, LANES), lambda i, j: (0, 0)),
                  pl.BlockSpec((GDN_CONV, IN_TN), gdn_tile),
                  pl.BlockSpec((SSM_CONV, IN_TN), ssm_tile),
                  pl.BlockSpec((1, IN_TN), ssm_tile)],
        out_specs=[pl.BlockSpec((IN_TM, IN_TN), lambda i, j: (i, j)),
                   pl.BlockSpec((IN_TM, LANES), lambda i, j: (i, 0))],
        out_shape=[jax.ShapeDtypeStruct((m, P_COLS), BF16), jax.ShapeDtypeStruct((m, LANES), F32)],
        scratch_shapes=[pltpu.VMEM((IN_TM, k), BF16),
                        pltpu.VMEM((HALO + IN_ROWS, IN_TN), F32),
                        pltpu.VMEM((HALO + IN_ROWS, IN_TN), F32),
                        pltpu.VMEM((n_tiles, HALO, IN_TN), F32)],
        compiler_params=pltpu.CompilerParams(
            dimension_semantics=("arbitrary", "arbitrary"), vmem_limit_bytes=VMEM_LIMIT),
        name="in_proj",
    )(x, nw.reshape(1, k), w, w_small, gdn_cw, ssm_cw, ssm_cb.reshape(1, -1))


def _matmul_res_norm_kernel(a_ref, w_ref, h_ref, nw_ref, ho_ref, no_ref):
    h = h_ref[...] + _dot(a_ref[...], w_ref[...])
    ho_ref[...] = h
    ms = jnp.mean(h * h, axis=-1, keepdims=True)
    no_ref[...] = (h * lax.rsqrt(ms + RMS_EPS) * nw_ref[...]).astype(no_ref.dtype)


def matmul_res_norm(a, w, layer, h, nw, *, tm, name):
    m, k = a.shape
    _, _, n = w.shape
    row = lambda i: (i, 0)
    return pl.pallas_call(
        _matmul_res_norm_kernel,
        grid=(m // tm,),
        in_specs=[pl.BlockSpec((tm, k), row),
                  pl.BlockSpec((None, k, n), lambda i: (layer, 0, 0)),
                  pl.BlockSpec((tm, n), row),
                  pl.BlockSpec((1, n), lambda i: (0, 0))],
        out_specs=[pl.BlockSpec((tm, n), row), pl.BlockSpec((tm, n), row)],
        out_shape=[jax.ShapeDtypeStruct((m, n), F32), jax.ShapeDtypeStruct((m, n), BF16)],
        compiler_params=pltpu.CompilerParams(
            dimension_semantics=("parallel",), vmem_limit_bytes=VMEM_LIMIT),
        name=name,
    )(a, w, h, nw.reshape(1, n))


def _matmul_res_kernel(a_ref, w_ref, h_ref, o_ref):
    o_ref[...] = h_ref[...] + _dot(a_ref[...], w_ref[...])


def matmul_res(a, w, layer, h, *, tm, tn, name):
    m, k = a.shape
    _, _, n = w.shape
    return pl.pallas_call(
        _matmul_res_kernel,
        grid=(n // tn, m // tm),
        in_specs=[pl.BlockSpec((tm, k), lambda j, i: (i, 0)),
                  pl.BlockSpec((None, k, tn), lambda j, i: (layer, 0, j)),
                  pl.BlockSpec((tm, tn), lambda j, i: (i, j))],
        out_specs=pl.BlockSpec((tm, tn), lambda j, i: (i, j)),
        out_shape=jax.ShapeDtypeStruct((m, n), F32),
        compiler_params=pltpu.CompilerParams(
            dimension_semantics=("parallel", "parallel"), vmem_limit_bytes=VMEM_LIMIT),
        name=name,
    )(a, w, h)


DA_BIAS_LANES = 3


def _diff_attn_kernel(slopes_ref, lamp_ref, subln_ref, q_ref, k_ref, v_ref, o_ref,
                      k1_ref, k2_ref, s_ref, st_ref, p_ref, *, tq, lambda_init):
    hd = DA_HEAD_DIM
    seq = q_ref.shape[0]
    nq = seq // tq
    half = tq // 2
    slope = slopes_ref[pl.program_id(1)]
    lp = lamp_ref[...]
    lam = (jnp.exp(jnp.sum(lp[0:1] * lp[1:2], axis=-1, keepdims=True))
           - jnp.exp(jnp.sum(lp[2:3] * lp[3:4], axis=-1, keepdims=True)) + lambda_init)

    lane_k = lax.broadcasted_iota(jnp.int32, (seq, 2 * hd), 1)
    bias = (slope * LOG2E) * lax.broadcasted_iota(jnp.int32, (seq, 2 * hd), 0).astype(F32)
    pieces = []
    rest = bias
    for _ in range(DA_BIAS_LANES):
        piece = rest.astype(BF16).astype(F32)
        pieces.append(piece)
        rest = rest - piece
    kf = k_ref[...].astype(F32)
    k1 = jnp.where(lane_k < hd, kf, 0.0)
    k2 = jnp.where(lane_k >= hd, kf, 0.0)
    for t, piece in enumerate(pieces):
        k1 = jnp.where(lane_k == hd + t, piece, k1)
        k2 = jnp.where(lane_k == t, piece, k2)
    k1_ref[...] = k1.astype(BF16)
    k2_ref[...] = k2.astype(BF16)

    lane_q = lax.broadcasted_iota(jnp.int32, (tq, 2 * hd), 1)
    causal = (lax.broadcasted_iota(jnp.int32, (tq, tq), 0)
              >= lax.broadcasted_iota(jnp.int32, (tq, tq), 1))

    def fold(x):
        return x[:, :half], x[:, half:]

    for qi in range(nq):
        rows = slice(qi * tq, (qi + 1) * tq)
        qf = q_ref[rows, :].astype(F32) * (hd ** -0.5 * LOG2E)
        q1 = jnp.where(lane_q < hd, qf, jnp.where(lane_q < hd + DA_BIAS_LANES, 1.0, 0.0)).astype(BF16)
        q2 = jnp.where(lane_q >= hd, qf, jnp.where(lane_q < DA_BIAS_LANES, 1.0, 0.0)).astype(BF16)

        mx = [None, None]
        for j in range(qi + 1):
            keys = slice(j * tq, (j + 1) * tq)
            for mp, (qm, km_ref) in enumerate(((q1, k1_ref), (q2, k2_ref))):
                s = _dot_nt(qm, km_ref[keys, :])
                if j == qi:
                    s = jnp.where(causal, s, -jnp.inf)
                s_ref[mp, j] = s
                a, b = fold(s)
                ab = jnp.maximum(a, b)
                mx[mp] = ab if mx[mp] is None else jnp.maximum(mx[mp], ab)
        for mp in range(2):
            st_ref[mp] = jnp.broadcast_to(jnp.max(mx[mp], axis=-1, keepdims=True), (tq, half))

        sm = [None, None]
        for j in range(qi + 1):
            for mp in range(2):
                m = st_ref[mp]
                a, b = fold(s_ref[mp, j])
                pa = jnp.exp2(a - m)
                pb = jnp.exp2(b - m)
                p_ref[mp, :, j * tq:(j + 1) * tq] = jnp.concatenate([pa, pb], axis=1).astype(BF16)
                sm[mp] = pa + pb if sm[mp] is None else sm[mp] + (pa + pb)
        l1 = jnp.sum(sm[0], axis=-1, keepdims=True)
        l2 = jnp.sum(sm[1], axis=-1, keepdims=True)

        kv = (qi + 1) * tq
        o = (_dot(p_ref[0, :, :kv], v_ref[:kv, :]) * (1.0 / l1)
             - _dot(p_ref[1, :, :kv], v_ref[:kv, :]) * (lam / l2))
        ms = jnp.mean(o * o, axis=-1, keepdims=True)
        o = o * lax.rsqrt(ms + RMS_EPS) * subln_ref[...] * (1.0 - lambda_init)
        o_ref[rows, :] = o.astype(o_ref.dtype)


def diff_attention(p3, lam_params, subln_w, lambda_init, *, tq=256):
    bsz, seq, _ = p3.shape
    hw = 2 * DA_HEAD_DIM
    nq = seq // tq
    slopes = jnp.asarray([2.0 ** (-8.0 * (h + 1) / DA_HEADS) for h in range(DA_HEADS)], F32)
    qb, kb, vb = P_DA_Q // hw, P_DA_K // hw, P_DA_V // hw
    return pl.pallas_call(
        functools.partial(_diff_attn_kernel, tq=tq, lambda_init=lambda_init),
        grid=(bsz, DA_HEADS),
        in_specs=[pl.BlockSpec(memory_space=pltpu.SMEM),
                  pl.BlockSpec((4, DA_HEAD_DIM), lambda b, h: (0, 0)),
                  pl.BlockSpec((1, hw), lambda b, h: (0, 0)),
                  pl.BlockSpec((None, seq, hw), lambda b, h: (b, 0, qb + h)),
                  pl.BlockSpec((None, seq, hw), lambda b, h: (b, 0, kb + h)),
                  pl.BlockSpec((None, seq, hw), lambda b, h: (b, 0, vb + h))],
        out_specs=pl.BlockSpec((None, seq, hw), lambda b, h: (b, 0, h)),
        out_shape=jax.ShapeDtypeStruct((bsz, seq, DA_WIDTH), BF16),
        scratch_shapes=[pltpu.VMEM((seq, hw), BF16),
                        pltpu.VMEM((seq, hw), BF16),
                        pltpu.VMEM((2, nq, tq, tq), F32),
                        pltpu.VMEM((2, tq, tq // 2), F32),
                        pltpu.VMEM((2, tq, seq), BF16)],
        compiler_params=pltpu.CompilerParams(
            dimension_semantics=("parallel", "parallel"), vmem_limit_bytes=VMEM_LIMIT),
        name="diff_attention",
    )(slopes, lam_params, subln_w.reshape(1, hw), p3, p3, p3)


def _ssd_kernel(x_ref, bc_ref, z_ref, sm_ref, dtb_ref, alog_ref, dskip_ref, nw_ref,
                o_ref, state_ref, y_ref):
    q = SSM_CHUNK
    c = pl.program_id(1)

    @pl.when(c == 0)
    def _():
        state_ref[...] = jnp.zeros_like(state_ref)

    xs_b = x_ref[...]
    xs = xs_b.astype(F32)
    bc = bc_ref[...]

    dt = _softplus(sm_ref[...] + dtb_ref[...])
    da = dt * (-jnp.exp(alog_ref[...]))
    row = lax.broadcasted_iota(jnp.int32, (q, q), 0)
    colm = lax.broadcasted_iota(jnp.int32, (q, q), 1)
    causal = row >= colm
    tri = causal.astype(F32)
    a_cs = _dot_f32(tri, da)
    a_cs_t = a_cs.T
    dt_t = dt.T
    lane = lax.broadcasted_iota(jnp.int32, (1, LANES), 1)

    for g in range(SSM_GROUPS):
        bm_b = bc[:, g * SSM_STATE:(g + 1) * SSM_STATE]
        cm_b = bc[:, (SSM_GROUPS + g) * SSM_STATE:(SSM_GROUPS + g + 1) * SSM_STATE]
        cm = cm_b.astype(F32)
        cb = _dot_nt(cm_b, bm_b)
        bm_t = bm_b.astype(F32).T
        hpg = SSM_HEADS // SSM_GROUPS
        for pr in range(hpg // 2):
            pair = g * (hpg // 2) + pr
            x_pair = xs_b[:, pair * LANES:(pair + 1) * LANES]
            st = state_ref[pair]
            rhs = jnp.concatenate([x_pair, st.astype(BF16)], axis=0)
            ys, sts, cds = [], [], []
            for sub in range(2):
                h = 2 * pair + sub
                acol = a_cs[:, h:h + 1]
                arow = a_cs_t[h:h + 1, :]
                dtrow = dt_t[h:h + 1, :]
                decay = jnp.exp(jnp.where(causal, acol - arow, -jnp.inf))
                sc = (cb * decay * dtrow).astype(BF16)
                c_in = (cm * jnp.exp(acol)).astype(BF16)
                ys.append(_dot(jnp.concatenate([sc, c_in], axis=1), rhs))
                a_last = arow[:, q - 1:q]
                wrow = jnp.exp(a_last - arow) * dtrow
                sts.append(_dot((bm_t * wrow).astype(BF16), x_pair))
                cds.append(jnp.exp(a_last))
            first = lane < SSM_HEAD_DIM
            y_ref[:, pair * LANES:(pair + 1) * LANES] = jnp.where(first, ys[0], ys[1])
            state_ref[pair] = (st * jnp.where(first, cds[0], cds[1])
                               + jnp.where(first, sts[0], sts[1]))

    y = y_ref[...] + xs * dskip_ref[...]
    y = y * _silu(z_ref[...].astype(F32))
    gw = SSM_INNER // SSM_GROUPS
    for g in range(SSM_GROUPS):
        yg = y[:, g * gw:(g + 1) * gw]
        ms = jnp.mean(yg * yg, axis=-1, keepdims=True)
        o_ref[:, g * gw:(g + 1) * gw] = (
            yg * lax.rsqrt(ms + RMS_EPS) * nw_ref[:, g * gw:(g + 1) * gw]).astype(o_ref.dtype)


def _pad_row(v, offset):
    return jnp.zeros((1, LANES), F32).at[0, offset:offset + v.shape[0]].set(v.astype(F32))


def mamba2_ssd(p3, small3, dt_bias, a_log, d_skip, norm_w):
    bsz, seq, _ = p3.shape
    q = SSM_CHUNK
    nc = seq // q
    const = lambda b, c: (0, 0)
    return pl.pallas_call(
        _ssd_kernel,
        grid=(bsz, nc),
        in_specs=[pl.BlockSpec((None, q, SSM_INNER), lambda b, c: (b, c, P_SSM_X // SSM_INNER)),
                  pl.BlockSpec((None, q, SSM_BC), lambda b, c: (b, c, P_SSM_BC // SSM_BC)),
                  pl.BlockSpec((None, q, SSM_INNER), lambda b, c: (b, c, P_SSM_Z // SSM_INNER)),
                  pl.BlockSpec((None, q, LANES), lambda b, c: (b, c, 0)),
                  pl.BlockSpec((1, LANES), const),
                  pl.BlockSpec((1, LANES), const),
                  pl.BlockSpec((1, SSM_INNER), const),
                  pl.BlockSpec((1, SSM_INNER), const)],
        out_specs=pl.BlockSpec((None, q, SSM_INNER), lambda b, c: (b, c, 0)),
        out_shape=jax.ShapeDtypeStruct((bsz, seq, SSM_INNER), BF16),
        scratch_shapes=[pltpu.VMEM((SSM_HEADS // 2, SSM_STATE, LANES), F32),
                        pltpu.VMEM((q, SSM_INNER), F32)],
        compiler_params=pltpu.CompilerParams(
            dimension_semantics=("parallel", "arbitrary"), vmem_limit_bytes=VMEM_LIMIT),
        name="mamba2_ssd",
    )(p3, p3, p3, small3, _pad_row(dt_bias, S_DT),
      _pad_row(a_log, S_DT), jnp.repeat(d_skip.astype(F32), SSM_HEAD_DIM).reshape(1, SSM_INNER),
      norm_w.reshape(1, SSM_INNER))


def _gdn_kernel(act_ref, z_ref, sm_ref, dtb_ref, alog_ref, nw_ref, o_ref, state_ref):
    cs = GDN_CHUNK
    rr = GDN_ROWS
    nch = rr // cs
    d = GDN_HEAD_DIM
    sh = int(math.log2(cs))
    step = pl.program_id(1)

    @pl.when(step == 0)
    def _():
        state_ref[...] = jnp.zeros_like(state_ref)

    nseq = act_ref.shape[0]
    row = lax.broadcasted_iota(jnp.int32, (rr, rr), 0)
    colm = lax.broadcasted_iota(jnp.int32, (rr, rr), 1)
    same_blk = lax.shift_right_logical(row, sh) == lax.shift_right_logical(colm, sh)
    tri_bd = (same_blk & (row >= colm)).astype(F32)
    beta, g_cs, g_t, beta_t = [], [], [], []
    for sq in range(nseq):
        sm = sm_ref[sq]
        beta.append(1.0 / (1.0 + jnp.exp(-sm)))
        gl = -jnp.exp(alog_ref[...]) * _softplus(sm + dtb_ref[...])
        g_cs.append(_dot_f32(tri_bd, gl))
        g_t.append(g_cs[sq].T)
        beta_t.append(beta[sq].T)

    l_idx = lax.broadcasted_iota(jnp.int32, (cs, rr), 0)
    j_idx = lax.broadcasted_iota(jnp.int32, (cs, rr), 1)
    s_idx = jnp.bitwise_and(j_idx, cs - 1)
    blk = lax.shift_right_logical(j_idx, sh)
    incl_cat = l_idx >= s_idx
    strict_cat = l_idx > s_idx
    eye_cat = (l_idx == s_idx).astype(F32)
    blk_row = lax.shift_right_logical(lax.broadcasted_iota(jnp.int32, (1, rr), 1), sh)

    def to_cat(x):
        if x.shape[1] != rr:
            x = jnp.concatenate([x] * (rr // x.shape[1]), axis=1)
        out = x[(nch - 1) * cs:]
        for c in reversed(range(nch - 1)):
            out = jnp.where(blk == c, x[c * cs:(c + 1) * cs], out)
        return out

    def to_bd(x_cat):
        return jnp.where(same_blk, jnp.concatenate([x_cat] * nch, axis=0), jnp.zeros((), x_cat.dtype))

    pairs = [(sq, h) for sq in range(nseq) for h in range(GDN_HEADS)]
    items = range(len(pairs))
    qn, kn_b, v_b, kn_t, attn_cat, p_cat, t_cat, grow, eg_full = ([None] * len(pairs) for _ in range(9))
    for i in items:
        sq, h = pairs[i]
        qh = act_ref[sq, :, h * d:(h + 1) * d].astype(F32)
        kh = act_ref[sq, :, GDN_WIDTH + h * d:GDN_WIDTH + (h + 1) * d].astype(F32)
        qn[i] = (qh * lax.rsqrt(jnp.sum(qh * qh, axis=-1, keepdims=True) + L2_EPS) * (d ** -0.5)).astype(BF16)
        kn = kh * lax.rsqrt(jnp.sum(kh * kh, axis=-1, keepdims=True) + L2_EPS)
        kn_b[i] = kn.astype(BF16)
        kn_t[i] = kn.T
        v_b[i] = act_ref[sq, :, 2 * GDN_WIDTH + h * d:2 * GDN_WIDTH + (h + 1) * d]
        gcol_full = jnp.broadcast_to(g_cs[sq][:, S_DECAY + h:S_DECAY + h + 1], (rr, LANES))
        bcol_full = jnp.broadcast_to(beta[sq][:, S_BETA + h:S_BETA + h + 1], (rr, LANES))
        eg_full[i] = jnp.exp(gcol_full)
        grow[i] = g_t[sq][S_DECAY + h:S_DECAY + h + 1, :]
        dec = jnp.exp(jnp.where(incl_cat, to_cat(gcol_full) - grow[i], 0.0))
        kk = to_cat(_dot_nt(kn_b[i], kn_b[i]))
        qk = to_cat(_dot_nt(qn[i], kn_b[i]))
        attn_cat[i] = jnp.where(incl_cat, qk * dec, 0.0)
        p_cat[i] = jnp.where(strict_cat, -(kk * dec * to_cat(bcol_full)), 0.0)
        t_cat[i] = eye_cat + p_cat[i]

    p_bd = [to_bd(p_cat[i].astype(BF16)) for i in items]
    for _ in range(sh - 1):
        for i in items:
            p_cat[i] = _dot(p_cat[i].astype(BF16), p_bd[i])
        for i in items:
            p_bd[i] = to_bd(p_cat[i].astype(BF16))
        for i in items:
            t_cat[i] = t_cat[i] + _dot(t_cat[i].astype(BF16), p_bd[i])

    u, w = [None] * len(pairs), [None] * len(pairs)
    for i in items:
        sq, h = pairs[i]
        brow = beta_t[sq][S_BETA + h:S_BETA + h + 1, :]
        u[i] = _dot(to_bd((t_cat[i] * brow).astype(BF16)), v_b[i])
        w[i] = _dot(to_bd((t_cat[i] * (brow * jnp.exp(grow[i]))).astype(BF16)), kn_b[i])

    st = [state_ref[i] for i in items]
    zero_blk = jnp.zeros((cs, d), BF16)
    for c in range(nch):
        rows = slice(c * cs, (c + 1) * cs)
        for i in items:
            sq, h = pairs[i]
            ws = _dot(jnp.concatenate([w[i][rows].astype(BF16), qn[i][rows]], axis=0), st[i].astype(BF16))
            v_new = (u[i][rows] - ws[:cs]).astype(BF16)
            rhs = jnp.concatenate([zero_blk] * c + [v_new] + [zero_blk] * (nch - 1 - c), axis=0)
            g_last = grow[i][:, (c + 1) * cs - 1:(c + 1) * cs]
            e_row = jnp.exp(jnp.where(blk_row == c, g_last - grow[i], 0.0))
            lhs = jnp.concatenate(
                [jnp.where(blk == c, attn_cat[i], 0.0),
                 jnp.where(blk_row == c, kn_t[i] * e_row, 0.0)], axis=0).astype(BF16)
            r = _dot(lhs, rhs)
            o = eg_full[i][rows] * ws[cs:] + r[:cs]
            st[i] = st[i] * jnp.exp(g_last) + r[cs:]
            ms = jnp.mean(o * o, axis=-1, keepdims=True)
            zh = z_ref[sq, rows, h * d:(h + 1) * d].astype(F32)
            o_ref[sq, rows, h * d:(h + 1) * d] = (
                o * lax.rsqrt(ms + RMS_EPS) * nw_ref[...] * _silu(zh)).astype(o_ref.dtype)
    for i in items:
        state_ref[i] = st[i]


def gated_deltanet(p3, small3, dt_bias, a_log, norm_w):
    bsz, seq, _ = p3.shape
    rr = GDN_ROWS
    ns = GDN_SEQS if bsz % GDN_SEQS == 0 else 1
    w3 = 3 * GDN_WIDTH
    const = lambda b, c: (0, 0)
    return pl.pallas_call(
        _gdn_kernel,
        grid=(bsz // ns, seq // rr),
        in_specs=[pl.BlockSpec((ns, rr, w3), lambda b, c: (b, c, P_GDN_QKV // w3)),
                  pl.BlockSpec((ns, rr, GDN_WIDTH), lambda b, c: (b, c, P_GDN_Z // GDN_WIDTH)),
                  pl.BlockSpec((ns, rr, LANES), lambda b, c: (b, c, 0)),
                  pl.BlockSpec((1, LANES), const),
                  pl.BlockSpec((1, LANES), const),
                  pl.BlockSpec((1, GDN_HEAD_DIM), const)],
        out_specs=pl.BlockSpec((ns, rr, GDN_WIDTH), lambda b, c: (b, c, 0)),
        out_shape=jax.ShapeDtypeStruct((bsz, seq, GDN_WIDTH), BF16),
        scratch_shapes=[pltpu.VMEM((ns * GDN_HEADS, GDN_HEAD_DIM, GDN_HEAD_DIM), F32)],
        compiler_params=pltpu.CompilerParams(
            dimension_semantics=("parallel", "arbitrary"), vmem_limit_bytes=VMEM_LIMIT),
        name="gated_deltanet",
    )(p3, p3, small3, _pad_row(dt_bias, S_DECAY), _pad_row(a_log, S_DECAY),
      norm_w.reshape(1, GDN_HEAD_DIM))


def _merge_kernel(oa_ref, os_ref, og_ref, wb_ref, g0_ref, g1_ref, g2_ref, o_ref):
    acc = None
    for br, (x_ref, g_ref) in enumerate(((oa_ref, g0_ref), (os_ref, g1_ref), (og_ref, g2_ref))):
        gate = 1.0 / (1.0 + jnp.exp(-g_ref[...].astype(F32)))
        term = gate * _dot(x_ref[...], wb_ref[br].astype(BF16))
        acc = term if acc is None else acc + term
    o_ref[...] = acc.astype(o_ref.dtype)


def merge_branches(o_da, o_ssm, o_gdn, wb3, layer, p, *, tm=1024, tn=512):
    m = o_da.shape[0]
    gb = P_GATES // tn
    nb = D_MODEL // tn
    act = pl.BlockSpec((tm, DA_WIDTH), lambda i, j: (i, 0))
    return pl.pallas_call(
        _merge_kernel,
        grid=(m // tm, nb),
        in_specs=[act, act, act,
                  pl.BlockSpec((N_BRANCH, DA_WIDTH, tn), lambda i, j: (layer, 0, j)),
                  pl.BlockSpec((tm, tn), lambda i, j: (i, gb + j)),
                  pl.BlockSpec((tm, tn), lambda i, j: (i, gb + nb + j)),
                  pl.BlockSpec((tm, tn), lambda i, j: (i, gb + 2 * nb + j))],
        out_specs=pl.BlockSpec((tm, tn), lambda i, j: (i, j)),
        out_shape=jax.ShapeDtypeStruct((m, D_MODEL), BF16),
        compiler_params=pltpu.CompilerParams(
            dimension_semantics=("parallel", "arbitrary"), vmem_limit_bytes=VMEM_LIMIT),
        name="merge_branches",
    )(o_da, o_ssm, o_gdn, wb3, p, p, p)


def _ffn_up_kernel(x_ref, wg_ref, wv_ref, cwg_ref, cwv_ref, cbg_ref, cbv_ref, o_ref,
                   ug0_ref, ug1_ref, uv0_ref, uv1_ref, wgb_ref, wvb_ref, *, rows):
    seq = x_ref.shape[0]
    nblk = seq // rows
    ug = (ug0_ref, ug1_ref)
    uv = (uv0_ref, uv1_ref)
    ug0_ref[0:HALO, :] = jnp.zeros((HALO, ug0_ref.shape[1]), F32)
    uv0_ref[0:HALO, :] = jnp.zeros((HALO, uv0_ref.shape[1]), F32)
    wgb_ref[...] = wg_ref[...].astype(BF16)
    wvb_ref[...] = wv_ref[...].astype(BF16)

    def project(b):
        xr = x_ref[b * rows:(b + 1) * rows, :]
        ug[b % 2][HALO:HALO + rows, :] = _dot(xr, wgb_ref[...])
        uv[b % 2][HALO:HALO + rows, :] = _dot(xr, wvb_ref[...])

    def conv(buf, cw_ref, cb_ref):
        acc = cb_ref[...] + cw_ref[0:1, :] * buf[HALO - 2:HALO - 2 + rows, :]
        for t in range(1, FFN_CONV):
            acc = acc + cw_ref[t:t + 1, :] * buf[HALO - 2 + t:HALO - 2 + t + rows, :]
        return acc

    def activate(b):
        gate = conv(ug[b % 2], cwg_ref, cbg_ref)
        val = conv(uv[b % 2], cwv_ref, cbv_ref)
        o_ref[b * rows:(b + 1) * rows, :] = (_silu(gate) * val).astype(o_ref.dtype)

    project(0)
    for b in range(1, nblk):
        ug[b % 2][0:HALO, :] = ug[(b - 1) % 2][rows:rows + HALO, :]
        uv[b % 2][0:HALO, :] = uv[(b - 1) % 2][rows:rows + HALO, :]
        project(b)
        activate(b - 1)
    activate(nblk - 1)


def ffn_up_proj(hn3, w_up, conv_w, conv_b, layer, *, tf=512, rows=256):
    bsz, seq, d = hn3.shape
    nf = D_FF // tf
    cb = conv_b.reshape(DEPTH, 1, 2 * D_FF)
    return pl.pallas_call(
        functools.partial(_ffn_up_kernel, rows=rows),
        grid=(bsz, nf),
        in_specs=[pl.BlockSpec((None, seq, d), lambda b, f: (b, 0, 0)),
                  pl.BlockSpec((None, d, tf), lambda b, f: (layer, 0, f)),
                  pl.BlockSpec((None, d, tf), lambda b, f: (layer, 0, nf + f)),
                  pl.BlockSpec((None, FFN_CONV, tf), lambda b, f: (layer, 0, f)),
                  pl.BlockSpec((None, FFN_CONV, tf), lambda b, f: (layer, 0, nf + f)),
                  pl.BlockSpec((None, 1, tf), lambda b, f: (layer, 0, f)),
                  pl.BlockSpec((None, 1, tf), lambda b, f: (layer, 0, nf + f))],
        out_specs=pl.BlockSpec((None, seq, tf), lambda b, f: (b, 0, f)),
        out_shape=jax.ShapeDtypeStruct((bsz, seq, D_FF), BF16),
        scratch_shapes=[pltpu.VMEM((HALO + rows, tf), F32)] * 4
                       + [pltpu.VMEM((d, tf), BF16), pltpu.VMEM((d, tf), BF16)],
        compiler_params=pltpu.CompilerParams(
            dimension_semantics=("parallel", "arbitrary"), vmem_limit_bytes=VMEM_LIMIT),
        name="ffn_up",
    )(hn3, w_up, w_up, conv_w, conv_w, cb, cb)


def _reorder_w_in(w):
    sizes = (DA_WIDTH, DA_WIDTH, DA_WIDTH, SSM_INNER, SSM_INNER + SSM_BC, SSM_HEADS,
             3 * GDN_WIDTH, GDN_WIDTH, GDN_HEADS, GDN_HEADS, N_BRANCH * D_MODEL)
    offs = [0]
    for s in sizes:
        offs.append(offs[-1] + s)
    seg = lambda i: w[:, offs[i]:offs[i + 1]]
    da_q, da_k, da_v, ssm_z, ssm_xbc, ssm_dt, gdn_qkv, gdn_z, gdn_b, gdn_a, gates = (
        seg(i) for i in range(len(sizes)))
    big = jnp.concatenate([gdn_qkv, ssm_z, gdn_z, da_q, da_k, da_v, gates, ssm_xbc], axis=1)
    small = jnp.concatenate(
        [ssm_dt, gdn_b, gdn_a,
         jnp.zeros((w.shape[0], LANES - SSM_HEADS - 2 * GDN_HEADS), w.dtype)], axis=1)
    return big.astype(BF16), small.astype(BF16)


def kernel(x, norm_mix, w_in, da_lambda, da_subln, ssm_conv_w, ssm_conv_b, ssm_dt_bias, ssm_a_log,
           ssm_d, ssm_norm, gdn_conv_w, gdn_dt_bias, gdn_a_log, gdn_norm, w_branch, w_out, norm_ffn,
           ffn_up, ffn_conv_w, ffn_conv_b, ffn_down, norm_final):
    bsz, seq, d = x.shape
    m = bsz * seq
    h = x.reshape(m, d)
    w_out_b = w_out.astype(BF16)
    ffn_down_b = ffn_down.astype(BF16)
    wb3 = w_branch.reshape(DEPTH * N_BRANCH, DA_WIDTH, D_MODEL)
    for l in range(DEPTH):
        w_big, w_small = _reorder_w_in(w_in[l])
        p, small = in_proj(h, norm_mix[l], w_big, w_small, gdn_conv_w[l], ssm_conv_w[l], ssm_conv_b[l], seq=seq)
        p3 = p.reshape(bsz, seq, P_COLS)
        small3 = small.reshape(bsz, seq, LANES)
        lambda_init = 0.8 - 0.6 * math.exp(-0.3 * l)
        o_da = diff_attention(p3, da_lambda[l], da_subln[l], lambda_init)
        o_ssm = mamba2_ssd(p3, small3, ssm_dt_bias[l], ssm_a_log[l], ssm_d[l], ssm_norm[l])
        o_gdn = gated_deltanet(p3, small3, gdn_dt_bias[l], gdn_a_log[l], gdn_norm[l])
        merged = merge_branches(o_da.reshape(m, DA_WIDTH), o_ssm.reshape(m, SSM_INNER),
                                o_gdn.reshape(m, GDN_WIDTH), wb3, l, p)
        h, hn = matmul_res_norm(merged, w_out_b, l, h, norm_ffn[l], tm=512, name="out_proj")
        act = ffn_up_proj(hn.reshape(bsz, seq, d), ffn_up, ffn_conv_w, ffn_conv_b, l)
        h = matmul_res(act.reshape(m, D_FF), ffn_down_b, l, h, tm=512, tn=1024, name="ffn_down")
    return rmsnorm(h, norm_final, out_dtype=F32).reshape(bsz, seq, d)
```

```python
import functools
import math

import jax
import jax.numpy as jnp
from jax import lax
from jax.experimental import pallas as pl
from jax.experimental.pallas import tpu as pltpu

F32 = jnp.float32
BF16 = jnp.bfloat16

D_MODEL = 2048
DEPTH = 2
DA_HEADS = 8
DA_HEAD_DIM = 64
DA_WIDTH = DA_HEADS * 2 * DA_HEAD_DIM
SSM_HEADS = 16
SSM_HEAD_DIM = 64
SSM_INNER = SSM_HEADS * SSM_HEAD_DIM
SSM_GROUPS = 2
SSM_STATE = 128
SSM_CONV = 4
SSM_CHUNK = 128
SSM_BC = 2 * SSM_GROUPS * SSM_STATE
GDN_HEADS = 8
GDN_HEAD_DIM = 128
GDN_WIDTH = GDN_HEADS * GDN_HEAD_DIM
GDN_CONV = 4
GDN_CHUNK = 64
GDN_ROWS = 256
GDN_SEQS = 1
D_FF = 5632
FFN_CONV = 3
N_BRANCH = 3
RMS_EPS = 1e-6
L2_EPS = 1e-6
LOG2E = math.log2(math.e)

LANES = 128
HALO = 8
VMEM_LIMIT = 56 * 1024 * 1024

P_GDN_QKV = 0
P_SSM_Z = P_GDN_QKV + 3 * GDN_WIDTH
P_GDN_Z = P_SSM_Z + SSM_INNER
P_DA_Q = P_GDN_Z + GDN_WIDTH
P_DA_K = P_DA_Q + DA_WIDTH
P_DA_V = P_DA_K + DA_WIDTH
P_GATES = P_DA_V + DA_WIDTH
P_SSM_X = P_GATES + N_BRANCH * D_MODEL
P_SSM_BC = P_SSM_X + SSM_INNER
P_COLS = P_SSM_BC + SSM_BC
S_DT = 0
S_BETA = SSM_HEADS
S_DECAY = SSM_HEADS + GDN_HEADS


def _silu(x):
    return x / (1.0 + jnp.exp(-x))


def _softplus(x):
    return jnp.maximum(x, 0.0) + jnp.log1p(jnp.exp(-jnp.abs(x)))


def _dot(a, b):
    return jnp.dot(a, b, preferred_element_type=F32)


def _dot_nt(a, b):
    return lax.dot_general(a, b, (((1,), (1,)), ((), ())), preferred_element_type=F32)


def _dot_f32(a, b):
    return jnp.dot(a, b, preferred_element_type=F32, precision=lax.Precision.HIGHEST)


def _rmsnorm_kernel(x_ref, w_ref, o_ref):
    x = x_ref[...]
    ms = jnp.mean(x * x, axis=-1, keepdims=True)
    o_ref[...] = (x * lax.rsqrt(ms + RMS_EPS) * w_ref[...]).astype(o_ref.dtype)


def rmsnorm(x, w, *, tm=512, out_dtype=BF16):
    m, d = x.shape
    return pl.pallas_call(
        _rmsnorm_kernel,
        grid=(m // tm,),
        in_specs=[pl.BlockSpec((tm, d), lambda i: (i, 0)),
                  pl.BlockSpec((1, d), lambda i: (0, 0))],
        out_specs=pl.BlockSpec((tm, d), lambda i: (i, 0)),
        out_shape=jax.ShapeDtypeStruct((m, d), out_dtype),
        name="rmsnorm",
    )(x, w.reshape(1, d))


IN_TM = 2048
IN_TN = 512
IN_ROWS = 256


def _in_proj_kernel(x_ref, nw_ref, w_ref, ws_ref, cwg_ref, cws_ref, cbs_ref, o_ref, small_ref,
                    xn_ref, u_ref, u2_ref, halo_ref, *, conv_front, conv_back, tiles_per_seq):
    i = pl.program_id(0)
    j = pl.program_id(1)
    tm = x_ref.shape[0]

    @pl.when(j == 0)
    def _():
        x = x_ref[...]
        ms = jnp.mean(x * x, axis=-1, keepdims=True)
        xn_ref[...] = (x * lax.rsqrt(ms + RMS_EPS) * nw_ref[...]).astype(xn_ref.dtype)
        small_ref[...] = _dot(xn_ref[...], ws_ref[...])

    is_conv = (j < conv_front) | (j >= conv_back)

    @pl.when(jnp.logical_not(is_conv))
    def _():
        o_ref[...] = _dot(xn_ref[...], w_ref[...]).astype(o_ref.dtype)

    @pl.when(is_conv)
    def _():
        first = lax.rem(i, tiles_per_seq) == 0

        @pl.when(first)
        def _():
            u_ref[0:HALO, :] = jnp.zeros((HALO, u_ref.shape[1]), F32)

        @pl.when(jnp.logical_not(first))
        def _():
            u_ref[0:HALO, :] = halo_ref[j]

        bufs = (u_ref, u2_ref)
        nblk = tm // IN_ROWS
        front = j < conv_front
        cw = jnp.where(front, cwg_ref[...], cws_ref[...])
        cb = jnp.where(front, 0.0, cbs_ref[...])

        def project(b):
            bufs[b % 2][HALO:HALO + IN_ROWS, :] = _dot(xn_ref[b * IN_ROWS:(b + 1) * IN_ROWS, :], w_ref[...])

        def activate(b):
            buf = bufs[b % 2]
            acc = cb + cw[0:1, :] * buf[HALO - 3:HALO - 3 + IN_ROWS, :]
            for t in range(1, SSM_CONV):
                acc = acc + cw[t:t + 1, :] * buf[HALO - 3 + t:HALO - 3 + t + IN_ROWS, :]
            o_ref[b * IN_ROWS:(b + 1) * IN_ROWS, :] = _silu(acc).astype(o_ref.dtype)

        project(0)
        for b in range(1, nblk):
            bufs[b % 2][0:HALO, :] = bufs[(b - 1) % 2][IN_ROWS:IN_ROWS + HALO, :]
            project(b)
            activate(b - 1)
        activate(nblk - 1)
        halo_ref[j] = bufs[(nblk - 1) % 2][IN_ROWS:IN_ROWS + HALO, :]


def in_proj(x, nw, w, w_small, gdn_cw, ssm_cw, ssm_cb, *, seq):
    m, k = x.shape
    assert GDN_CONV == SSM_CONV and P_GDN_QKV == 0 and P_SSM_X % IN_TN == 0 and (3 * GDN_WIDTH) % IN_TN == 0
    n_tiles = P_COLS // IN_TN
    front = 3 * GDN_WIDTH // IN_TN
    back = P_SSM_X // IN_TN
    gdn_tile = lambda i, j: (0, jnp.minimum(j, front - 1))
    ssm_tile = lambda i, j: (0, jnp.clip(j - back, 0, n_tiles - back - 1))
    return pl.pallas_call(
        functools.partial(_in_proj_kernel, conv_front=front, conv_back=back, tiles_per_seq=seq // IN_TM),
        grid=(m // IN_TM, n_tiles),
        in_specs=[pl.BlockSpec((IN_TM, k), lambda i, j: (i, 0), pipeline_mode=pl.Buffered(1)),
                  pl.BlockSpec((1, k), lambda i, j: (0, 0)),
                  pl.BlockSpec((k, IN_TN), lambda i, j: (0, j)),
                  pl.BlockSpec((k, LANES), lambda i, j: (0, 0)),
                  pl.BlockSpec((GDN_CONV, IN_TN), gdn_tile),
                  pl.BlockSpec((SSM_CONV, IN_TN), ssm_tile),
                  pl.BlockSpec((1, IN_TN), ssm_tile)],
        out_specs=[pl.BlockSpec((IN_TM, IN_TN), lambda i, j: (i, j)),
                   pl.BlockSpec((IN_TM, LANES), lambda i, j: (i, 0))],
        out_shape=[jax.ShapeDtypeStruct((m, P_COLS), BF16), jax.ShapeDtypeStruct((m, LANES), F32)],
        scratch_shapes=[pltpu.VMEM((IN_TM, k), BF16),
                        pltpu.VMEM((HALO + IN_ROWS, IN_TN), F32),
                        pltpu.VMEM((HALO + IN_ROWS, IN_TN), F32),
                        pltpu.VMEM((n_tiles, HALO, IN_TN), F32)],
        compiler_params=pltpu.CompilerParams(
            dimension_semantics=("arbitrary", "arbitrary"), vmem_limit_bytes=VMEM_LIMIT),
        name="in_proj",
    )(x, nw.reshape(1, k), w, w_small, gdn_cw, ssm_cw, ssm_cb.reshape(1, -1))


def _matmul_res_norm_kernel(a_ref, w_ref, h_ref, nw_ref, ho_ref, no_ref):
    h = h_ref[...] + _dot(a_ref[...], w_ref[...])
    ho_ref[...] = h
    ms = jnp.mean(h * h, axis=-1, keepdims=True)
    no_ref[...] = (h * lax.rsqrt(ms + RMS_EPS) * nw_ref[...]).astype(no_ref.dtype)


def matmul_res_norm(a, w, layer, h, nw, *, tm, name):
    m, k = a.shape
    _, _, n = w.shape
    row = lambda i: (i, 0)
    return pl.pallas_call(
        _matmul_res_norm_kernel,
        grid=(m // tm,),
        in_specs=[pl.BlockSpec((tm, k), row),
                  pl.BlockSpec((None, k, n), lambda i: (layer, 0, 0)),
                  pl.BlockSpec((tm, n), row),
                  pl.BlockSpec((1, n), lambda i: (0, 0))],
        out_specs=[pl.BlockSpec((tm, n), row), pl.BlockSpec((tm, n), row)],
        out_shape=[jax.ShapeDtypeStruct((m, n), F32), jax.ShapeDtypeStruct((m, n), BF16)],
        compiler_params=pltpu.CompilerParams(
            dimension_semantics=("parallel",), vmem_limit_bytes=VMEM_LIMIT),
        name=name,
    )(a, w, h, nw.reshape(1, n))


def _matmul_res_kernel(a_ref, w_ref, h_ref, o_ref):
    o_ref[...] = h_ref[...] + _dot(a_ref[...], w_ref[...])


def matmul_res(a, w, layer, h, *, tm, tn, name):
    m, k = a.shape
    _, _, n = w.shape
    return pl.pallas_call(
        _matmul_res_kernel,
        grid=(n // tn, m // tm),
        in_specs=[pl.BlockSpec((tm, k), lambda j, i: (i, 0)),
                  pl.BlockSpec((None, k, tn), lambda j, i: (layer, 0, j)),
                  pl.BlockSpec((tm, tn), lambda j, i: (i, j))],
        out_specs=pl.BlockSpec((tm, tn), lambda j, i: (i, j)),
        out_shape=jax.ShapeDtypeStruct((m, n), F32),
        compiler_params=pltpu.CompilerParams(
            dimension_semantics=("parallel", "parallel"), vmem_limit_bytes=VMEM_LIMIT),
        name=name,
    )(a, w, h)


DA_BIAS_LANES = 3


def _diff_attn_kernel(slopes_ref, lamp_ref, subln_ref, q_ref, k_ref, v_ref, o_ref,
                      k1_ref, k2_ref, s_ref, st_ref, p_ref, *, tq, lambda_init):
    hd = DA_HEAD_DIM
    seq = q_ref.shape[0]
    nq = seq // tq
    half = tq // 2
    slope = slopes_ref[pl.program_id(1)]
    lp = lamp_ref[...]
    lam = (jnp.exp(jnp.sum(lp[0:1] * lp[1:2], axis=-1, keepdims=True))
           - jnp.exp(jnp.sum(lp[2:3] * lp[3:4], axis=-1, keepdims=True)) + lambda_init)

    lane_k = lax.broadcasted_iota(jnp.int32, (seq, 2 * hd), 1)
    bias = (slope * LOG2E) * lax.broadcasted_iota(jnp.int32, (seq, 2 * hd), 0).astype(F32)
    pieces = []
    rest = bias
    for _ in range(DA_BIAS_LANES):
        piece = rest.astype(BF16).astype(F32)
        pieces.append(piece)
        rest = rest - piece
    kf = k_ref[...].astype(F32)
    k1 = jnp.where(lane_k < hd, kf, 0.0)
    k2 = jnp.where(lane_k >= hd, kf, 0.0)
    for t, piece in enumerate(pieces):
        k1 = jnp.where(lane_k == hd + t, piece, k1)
        k2 = jnp.where(lane_k == t, piece, k2)
    k1_ref[...] = k1.astype(BF16)
    k2_ref[...] = k2.astype(BF16)

    lane_q = lax.broadcasted_iota(jnp.int32, (tq, 2 * hd), 1)
    causal = (lax.broadcasted_iota(jnp.int32, (tq, tq), 0)
              >= lax.broadcasted_iota(jnp.int32, (tq, tq), 1))

    def fold(x):
        return x[:, :half], x[:, half:]

    for qi in range(nq):
        rows = slice(qi * tq, (qi + 1) * tq)
        qf = q_ref[rows, :].astype(F32) * (hd ** -0.5 * LOG2E)
        q1 = jnp.where(lane_q < hd, qf, jnp.where(lane_q < hd + DA_BIAS_LANES, 1.0, 0.0)).astype(BF16)
        q2 = jnp.where(lane_q >= hd, qf, jnp.where(lane_q < DA_BIAS_LANES, 1.0, 0.0)).astype(BF16)

        mx = [None, None]
        for j in range(qi + 1):
            keys = slice(j * tq, (j + 1) * tq)
            for mp, (qm, km_ref) in enumerate(((q1, k1_ref), (q2, k2_ref))):
                s = _dot_nt(qm, km_ref[keys, :])
                if j == qi:
                    s = jnp.where(causal, s, -jnp.inf)
                s_ref[mp, j] = s
                a, b = fold(s)
                ab = jnp.maximum(a, b)
                mx[mp] = ab if mx[mp] is None else jnp.maximum(mx[mp], ab)
        for mp in range(2):
            st_ref[mp] = jnp.broadcast_to(jnp.max(mx[mp], axis=-1, keepdims=True), (tq, half))

        sm = [None, None]
        for j in range(qi + 1):
            for mp in range(2):
                m = st_ref[mp]
                a, b = fold(s_ref[mp, j])
                pa = jnp.exp2(a - m)
                pb = jnp.exp2(b - m)
                p_ref[mp, :, j * tq:(j + 1) * tq] = jnp.concatenate([pa, pb], axis=1).astype(BF16)
                sm[mp] = pa + pb if sm[mp] is None else sm[mp] + (pa + pb)
        l1 = jnp.sum(sm[0], axis=-1, keepdims=True)
        l2 = jnp.sum(sm[1], axis=-1, keepdims=True)

        kv = (qi + 1) * tq
        o = (_dot(p_ref[0, :, :kv], v_ref[:kv, :]) * (1.0 / l1)
             - _dot(p_ref[1, :, :kv], v_ref[:kv, :]) * (lam / l2))
        ms = jnp.mean(o * o, axis=-1, keepdims=True)
        o = o * lax.rsqrt(ms + RMS_EPS) * subln_ref[...] * (1.0 - lambda_init)
        o_ref[rows, :] = o.astype(o_ref.dtype)


def diff_attention(p3, lam_params, subln_w, lambda_init, *, tq=256):
    bsz, seq, _ = p3.shape
    hw = 2 * DA_HEAD_DIM
    nq = seq // tq
    slopes = jnp.asarray([2.0 ** (-8.0 * (h + 1) / DA_HEADS) for h in range(DA_HEADS)], F32)
    qb, kb, vb = P_DA_Q // hw, P_DA_K // hw, P_DA_V // hw
    return pl.pallas_call(
        functools.partial(_diff_attn_kernel, tq=tq, lambda_init=lambda_init),
        grid=(bsz, DA_HEADS),
        in_specs=[pl.BlockSpec(memory_space=pltpu.SMEM),
                  pl.BlockSpec((4, DA_HEAD_DIM), lambda b, h: (0, 0)),
                  pl.BlockSpec((1, hw), lambda b, h: (0, 0)),
                  pl.BlockSpec((None, seq, hw), lambda b, h: (b, 0, qb + h)),
                  pl.BlockSpec((None, seq, hw), lambda b, h: (b, 0, kb + h)),
                  pl.BlockSpec((None, seq, hw), lambda b, h: (b, 0, vb + h))],
        out_specs=pl.BlockSpec((None, seq, hw), lambda b, h: (b, 0, h)),
        out_shape=jax.ShapeDtypeStruct((bsz, seq, DA_WIDTH), BF16),
        scratch_shapes=[pltpu.VMEM((seq, hw), BF16),
                        pltpu.VMEM((seq, hw), BF16),
                        pltpu.VMEM((2, nq, tq, tq), F32),
                        pltpu.VMEM((2, tq, tq // 2), F32),
                        pltpu.VMEM((2, tq, seq), BF16)],
        compiler_params=pltpu.CompilerParams(
            dimension_semantics=("parallel", "parallel"), vmem_limit_bytes=VMEM_LIMIT),
        name="diff_attention",
    )(slopes, lam_params, subln_w.reshape(1, hw), p3, p3, p3)


def _ssd_kernel(x_ref, bc_ref, z_ref, sm_ref, dtb_ref, alog_ref, dskip_ref, nw_ref,
                o_ref, state_ref, y_ref):
    q = SSM_CHUNK
    c = pl.program_id(1)

    @pl.when(c == 0)
    def _():
        state_ref[...] = jnp.zeros_like(state_ref)

    xs_b = x_ref[...]
    xs = xs_b.astype(F32)
    bc = bc_ref[...]

    dt = _softplus(sm_ref[...] + dtb_ref[...])
    da = dt * (-jnp.exp(alog_ref[...]))
    row = lax.broadcasted_iota(jnp.int32, (q, q), 0)
    colm = lax.broadcasted_iota(jnp.int32, (q, q), 1)
    causal = row >= colm
    tri = causal.astype(F32)
    a_cs = _dot_f32(tri, da)
    a_cs_t = a_cs.T
    dt_t = dt.T
    lane = lax.broadcasted_iota(jnp.int32, (1, LANES), 1)

    for g in range(SSM_GROUPS):
        bm_b = bc[:, g * SSM_STATE:(g + 1) * SSM_STATE]
        cm_b = bc[:, (SSM_GROUPS + g) * SSM_STATE:(SSM_GROUPS + g + 1) * SSM_STATE]
        cm = cm_b.astype(F32)
        cb = _dot_nt(cm_b, bm_b)
        bm_t = bm_b.astype(F32).T
        hpg = SSM_HEADS // SSM_GROUPS
        for pr in range(hpg // 2):
            pair = g * (hpg // 2) + pr
            x_pair = xs_b[:, pair * LANES:(pair + 1) * LANES]
            st = state_ref[pair]
            rhs = jnp.concatenate([x_pair, st.astype(BF16)], axis=0)
            ys, sts, cds = [], [], []
            for sub in range(2):
                h = 2 * pair + sub
                acol = a_cs[:, h:h + 1]
                arow = a_cs_t[h:h + 1, :]
                dtrow = dt_t[h:h + 1, :]
                decay = jnp.exp(jnp.where(causal, acol - arow, -jnp.inf))
                sc = (cb * decay * dtrow).astype(BF16)
                c_in = (cm * jnp.exp(acol)).astype(BF16)
                ys.append(_dot(jnp.concatenate([sc, c_in], axis=1), rhs))
                a_last = arow[:, q - 1:q]
                wrow = jnp.exp(a_last - arow) * dtrow
                sts.append(_dot((bm_t * wrow).astype(BF16), x_pair))
                cds.append(jnp.exp(a_last))
            first = lane < SSM_HEAD_DIM
            y_ref[:, pair * LANES:(pair + 1) * LANES] = jnp.where(first, ys[0], ys[1])
            state_ref[pair] = (st * jnp.where(first, cds[0], cds[1])
                               + jnp.where(first, sts[0], sts[1]))

    y = y_ref[...] + xs * dskip_ref[...]
    y = y * _silu(z_ref[...].astype(F32))
    gw = SSM_INNER // SSM_GROUPS
    for g in range(SSM_GROUPS):
        yg = y[:, g * gw:(g + 1) * gw]
        ms = jnp.mean(yg * yg, axis=-1, keepdims=True)
        o_ref[:, g * gw:(g + 1) * gw] = (
            yg * lax.rsqrt(ms + RMS_EPS) * nw_ref[:, g * gw:(g + 1) * gw]).astype(o_ref.dtype)


def _pad_row(v, offset):
    return jnp.zeros((1, LANES), F32).at[0, offset:offset + v.shape[0]].set(v.astype(F32))


def mamba2_ssd(p3, small3, dt_bias, a_log, d_skip, norm_w):
    bsz, seq, _ = p3.shape
    q = SSM_CHUNK
    nc = seq // q
    const = lambda b, c: (0, 0)
    return pl.pallas_call(
        _ssd_kernel,
        grid=(bsz, nc),
        in_specs=[pl.BlockSpec((None, q, SSM_INNER), lambda b, c: (b, c, P_SSM_X // SSM_INNER)),
                  pl.BlockSpec((None, q, SSM_BC), lambda b, c: (b, c, P_SSM_BC // SSM_BC)),
                  pl.BlockSpec((None, q, SSM_INNER), lambda b, c: (b, c, P_SSM_Z // SSM_INNER)),
                  pl.BlockSpec((None, q, LANES), lambda b, c: (b, c, 0)),
                  pl.BlockSpec((1, LANES), const),
                  pl.BlockSpec((1, LANES), const),
                  pl.BlockSpec((1, SSM_INNER), const),
                  pl.BlockSpec((1, SSM_INNER), const)],
        out_specs=pl.BlockSpec((None, q, SSM_INNER), lambda b, c: (b, c, 0)),
        out_shape=jax.ShapeDtypeStruct((bsz, seq, SSM_INNER), BF16),
        scratch_shapes=[pltpu.VMEM((SSM_HEADS // 2, SSM_STATE, LANES), F32),
                        pltpu.VMEM((q, SSM_INNER), F32)],
        compiler_params=pltpu.CompilerParams(
            dimension_semantics=("parallel", "arbitrary"), vmem_limit_bytes=VMEM_LIMIT),
        name="mamba2_ssd",
    )(p3, p3, p3, small3, _pad_row(dt_bias, S_DT),
      _pad_row(a_log, S_DT), jnp.repeat(d_skip.astype(F32), SSM_HEAD_DIM).reshape(1, SSM_INNER),
      norm_w.reshape(1, SSM_INNER))


def _gdn_kernel(act_ref, z_ref, sm_ref, dtb_ref, alog_ref, nw_ref, o_ref, state_ref):
    cs = GDN_CHUNK
    rr = GDN_ROWS
    nch = rr // cs
    d = GDN_HEAD_DIM
    sh = int(math.log2(cs))
    step = pl.program_id(1)

    @pl.when(step == 0)
    def _():
        state_ref[...] = jnp.zeros_like(state_ref)

    nseq = act_ref.shape[0]
    row = lax.broadcasted_iota(jnp.int32, (rr, rr), 0)
    colm = lax.broadcasted_iota(jnp.int32, (rr, rr), 1)
    same_blk = lax.shift_right_logical(row, sh) == lax.shift_right_logical(colm, sh)
    tri_bd = (same_blk & (row >= colm)).astype(F32)
    beta, g_cs, g_t, beta_t = [], [], [], []
    for sq in range(nseq):
        sm = sm_ref[sq]
        beta.append(1.0 / (1.0 + jnp.exp(-sm)))
        gl = -jnp.exp(alog_ref[...]) * _softplus(sm + dtb_ref[...])
        g_cs.append(_dot_f32(tri_bd, gl))
        g_t.append(g_cs[sq].T)
        beta_t.append(beta[sq].T)

    l_idx = lax.broadcasted_iota(jnp.int32, (cs, rr), 0)
    j_idx = lax.broadcasted_iota(jnp.int32, (cs, rr), 1)
    s_idx = jnp.bitwise_and(j_idx, cs - 1)
    blk = lax.shift_right_logical(j_idx, sh)
    incl_cat = l_idx >= s_idx
    strict_cat = l_idx > s_idx
    eye_cat = (l_idx == s_idx).astype(F32)
    blk_row = lax.shift_right_logical(lax.broadcasted_iota(jnp.int32, (1, rr), 1), sh)

    def to_cat(x):
        if x.shape[1] != rr:
            x = jnp.concatenate([x] * (rr // x.shape[1]), axis=1)
        out = x[(nch - 1) * cs:]
        for c in reversed(range(nch - 1)):
            out = jnp.where(blk == c, x[c * cs:(c + 1) * cs], out)
        return out

    def to_bd(x_cat):
        return jnp.where(same_blk, jnp.concatenate([x_cat] * nch, axis=0), jnp.zeros((), x_cat.dtype))

    pairs = [(sq, h) for sq in range(nseq) for h in range(GDN_HEADS)]
    items = range(len(pairs))
    qn, kn_b, v_b, kn_t, attn_cat, p_cat, t_cat, grow, eg_full = ([None] * len(pairs) for _ in range(9))
    for i in items:
        sq, h = pairs[i]
        qh = act_ref[sq, :, h * d:(h + 1) * d].astype(F32)
        kh = act_ref[sq, :, GDN_WIDTH + h * d:GDN_WIDTH + (h + 1) * d].astype(F32)
        qn[i] = (qh * lax.rsqrt(jnp.sum(qh * qh, axis=-1, keepdims=True) + L2_EPS) * (d ** -0.5)).astype(BF16)
        kn = kh * lax.rsqrt(jnp.sum(kh * kh, axis=-1, keepdims=True) + L2_EPS)
        kn_b[i] = kn.astype(BF16)
        kn_t[i] = kn.T
        v_b[i] = act_ref[sq, :, 2 * GDN_WIDTH + h * d:2 * GDN_WIDTH + (h + 1) * d]
        gcol_full = jnp.broadcast_to(g_cs[sq][:, S_DECAY + h:S_DECAY + h + 1], (rr, LANES))
        bcol_full = jnp.broadcast_to(beta[sq][:, S_BETA + h:S_BETA + h + 1], (rr, LANES))
        eg_full[i] = jnp.exp(gcol_full)
        grow[i] = g_t[sq][S_DECAY + h:S_DECAY + h + 1, :]
        dec = jnp.exp(jnp.where(incl_cat, to_cat(gcol_full) - grow[i], 0.0))
        kk = to_cat(_dot_nt(kn_b[i], kn_b[i]))
        qk = to_cat(_dot_nt(qn[i], kn_b[i]))
        attn_cat[i] = jnp.where(incl_cat, qk * dec, 0.0)
        p_cat[i] = jnp.where(strict_cat, -(kk * dec * to_cat(bcol_full)), 0.0)
        t_cat[i] = eye_cat + p_cat[i]

    p_bd = [to_bd(p_cat[i].astype(BF16)) for i in items]
    for _ in range(sh - 1):
        for i in items:
            p_cat[i] = _dot(p_cat[i].astype(BF16), p_bd[i])
        for i in items:
            p_bd[i] = to_bd(p_cat[i].astype(BF16))
        for i in items:
            t_cat[i] = t_cat[i] + _dot(t_cat[i].astype(BF16), p_bd[i])

    u, w = [None] * len(pairs), [None] * len(pairs)
    for i in items:
        sq, h = pairs[i]
        brow = beta_t[sq][S_BETA + h:S_BETA + h + 1, :]
        u[i] = _dot(to_bd((t_cat[i] * brow).astype(BF16)), v_b[i])
        w[i] = _dot(to_bd((t_cat[i] * (brow * jnp.exp(grow[i]))).astype(BF16)), kn_b[i])

    st = [state_ref[i] for i in items]
    zero_blk = jnp.zeros((cs, d), BF16)
    for c in range(nch):
        rows = slice(c * cs, (c + 1) * cs)
        for i in items:
            sq, h = pairs[i]
            ws = _dot(jnp.concatenate([w[i][rows].astype(BF16), qn[i][rows]], axis=0), st[i].astype(BF16))
            v_new = (u[i][rows] - ws[:cs]).astype(BF16)
            rhs = jnp.concatenate([zero_blk] * c + [v_new] + [zero_blk] * (nch - 1 - c), axis=0)
            g_last = grow[i][:, (c + 1) * cs - 1:(c + 1) * cs]
            e_row = jnp.exp(jnp.where(blk_row == c, g_last - grow[i], 0.0))
            lhs = jnp.concatenate(
                [jnp.where(blk == c, attn_cat[i], 0.0),
                 jnp.where(blk_row == c, kn_t[i] * e_row, 0.0)], axis=0).astype(BF16)
            r = _dot(lhs, rhs)
            o = eg_full[i][rows] * ws[cs:] + r[:cs]
            st[i] = st[i] * jnp.exp(g_last) + r[cs:]
            ms = jnp.mean(o * o, axis=-1, keepdims=True)
            zh = z_ref[sq, rows, h * d:(h + 1) * d].astype(F32)
            o_ref[sq, rows, h * d:(h + 1) * d] = (
                o * lax.rsqrt(ms + RMS_EPS) * nw_ref[...] * _silu(zh)).astype(o_ref.dtype)
    for i in items:
        state_ref[i] = st[i]


def gated_deltanet(p3, small3, dt_bias, a_log, norm_w):
    bsz, seq, _ = p3.shape
    rr = GDN_ROWS
    ns = GDN_SEQS if bsz % GDN_SEQS == 0 else 1
    w3 = 3 * GDN_WIDTH
    const = lambda b, c: (0, 0)
    return pl.pallas_call(
        _gdn_kernel,
        grid=(bsz // ns, seq // rr),
        in_specs=[pl.BlockSpec((ns, rr, w3), lambda b, c: (b, c, P_GDN_QKV // w3)),
                  pl.BlockSpec((ns, rr, GDN_WIDTH), lambda b, c: (b, c, P_GDN_Z // GDN_WIDTH)),
                  pl.BlockSpec((ns, rr, LANES), lambda b, c: (b, c, 0)),
                  pl.BlockSpec((1, LANES), const),
                  pl.BlockSpec((1, LANES), const),
                  pl.BlockSpec((1, GDN_HEAD_DIM), const)],
        out_specs=pl.BlockSpec((ns, rr, GDN_WIDTH), lambda b, c: (b, c, 0)),
        out_shape=jax.ShapeDtypeStruct((bsz, seq, GDN_WIDTH), BF16),
        scratch_shapes=[pltpu.VMEM((ns * GDN_HEADS, GDN_HEAD_DIM, GDN_HEAD_DIM), F32)],
        compiler_params=pltpu.CompilerParams(
            dimension_semantics=("parallel", "arbitrary"), vmem_limit_bytes=VMEM_LIMIT),
        name="gated_deltanet",
    )(p3, p3, small3, _pad_row(dt_bias, S_DECAY), _pad_row(a_log, S_DECAY),
      norm_w.reshape(1, GDN_HEAD_DIM))


def _merge_kernel(oa_ref, os_ref, og_ref, wb_ref, g0_ref, g1_ref, g2_ref, o_ref):
    acc = None
    for br, (x_ref, g_ref) in enumerate(((oa_ref, g0_ref), (os_ref, g1_ref), (og_ref, g2_ref))):
        gate = 1.0 / (1.0 + jnp.exp(-g_ref[...].astype(F32)))
        term = gate * _dot(x_ref[...], wb_ref[br].astype(BF16))
        acc = term if acc is None else acc + term
    o_ref[...] = acc.astype(o_ref.dtype)


def merge_branches(o_da, o_ssm, o_gdn, wb3, layer, p, *, tm=1024, tn=512):
    m = o_da.shape[0]
    gb = P_GATES // tn
    nb = D_MODEL // tn
    act = pl.BlockSpec((tm, DA_WIDTH), lambda i, j: (i, 0))
    return pl.pallas_call(
        _merge_kernel,
        grid=(m // tm, nb),
        in_specs=[act, act, act,
                  pl.BlockSpec((N_BRANCH, DA_WIDTH, tn), lambda i, j: (layer, 0, j)),
                  pl.BlockSpec((tm, tn), lambda i, j: (i, gb + j)),
                  pl.BlockSpec((tm, tn), lambda i, j: (i, gb + nb + j)),
                  pl.BlockSpec((tm, tn), lambda i, j: (i, gb + 2 * nb + j))],
        out_specs=pl.BlockSpec((tm, tn), lambda i, j: (i, j)),
        out_shape=jax.ShapeDtypeStruct((m, D_MODEL), BF16),
        compiler_params=pltpu.CompilerParams(
            dimension_semantics=("parallel", "arbitrary"), vmem_limit_bytes=VMEM_LIMIT),
        name="merge_branches",
    )(o_da, o_ssm, o_gdn, wb3, p, p, p)


def _ffn_up_kernel(x_ref, wg_ref, wv_ref, cwg_ref, cwv_ref, cbg_ref, cbv_ref, o_ref,
                   ug0_ref, ug1_ref, uv0_ref, uv1_ref, wgb_ref, wvb_ref, *, rows):
    seq = x_ref.shape[0]
    nblk = seq // rows
    ug = (ug0_ref, ug1_ref)
    uv = (uv0_ref, uv1_ref)
    ug0_ref[0:HALO, :] = jnp.zeros((HALO, ug0_ref.shape[1]), F32)
    uv0_ref[0:HALO, :] = jnp.zeros((HALO, uv0_ref.shape[1]), F32)
    wgb_ref[...] = wg_ref[...].astype(BF16)
    wvb_ref[...] = wv_ref[...].astype(BF16)

    def project(b):
        xr = x_ref[b * rows:(b + 1) * rows, :]
        ug[b % 2][HALO:HALO + rows, :] = _dot(xr, wgb_ref[...])
        uv[b % 2][HALO:HALO + rows, :] = _dot(xr, wvb_ref[...])

    def conv(buf, cw_ref, cb_ref):
        acc = cb_ref[...] + cw_ref[0:1, :] * buf[HALO - 2:HALO - 2 + rows, :]
        for t in range(1, FFN_CONV):
            acc = acc + cw_ref[t:t + 1, :] * buf[HALO - 2 + t:HALO - 2 + t + rows, :]
        return acc

    def activate(b):
        gate = conv(ug[b % 2], cwg_ref, cbg_ref)
        val = conv(uv[b % 2], cwv_ref, cbv_ref)
        o_ref[b * rows:(b + 1) * rows, :] = (_silu(gate) * val).astype(o_ref.dtype)

    project(0)
    for b in range(1, nblk):
        ug[b % 2][0:HALO, :] = ug[(b - 1) % 2][rows:rows + HALO, :]
        uv[b % 2][0:HALO, :] = uv[(b - 1) % 2][rows:rows + HALO, :]
        project(b)
        activate(b - 1)
    activate(nblk - 1)


def ffn_up_proj(hn3, w_up, conv_w, conv_b, layer, *, tf=512, rows=256):
    bsz, seq, d = hn3.shape
    nf = D_FF // tf
    cb = conv_b.reshape(DEPTH, 1, 2 * D_FF)
    return pl.pallas_call(
        functools.partial(_ffn_up_kernel, rows=rows),
        grid=(bsz, nf),
        in_specs=[pl.BlockSpec((None, seq, d), lambda b, f: (b, 0, 0)),
                  pl.BlockSpec((None, d, tf), lambda b, f: (layer, 0, f)),
                  pl.BlockSpec((None, d, tf), lambda b, f: (layer, 0, nf + f)),
                  pl.BlockSpec((None, FFN_CONV, tf), lambda b, f: (layer, 0, f)),
                  pl.BlockSpec((None, FFN_CONV, tf), lambda b, f: (layer, 0, nf + f)),
                  pl.BlockSpec((None, 1, tf), lambda b, f: (layer, 0, f)),
                  pl.BlockSpec((None, 1, tf), lambda b, f: (layer, 0, nf + f))],
        out_specs=pl.BlockSpec((None, seq, tf), lambda b, f: (b, 0, f)),
        out_shape=jax.ShapeDtypeStruct((bsz, seq, D_FF), BF16),
        scratch_shapes=[pltpu.VMEM((HALO + rows, tf), F32)] * 4
                       + [pltpu.VMEM((d, tf), BF16), pltpu.VMEM((d, tf), BF16)],
        compiler_params=pltpu.CompilerParams(
            dimension_semantics=("parallel", "arbitrary"), vmem_limit_bytes=VMEM_LIMIT),
        name="ffn_up",
    )(hn3, w_up, w_up, conv_w, conv_w, cb, cb)


def _reorder_w_in(w):
    sizes = (DA_WIDTH, DA_WIDTH, DA_WIDTH, SSM_INNER, SSM_INNER + SSM_BC, SSM_HEADS,
             3 * GDN_WIDTH, GDN_WIDTH, GDN_HEADS, GDN_HEADS, N_BRANCH * D_MODEL)
    offs = [0]
    for s in sizes:
        offs.append(offs[-1] + s)
    seg = lambda i: w[:, offs[i]:offs[i + 1]]
    da_q, da_k, da_v, ssm_z, ssm_xbc, ssm_dt, gdn_qkv, gdn_z, gdn_b, gdn_a, gates = (
        seg(i) for i in range(len(sizes)))
    big = jnp.concatenate([gdn_qkv, ssm_z, gdn_z, da_q, da_k, da_v, gates, ssm_xbc], axis=1)
    small = jnp.concatenate(
        [ssm_dt, gdn_b, gdn_a,
         jnp.zeros((w.shape[0], LANES - SSM_HEADS - 2 * GDN_HEADS), w.dtype)], axis=1)
    return big.astype(BF16), small.astype(BF16)


def kernel(x, norm_mix, w_in, da_lambda, da_subln, ssm_conv_w, ssm_conv_b, ssm_dt_bias, ssm_a_log,
           ssm_d, ssm_norm, gdn_conv_w, gdn_dt_bias, gdn_a_log, gdn_norm, w_branch, w_out, norm_ffn,
           ffn_up, ffn_conv_w, ffn_conv_b, ffn_down, norm_final):
    bsz, seq, d = x.shape
    m = bsz * seq
    h = x.reshape(m, d)
    w_out_b = w_out.astype(BF16)
    ffn_down_b = ffn_down.astype(BF16)
    wb3 = w_branch.reshape(DEPTH * N_BRANCH, DA_WIDTH, D_MODEL)
    for l in range(DEPTH):
        w_big, w_small = _reorder_w_in(w_in[l])
        p, small = in_proj(h, norm_mix[l], w_big, w_small, gdn_conv_w[l], ssm_conv_w[l], ssm_conv_b[l], seq=seq)
        p3 = p.reshape(bsz, seq, P_COLS)
        small3 = small.reshape(bsz, seq, LANES)
        lambda_init = 0.8 - 0.6 * math.exp(-0.3 * l)
        o_da = diff_attention(p3, da_lambda[l], da_subln[l], lambda_init)
        o_ssm = mamba2_ssd(p3, small3, ssm_dt_bias[l], ssm_a_log[l], ssm_d[l], ssm_norm[l])
        o_gdn = gated_deltanet(p3, small3, gdn_dt_bias[l], gdn_a_log[l], gdn_norm[l])
        merged = merge_branches(o_da.reshape(m, DA_WIDTH), o_ssm.reshape(m, SSM_INNER),
                                o_gdn.reshape(m, GDN_WIDTH), wb3, l, p)
        h, hn = matmul_res_norm(merged, w_out_b, l, h, norm_ffn[l], tm=512, name="out_proj")
        act = ffn_up_proj(hn.reshape(bsz, seq, d), ffn_up, ffn_conv_w, ffn_conv_b, l)
        h = matmul_res(act.reshape(m, D_FF), ffn_down_b, l, h, tm=512, tn=1024, name="ffn_down")
    return rmsnorm(h, norm_final, out_dtype=F32).reshape(bsz, seq, d)
```
